```python
import math
import jax, jax.numpy as jnp
from jax import lax
import numpy as np

D_MODEL = 4096
BATCH = 8
SEQ = 4096
DEPTH = 1

CONV_DIM = D_MODEL // 2
CONV_GROUPS = 16
CONV_WIDTH = 3
N_HEADS = 16
QK_NOPE_DIM = 128
QK_ROPE_DIM = 64
V_HEAD_DIM = 128
QK_HEAD_DIM = QK_NOPE_DIM + QK_ROPE_DIM
ATTN_DIM = N_HEADS * V_HEAD_DIM
Q_LORA_RANK = 1024
KV_LORA_RANK = 512
N_BRANCHES = 2
D_FF = ((8 * D_MODEL // 3 + 255) // 256) * 256
ROPE_THETA = 10000.0
RMS_EPS = 1e-6
Q_BLOCK = 128
SOFTMAX_SCALE = 1.0 / math.sqrt(QK_HEAD_DIM)

IN_SPLITS = (CONV_DIM, CONV_DIM, CONV_DIM, Q_LORA_RANK, KV_LORA_RANK, QK_ROPE_DIM,
             N_BRANCHES * D_MODEL)
IN_COLS = CONV_DIM * 3 + Q_LORA_RANK + KV_LORA_RANK + QK_ROPE_DIM + N_BRANCHES * D_MODEL

kernel_name = "hybrid_shortconv_mla_gated_encoder"


def split_columns(z, widths):
    parts = []
    start = 0
    for wdt in widths:
        parts.append(z[..., start:start + wdt])
        start += wdt
    return parts


def rms_norm(x, g):
    xf = x.astype(jnp.float32)
    inv = lax.rsqrt(jnp.mean(xf * xf, axis=-1, keepdims=True) + RMS_EPS)
    return (xf * inv * g.astype(jnp.float32)).astype(x.dtype)


def centred_short_conv(u, w):
    up = jnp.pad(u, ((0, 0), (1, 1), (0, 0)))
    return up[:, :-2] * w[0] + up[:, 1:-1] * w[1] + up[:, 2:] * w[2]


def rotary(t, cos, sin):
    half = t.shape[-1] // 2
    t1, t2 = t[..., :half], t[..., half:]
    return jnp.concatenate([t1 * cos - t2 * sin, t1 * sin + t2 * cos], axis=-1)


def mla_attention(q_nope, q_rope, k_nope, k_rope, v):
    b, s, h, _ = q_nope.shape
    nblk = s // Q_BLOCK

    def to_blocks(t):
        return jnp.swapaxes(t.reshape((b, nblk, Q_BLOCK) + t.shape[2:]), 0, 1)

    def block(qs):
        qn, qr = qs
        sc = jnp.einsum('bqhd,bkhd->bhqk', qn, k_nope).astype(jnp.float32)
        sc = sc + jnp.einsum('bqhr,bkr->bhqk', qr, k_rope).astype(jnp.float32)
        p = jax.nn.softmax(sc * SOFTMAX_SCALE, axis=-1).astype(v.dtype)
        return jnp.einsum('bhqk,bkhd->bqhd', p, v)

    out = lax.map(block, (to_blocks(q_nope), to_blocks(q_rope)))
    return jnp.swapaxes(out, 0, 1).reshape(b, s, h * V_HEAD_DIM)


def _fwd_setup_inputs(seed: int = 0) -> dict:
    key = jax.random.key(seed)
    ks = jax.random.split(key, 20)

    def w(k, shape, fan_in):
        return jax.random.normal(k, shape, jnp.float32) * (fan_in ** -0.5)

    def gain(k, shape):
        return 1.0 + 0.02 * jax.random.normal(k, shape, jnp.float32)

    x = jax.random.normal(ks[0], (BATCH, SEQ, D_MODEL), jnp.float32)
    positions = (jnp.arange(SEQ, dtype=jnp.int32)[None, :]
                 + jax.random.randint(ks[1], (BATCH, 1), 0, 1024, dtype=jnp.int32))
    return {
        "x": x,
        "positions": positions,
        "g_mix": gain(ks[2], (DEPTH, D_MODEL)),
        "w_in": w(ks[3], (DEPTH, D_MODEL, IN_COLS), D_MODEL),
        "b_gate": 0.01 * jax.random.normal(ks[4], (DEPTH, N_BRANCHES * D_MODEL), jnp.float32),
        "conv_w": w(ks[5], (DEPTH, CONV_WIDTH, CONV_DIM), CONV_WIDTH),
        "g_q_a": gain(ks[6], (DEPTH, Q_LORA_RANK)),
        "w_q_b": w(ks[7], (DEPTH, Q_LORA_RANK, N_HEADS * QK_HEAD_DIM), Q_LORA_RANK),
        "g_kv_a": gain(ks[8], (DEPTH, KV_LORA_RANK)),
        "w_kv_b": w(ks[9], (DEPTH, KV_LORA_RANK, N_HEADS * (QK_NOPE_DIM + V_HEAD_DIM)), KV_LORA_RANK),
        "w_branch": w(ks[10], (DEPTH, N_BRANCHES, CONV_DIM, D_MODEL), CONV_DIM),
        "w_out": w(ks[11], (DEPTH, D_MODEL, D_MODEL), D_MODEL),
        "g_ffn": gain(ks[12], (DEPTH, D_MODEL)),
        "w_ffn_gate": w(ks[13], (DEPTH, D_MODEL, D_FF), D_MODEL),
        "w_ffn_up": w(ks[14], (DEPTH, D_MODEL, D_FF), D_MODEL),
        "w_ffn_down": w(ks[15], (DEPTH, D_FF, D_MODEL), D_FF),
        "g_final": gain(ks[16], (D_MODEL,)),
    }


def _fwd_reference(x, positions, g_mix, w_in, b_gate, conv_w, g_q_a, w_q_b, g_kv_a, w_kv_b,
              w_branch, w_out, g_ffn, w_ffn_gate, w_ffn_up, w_ffn_down, g_final):
    b, s, d = x.shape
    dt = x.dtype
    inv_freq = ROPE_THETA ** (-jnp.arange(0, QK_ROPE_DIM, 2, dtype=jnp.float32) / QK_ROPE_DIM)
    ang = positions.astype(jnp.float32)[..., None] * inv_freq[None, None, :]
    cos, sin = jnp.cos(ang).astype(dt), jnp.sin(ang).astype(dt)

    for l in range(DEPTH):
        h = rms_norm(x, g_mix[l])
        z = h @ w_in[l]
        c_b, c_c, c_h, q_a, kv_a, k_rope, z_gate = split_columns(z, IN_SPLITS)

        y_a = c_b * centred_short_conv(c_c * c_h, conv_w[l])

        q = (rms_norm(q_a, g_q_a[l]) @ w_q_b[l]).reshape(b, s, N_HEADS, QK_HEAD_DIM)
        q_nope, q_rope = q[..., :QK_NOPE_DIM], q[..., QK_NOPE_DIM:]
        q_rope = rotary(q_rope, cos[:, :, None, :], sin[:, :, None, :])
        kv = (rms_norm(kv_a, g_kv_a[l]) @ w_kv_b[l]).reshape(b, s, N_HEADS, QK_NOPE_DIM + V_HEAD_DIM)
        k_nope, v = kv[..., :QK_NOPE_DIM], kv[..., QK_NOPE_DIM:]
        k_rope = rotary(k_rope, cos, sin)
        y_b = mla_attention(q_nope, q_rope, k_nope, k_rope, v)

        y_br = jnp.einsum('nbsc,ncd->bsnd', jnp.stack([y_a, y_b], axis=0), w_branch[l])
        gates = jax.nn.sigmoid((z_gate + b_gate[l]).astype(jnp.float32)).astype(dt)
        merged = jnp.sum(gates.reshape(b, s, N_BRANCHES, d) * y_br, axis=2)
        x = x + merged @ w_out[l]

        h2 = rms_norm(x, g_ffn[l])
        x = x + (jax.nn.silu(h2 @ w_ffn_gate[l]) * (h2 @ w_ffn_up[l])) @ w_ffn_down[l]

    return rms_norm(x, g_final)


import jax as _jax
import jax.numpy as _jnp

TWIN_FORMAT = 'train_step'
FWD_PARAMS = ['x', 'positions', 'g_mix', 'w_in', 'b_gate', 'conv_w', 'g_q_a', 'w_q_b', 'g_kv_a', 'w_kv_b', 'w_branch', 'w_out', 'g_ffn', 'w_ffn_gate', 'w_ffn_up', 'w_ffn_down', 'g_final']
TWIN_WEIGHTS = ['g_mix', 'w_in', 'b_gate', 'conv_w', 'g_q_a', 'w_q_b', 'g_kv_a', 'w_kv_b', 'w_branch', 'w_out', 'g_ffn', 'w_ffn_gate', 'w_ffn_up', 'w_ffn_down', 'g_final']
TWIN_DIFF_INPUT = 'x'
TWIN_INPUTS = ['x', 'positions', 'g_mix', 'w_in', 'b_gate', 'conv_w', 'g_q_a', 'w_q_b', 'g_kv_a', 'w_kv_b', 'w_branch', 'w_out', 'g_ffn', 'w_ffn_gate', 'w_ffn_up', 'w_ffn_down', 'g_final', 'loss_target', 'm_g_mix', 'm_w_in', 'm_b_gate', 'm_conv_w', 'm_g_q_a', 'm_w_q_b', 'm_g_kv_a', 'm_w_kv_b', 'm_w_branch', 'm_w_out', 'm_g_ffn', 'm_w_ffn_gate', 'm_w_ffn_up', 'm_w_ffn_down', 'm_g_final', 'v_g_mix', 'v_w_in', 'v_b_gate', 'v_conv_w', 'v_g_q_a', 'v_w_q_b', 'v_g_kv_a', 'v_w_kv_b', 'v_w_branch', 'v_w_out', 'v_g_ffn', 'v_w_ffn_gate', 'v_w_ffn_up', 'v_w_ffn_down', 'v_g_final']
TWIN_OUTPUTS = ['loss', 'grad_x', 'grad_g_mix', 'grad_w_in', 'grad_b_gate', 'grad_conv_w', 'grad_g_q_a', 'grad_w_q_b', 'grad_g_kv_a', 'grad_w_kv_b', 'grad_w_branch', 'grad_w_out', 'grad_g_ffn', 'grad_w_ffn_gate', 'grad_w_ffn_up', 'grad_w_ffn_down', 'grad_g_final', 'delta_g_mix', 'delta_w_in', 'delta_b_gate', 'delta_conv_w', 'delta_g_q_a', 'delta_w_q_b', 'delta_g_kv_a', 'delta_w_kv_b', 'delta_w_branch', 'delta_w_out', 'delta_g_ffn', 'delta_w_ffn_gate', 'delta_w_ffn_up', 'delta_w_ffn_down', 'delta_g_final', 'new_m_g_mix', 'new_m_w_in', 'new_m_b_gate', 'new_m_conv_w', 'new_m_g_q_a', 'new_m_w_q_b', 'new_m_g_kv_a', 'new_m_w_kv_b', 'new_m_w_branch', 'new_m_w_out', 'new_m_g_ffn', 'new_m_w_ffn_gate', 'new_m_w_ffn_up', 'new_m_w_ffn_down', 'new_m_g_final', 'new_v_g_mix', 'new_v_w_in', 'new_v_b_gate', 'new_v_conv_w', 'new_v_g_q_a', 'new_v_w_q_b', 'new_v_g_kv_a', 'new_v_w_kv_b', 'new_v_w_branch', 'new_v_w_out', 'new_v_g_ffn', 'new_v_w_ffn_gate', 'new_v_w_ffn_up', 'new_v_w_ffn_down', 'new_v_g_final']
TWIN_LEAF_KINDS = {'loss': 'loss', 'grad_x': 'grad_x', 'grad_g_mix': 'grad_w', 'grad_w_in': 'grad_w', 'grad_b_gate': 'grad_w', 'grad_conv_w': 'grad_w', 'grad_g_q_a': 'grad_w', 'grad_w_q_b': 'grad_w', 'grad_g_kv_a': 'grad_w', 'grad_w_kv_b': 'grad_w', 'grad_w_branch': 'grad_w', 'grad_w_out': 'grad_w', 'grad_g_ffn': 'grad_w', 'grad_w_ffn_gate': 'grad_w', 'grad_w_ffn_up': 'grad_w', 'grad_w_ffn_down': 'grad_w', 'grad_g_final': 'grad_w', 'delta_g_mix': 'delta_w', 'delta_w_in': 'delta_w', 'delta_b_gate': 'delta_w', 'delta_conv_w': 'delta_w', 'delta_g_q_a': 'delta_w', 'delta_w_q_b': 'delta_w', 'delta_g_kv_a': 'delta_w', 'delta_w_kv_b': 'delta_w', 'delta_w_branch': 'delta_w', 'delta_w_out': 'delta_w', 'delta_g_ffn': 'delta_w', 'delta_w_ffn_gate': 'delta_w', 'delta_w_ffn_up': 'delta_w', 'delta_w_ffn_down': 'delta_w', 'delta_g_final': 'delta_w', 'new_m_g_mix': 'new_m', 'new_m_w_in': 'new_m', 'new_m_b_gate': 'new_m', 'new_m_conv_w': 'new_m', 'new_m_g_q_a': 'new_m', 'new_m_w_q_b': 'new_m', 'new_m_g_kv_a': 'new_m', 'new_m_w_kv_b': 'new_m', 'new_m_w_branch': 'new_m', 'new_m_w_out': 'new_m', 'new_m_g_ffn': 'new_m', 'new_m_w_ffn_gate': 'new_m', 'new_m_w_ffn_up': 'new_m', 'new_m_w_ffn_down': 'new_m', 'new_m_g_final': 'new_m', 'new_v_g_mix': 'new_v', 'new_v_w_in': 'new_v', 'new_v_b_gate': 'new_v', 'new_v_conv_w': 'new_v', 'new_v_g_q_a': 'new_v', 'new_v_w_q_b': 'new_v', 'new_v_g_kv_a': 'new_v', 'new_v_w_kv_b': 'new_v', 'new_v_w_branch': 'new_v', 'new_v_w_out': 'new_v', 'new_v_g_ffn': 'new_v', 'new_v_w_ffn_gate': 'new_v', 'new_v_w_ffn_up': 'new_v', 'new_v_w_ffn_down': 'new_v', 'new_v_g_final': 'new_v'}


def _forward(args):
    return _fwd_reference(*[args[k] for k in FWD_PARAMS])


def _output_shape():
    out = _jax.eval_shape(lambda: _forward(_fwd_setup_inputs(0)))
    return out.shape, out.dtype

N_MICROBATCH = 1
ADAM_LR = 0.001
ADAM_B1 = 0.9
ADAM_B2 = 0.999
ADAM_EPS = 1e-08
ADAM_WD = 0.01
ADAM_STEP = 10
PER_EXAMPLE_BATCH_AXIS = {'x': 0, 'positions': 0, 'loss_target': 0}
SHARED_INPUTS = []
_WEIGHT_DTYPES = {'g_mix': _jnp.float32, 'w_in': _jnp.float32, 'b_gate': _jnp.float32, 'conv_w': _jnp.float32, 'g_q_a': _jnp.float32, 'w_q_b': _jnp.float32, 'g_kv_a': _jnp.float32, 'w_kv_b': _jnp.float32, 'w_branch': _jnp.float32, 'w_out': _jnp.float32, 'g_ffn': _jnp.float32, 'w_ffn_gate': _jnp.float32, 'w_ffn_up': _jnp.float32, 'w_ffn_down': _jnp.float32, 'g_final': _jnp.float32}
MOMENT_SCALE = {'g_mix': 4.175930e-02, 'w_in': 2.111915e-02, 'b_gate': 6.497124e-03, 'conv_w': 3.343885e-02, 'g_q_a': 5.059236e-03, 'w_q_b': 2.877295e-03, 'g_kv_a': 9.786987e-03, 'w_kv_b': 3.266459e-03, 'w_branch': 1.657423e-02, 'w_out': 2.345798e-02, 'g_ffn': 3.024106e-02, 'w_ffn_gate': 1.298760e-02, 'w_ffn_up': 1.257890e-02, 'w_ffn_down': 2.059851e-02, 'g_final': 7.981174e+00}


def _to_microbatches(a, axis):
    t = _jnp.moveaxis(a, axis, 0)
    t = t.reshape((N_MICROBATCH, t.shape[0] // N_MICROBATCH) + t.shape[1:])
    return _jnp.moveaxis(t, 1, axis + 1)


def setup_inputs(seed: int = 0) -> dict:
    inp = _fwd_setup_inputs(seed)
    key = _jax.random.fold_in(_jax.random.key(seed), 7919)
    shape, _ = _output_shape()
    out = dict(inp)
    out["loss_target"] = _jax.random.normal(_jax.random.fold_in(key, 0), shape, _jnp.float32)
    for i, name in enumerate(TWIN_WEIGHTS):
        w = inp[name].astype(_jnp.float32)
        if MOMENT_SCALE is None:
            s = _jnp.sqrt(_jnp.mean(_jnp.square(w)) + 1e-30)
        else:
            s = MOMENT_SCALE[name]
        km, kv = _jax.random.split(_jax.random.fold_in(key, i + 1))
        out[name] = w
        out["m_" + name] = s * _jax.random.normal(km, w.shape, _jnp.float32)
        out["v_" + name] = (s * s) * _jax.random.uniform(kv, w.shape, _jnp.float32, 0.5, 1.5)
    if N_MICROBATCH > 1:
        for name, axis in PER_EXAMPLE_BATCH_AXIS.items():
            out[name] = _to_microbatches(out[name], axis)
    return {'x': out['x'], 'positions': out['positions'], 'g_mix': out['g_mix'], 'w_in': out['w_in'], 'b_gate': out['b_gate'], 'conv_w': out['conv_w'], 'g_q_a': out['g_q_a'], 'w_q_b': out['w_q_b'], 'g_kv_a': out['g_kv_a'], 'w_kv_b': out['w_kv_b'], 'w_branch': out['w_branch'], 'w_out': out['w_out'], 'g_ffn': out['g_ffn'], 'w_ffn_gate': out['w_ffn_gate'], 'w_ffn_up': out['w_ffn_up'], 'w_ffn_down': out['w_ffn_down'], 'g_final': out['g_final'], 'loss_target': out['loss_target'], 'm_g_mix': out['m_g_mix'], 'm_w_in': out['m_w_in'], 'm_b_gate': out['m_b_gate'], 'm_conv_w': out['m_conv_w'], 'm_g_q_a': out['m_g_q_a'], 'm_w_q_b': out['m_w_q_b'], 'm_g_kv_a': out['m_g_kv_a'], 'm_w_kv_b': out['m_w_kv_b'], 'm_w_branch': out['m_w_branch'], 'm_w_out': out['m_w_out'], 'm_g_ffn': out['m_g_ffn'], 'm_w_ffn_gate': out['m_w_ffn_gate'], 'm_w_ffn_up': out['m_w_ffn_up'], 'm_w_ffn_down': out['m_w_ffn_down'], 'm_g_final': out['m_g_final'], 'v_g_mix': out['v_g_mix'], 'v_w_in': out['v_w_in'], 'v_b_gate': out['v_b_gate'], 'v_conv_w': out['v_conv_w'], 'v_g_q_a': out['v_g_q_a'], 'v_w_q_b': out['v_w_q_b'], 'v_g_kv_a': out['v_g_kv_a'], 'v_w_kv_b': out['v_w_kv_b'], 'v_w_branch': out['v_w_branch'], 'v_w_out': out['v_w_out'], 'v_g_ffn': out['v_g_ffn'], 'v_w_ffn_gate': out['v_w_ffn_gate'], 'v_w_ffn_up': out['v_w_ffn_up'], 'v_w_ffn_down': out['v_w_ffn_down'], 'v_g_final': out['v_g_final']}


def _loss(weights, diff, rest, loss_target):
    with _jax.named_scope("forward"):
        args = {**rest, TWIN_DIFF_INPUT: diff, **{k: w.astype(_WEIGHT_DTYPES[k]) for k, w in weights.items()}}
        y = _forward(args)
    with _jax.named_scope("loss_head"):
        err = _jnp.square(y.astype(_jnp.float32) - loss_target)
        return 0.5 * _jnp.sum(_jnp.mean(err, axis=-1)) if err.ndim else 0.5 * err


def _adamw(w, g, m, v):
    m = ADAM_B1 * m + (1.0 - ADAM_B1) * g
    v = ADAM_B2 * v + (1.0 - ADAM_B2) * _jnp.square(g)
    m_hat = m / (1.0 - ADAM_B1 ** ADAM_STEP)
    v_hat = v / (1.0 - ADAM_B2 ** ADAM_STEP)
    delta = -ADAM_LR * (m_hat / (_jnp.sqrt(v_hat) + ADAM_EPS) + ADAM_WD * w)
    return delta, m, v


def reference(x, positions, g_mix, w_in, b_gate, conv_w, g_q_a, w_q_b, g_kv_a, w_kv_b, w_branch, w_out, g_ffn, w_ffn_gate, w_ffn_up, w_ffn_down, g_final, loss_target, m_g_mix, m_w_in, m_b_gate, m_conv_w, m_g_q_a, m_w_q_b, m_g_kv_a, m_w_kv_b, m_w_branch, m_w_out, m_g_ffn, m_w_ffn_gate, m_w_ffn_up, m_w_ffn_down, m_g_final, v_g_mix, v_w_in, v_b_gate, v_conv_w, v_g_q_a, v_w_q_b, v_g_kv_a, v_w_kv_b, v_w_branch, v_w_out, v_g_ffn, v_w_ffn_gate, v_w_ffn_up, v_w_ffn_down, v_g_final):
    given = dict(x=x, positions=positions, g_mix=g_mix, w_in=w_in, b_gate=b_gate, conv_w=conv_w, g_q_a=g_q_a, w_q_b=w_q_b, g_kv_a=g_kv_a, w_kv_b=w_kv_b, w_branch=w_branch, w_out=w_out, g_ffn=g_ffn, w_ffn_gate=w_ffn_gate, w_ffn_up=w_ffn_up, w_ffn_down=w_ffn_down, g_final=g_final, loss_target=loss_target, m_g_mix=m_g_mix, m_w_in=m_w_in, m_b_gate=m_b_gate, m_conv_w=m_conv_w, m_g_q_a=m_g_q_a, m_w_q_b=m_w_q_b, m_g_kv_a=m_g_kv_a, m_w_kv_b=m_w_kv_b, m_w_branch=m_w_branch, m_w_out=m_w_out, m_g_ffn=m_g_ffn, m_w_ffn_gate=m_w_ffn_gate, m_w_ffn_up=m_w_ffn_up, m_w_ffn_down=m_w_ffn_down, m_g_final=m_g_final, v_g_mix=v_g_mix, v_w_in=v_w_in, v_b_gate=v_b_gate, v_conv_w=v_conv_w, v_g_q_a=v_g_q_a, v_w_q_b=v_w_q_b, v_g_kv_a=v_g_kv_a, v_w_kv_b=v_w_kv_b, v_w_branch=v_w_branch, v_w_out=v_w_out, v_g_ffn=v_g_ffn, v_w_ffn_gate=v_w_ffn_gate, v_w_ffn_up=v_w_ffn_up, v_w_ffn_down=v_w_ffn_down, v_g_final=v_g_final)
    weights = {n: given[n] for n in TWIN_WEIGHTS}
    shared = {n: given[n] for n in SHARED_INPUTS}
    per_example = {n: given[n] for n in ['x', 'positions']}
    grad_fn = _jax.value_and_grad(_loss, argnums=(0, 1))

    def one_microbatch(ex, loss_target):
        ex = dict(ex)
        diff = ex.pop(TWIN_DIFF_INPUT)
        return grad_fn(weights, diff, {**shared, **ex}, loss_target)

    if N_MICROBATCH == 1:
        loss, (grad_w, grad_x) = one_microbatch(per_example, given["loss_target"])
    else:
        def body(carry, xs):
            loss_sum, grad_sum = carry
            l_k, (gw_k, gx_k) = one_microbatch(xs[0], xs[1])
            with _jax.named_scope("update"):
                return (loss_sum + l_k, _jax.tree.map(_jnp.add, grad_sum, gw_k)), gx_k

        init = (_jnp.zeros((), _jnp.float32), _jax.tree.map(_jnp.zeros_like, weights))
        (loss, grad_w), grad_x = _jax.lax.scan(body, init, (per_example, given["loss_target"]))
    with _jax.named_scope("update"):
        delta_w, new_m, new_v = {}, {}, {}
        for n in TWIN_WEIGHTS:
            delta_w[n], new_m[n], new_v[n] = _adamw(weights[n], grad_w[n], given["m_" + n], given["v_" + n])
    return (loss, grad_x, *[grad_w[n] for n in TWIN_WEIGHTS], *[delta_w[n] for n in TWIN_WEIGHTS],
            *[new_m[n] for n in TWIN_WEIGHTS], *[new_v[n] for n in TWIN_WEIGHTS])
```

```python
import functools
import math

import jax
import jax.numpy as jnp
from jax import lax
from jax.experimental import pallas as pl
from jax.experimental.pallas import tpu as pltpu

F32 = jnp.float32
BF16 = jnp.bfloat16
N_DEV = 8
MESH_AXES = ("x", "y", "c")
MESH = pl.DeviceIdType.MESH

QK_NOPE = 128
QK_ROPE = 64
V_HEAD = 128
HEAD_PAD = 256
Q_LORA = 1024
KV_LORA = 512
ROPE_THETA = 10000.0
RMS_EPS = 1e-6
SOFTMAX_SCALE = 1.0 / math.sqrt(QK_NOPE + QK_ROPE)
ADAM_LR, ADAM_B1, ADAM_B2, ADAM_EPS, ADAM_WD, ADAM_STEP = 0.001, 0.9, 0.999, 1e-08, 0.01, 10

VMEM_LIMIT = 60 * 1024 * 1024
LANE = 128
ADAM_BLOCK_ELEMS = 1 << 18


def _pick(n, cands=(1024, 512, 256, 128)):
    for c in cands:
        if n % c == 0:
            return c
    return n


def _params(*sem):
    return pltpu.CompilerParams(dimension_semantics=sem, vmem_limit_bytes=VMEM_LIMIT)


def _rb(tm, c, cb=0):
    return pl.BlockSpec((tm, c), lambda i: (i, cb))


def _vec(c, cb=0):
    return pl.BlockSpec((1, c), lambda i: (0, cb))


def _all_gather(x, name):
    def body(x_ref, out_ref, send_sems, recv_sems, local_sem):
        x_, y_, c_ = lax.axis_index("x"), lax.axis_index("y"), lax.axis_index("c")
        me, sibling = (x_, y_, c_), (x_, y_, 1 - c_)
        chips = [(1 - x_, y_), (x_, 1 - y_), (1 - x_, 1 - y_)]

        def slot(px, py, pc):
            return out_ref.at[4 * px + 2 * py + pc]

        def copy(k, block, to, src=None):
            return pltpu.make_async_remote_copy(
                src_ref=slot(*block) if src is None else src, dst_ref=slot(*block),
                send_sem=send_sems.at[k], recv_sem=recv_sems.at[k], device_id=to, device_id_type=MESH)

        mine = pltpu.make_async_copy(x_ref, slot(*me), local_sem)
        mine.start()
        first = [copy(0, me, sibling, src=x_ref)]
        first += [copy(1 + j, me, (*chip, c_), src=x_ref) for j, chip in enumerate(chips)]
        for cp in first:
            cp.start()
        passed = [copy(4 + j, (*chip, c_), sibling) for j, chip in enumerate(chips)]
        for j, chip in enumerate(chips):
            copy(1 + j, (*chip, c_), me).wait_recv()
            passed[j].start()
        copy(0, sibling, me).wait_recv()
        for j, chip in enumerate(chips):
            copy(4 + j, (*chip, 1 - c_), me).wait_recv()
        for cp in first + passed:
            cp.wait_send()
        mine.wait()

    return pl.pallas_call(
        body, name=name,
        out_shape=jax.ShapeDtypeStruct((N_DEV,) + x.shape, x.dtype),
        in_specs=[pl.BlockSpec(memory_space=pl.ANY)],
        out_specs=pl.BlockSpec(memory_space=pl.ANY),
        scratch_shapes=[pltpu.SemaphoreType.DMA((7,)), pltpu.SemaphoreType.DMA((7,)), pltpu.SemaphoreType.DMA(())],
    )(x)


def _exchange(g, name):
    def body(g_ref, out_ref, send_sems, recv_sems, local_sem):
        x_, y_, c_ = lax.axis_index("x"), lax.axis_index("y"), lax.axis_index("c")
        me_idx = 4 * x_ + 2 * y_ + c_
        mine = pltpu.make_async_copy(g_ref.at[me_idx], out_ref.at[me_idx], local_sem)
        mine.start()
        sends, recvs = [], []
        for d in range(1, N_DEV):
            px = 1 - x_ if d & 4 else x_
            py = 1 - y_ if d & 2 else y_
            pc = 1 - c_ if d & 1 else c_
            peer_idx = 4 * px + 2 * py + pc
            sends.append(pltpu.make_async_remote_copy(
                src_ref=g_ref.at[peer_idx], dst_ref=out_ref.at[me_idx],
                send_sem=send_sems.at[d - 1], recv_sem=recv_sems.at[d - 1], device_id=(px, py, pc), device_id_type=MESH))
            recvs.append(pltpu.make_async_remote_copy(
                src_ref=g_ref.at[peer_idx], dst_ref=out_ref.at[peer_idx],
                send_sem=send_sems.at[d - 1], recv_sem=recv_sems.at[d - 1], device_id=(px, py, pc), device_id_type=MESH))
        for cp in sends:
            cp.start()
        for cp in recvs:
            cp.wait_recv()
        for cp in sends:
            cp.wait_send()
        mine.wait()

    return pl.pallas_call(
        body, name=name,
        out_shape=jax.ShapeDtypeStruct(g.shape, g.dtype),
        in_specs=[pl.BlockSpec(memory_space=pl.ANY)],
        out_specs=pl.BlockSpec(memory_space=pl.ANY),
        scratch_shapes=[pltpu.SemaphoreType.DMA((7,)), pltpu.SemaphoreType.DMA((7,)), pltpu.SemaphoreType.DMA(())],
    )(g)


def _matmul(a, b, *, name, ta=False, tb=False, res=None, out_dtype=BF16):
    (kdim, m) = a.shape if ta else a.shape[::-1]
    (n, kdim_b) = b.shape if tb else b.shape[::-1]
    assert kdim == kdim_b, (a.shape, b.shape, ta, tb)
    tm, tn, tk = _pick(m), _pick(n), _pick(kdim)
    nk = kdim // tk
    a_spec = pl.BlockSpec((tk, tm), lambda i, j, k: (k, i)) if ta else pl.BlockSpec((tm, tk), lambda i, j, k: (i, k))
    b_spec = pl.BlockSpec((tn, tk), lambda i, j, k: (j, k)) if tb else pl.BlockSpec((tk, tn), lambda i, j, k: (k, j))
    o_spec = pl.BlockSpec((tm, tn), lambda i, j, k: (i, j))
    dims = (((0 if ta else 1,), (1 if tb else 0,)), ((), ()))

    def body(*refs):
        if res is None:
            a_ref, b_ref, o_ref, acc = refs
            r_ref = None
        else:
            a_ref, b_ref, r_ref, o_ref, acc = refs
        k = pl.program_id(2)

        @pl.when(k == 0)
        def _():
            acc[...] = jnp.zeros_like(acc)

        acc[...] += lax.dot_general(a_ref[...], b_ref[...], dims, preferred_element_type=F32)

        @pl.when(k == nk - 1)
        def _():
            v = acc[...]
            if r_ref is not None:
                v = r_ref[...] + v
            o_ref[...] = v.astype(out_dtype)

    operands = (a, b) if res is None else (a, b, res)
    in_specs = [a_spec, b_spec] if res is None else [a_spec, b_spec, o_spec]
    return pl.pallas_call(
        body, name=name, grid=(m // tm, n // tn, nk),
        out_shape=jax.ShapeDtypeStruct((m, n), out_dtype),
        in_specs=in_specs, out_specs=o_spec,
        scratch_shapes=[pltpu.VMEM((tm, tn), F32)],
        compiler_params=_params("parallel", "parallel", "arbitrary"),
    )(*operands)


def _rms_inv(x):
    return lax.rsqrt(jnp.mean(x * x, axis=-1, keepdims=True) + RMS_EPS)


def _rms_fwd(x, g, name):
    t, d = x.shape
    tm = _pick(t, (256, 128))

    def body(x_ref, g_ref, h_ref):
        xv = x_ref[...]
        h_ref[...] = (xv * _rms_inv(xv) * g_ref[...]).astype(BF16)

    return pl.pallas_call(
        body, name=name, grid=(t // tm,), out_shape=jax.ShapeDtypeStruct((t, d), BF16),
        in_specs=[_rb(tm, d), _vec(d)], out_specs=_rb(tm, d), compiler_params=_params("parallel"))(x, g)


def _rms_bwd_rows(dy, xv, g):
    inv = _rms_inv(xv)
    xhat = xv * inv
    dxhat = dy * g
    dx = inv * (dxhat - xhat * jnp.mean(dxhat * xhat, axis=-1, keepdims=True))
    return dx, dy * xhat


def _rms_bwd(dy, x, g, res, name):
    t, d = x.shape
    tm = _pick(t, (128,))

    def body(dy_ref, x_ref, g_ref, r_ref, dx_ref, dxb_ref, dg_ref):
        dx, dgrow = _rms_bwd_rows(dy_ref[...].astype(F32), x_ref[...], g_ref[...])
        dx = r_ref[...] + dx
        dx_ref[...] = dx
        dxb_ref[...] = dx.astype(BF16)

        @pl.when(pl.program_id(0) == 0)
        def _():
            dg_ref[...] = jnp.zeros_like(dg_ref)

        dg_ref[...] += jnp.sum(dgrow, axis=0, keepdims=True)

    return pl.pallas_call(
        body, name=name, grid=(t // tm,),
        out_shape=(jax.ShapeDtypeStruct((t, d), F32), jax.ShapeDtypeStruct((t, d), BF16), jax.ShapeDtypeStruct((1, d), F32)),
        in_specs=[_rb(tm, d), _rb(tm, d), _vec(d), _rb(tm, d)],
        out_specs=(_rb(tm, d), _rb(tm, d), _vec(d)), compiler_params=_params("arbitrary"))(dy, x, g, res)


def _latent_norm(z_all, g_q, g_kv, q_off, kv_off, name):
    t = z_all.shape[0]
    tm = _pick(t, (256, 128))

    def body(qa_ref, kva_ref, gq_ref, gkv_ref, qn_ref, kvn_ref):
        qa = qa_ref[...].astype(F32)
        qn_ref[...] = (qa * _rms_inv(qa) * gq_ref[...]).astype(BF16)
        kva = kva_ref[...].astype(F32)
        kvn_ref[...] = (kva * _rms_inv(kva) * gkv_ref[...]).astype(BF16)

    return pl.pallas_call(
        body, name=name, grid=(t // tm,),
        out_shape=(jax.ShapeDtypeStruct((t, Q_LORA), BF16), jax.ShapeDtypeStruct((t, KV_LORA), BF16)),
        in_specs=[_rb(tm, Q_LORA, q_off // Q_LORA), _rb(tm, KV_LORA, kv_off // KV_LORA), _vec(Q_LORA), _vec(KV_LORA)],
        out_specs=(_rb(tm, Q_LORA), _rb(tm, KV_LORA)), compiler_params=_params("parallel"))(z_all, z_all, g_q, g_kv)


def _latent_norm_bwd(dqn, dkvn, z_all, g_q, g_kv, q_off, kv_off, name):
    t = z_all.shape[0]
    tm = _pick(t, (256, 128))

    def body(dqn_ref, dkvn_ref, qa_ref, kva_ref, gq_ref, gkv_ref, dqa_ref, dkva_ref, dgq_ref, dgkv_ref):
        dqa, dgq = _rms_bwd_rows(dqn_ref[...].astype(F32), qa_ref[...].astype(F32), gq_ref[...])
        dkva, dgkv = _rms_bwd_rows(dkvn_ref[...].astype(F32), kva_ref[...].astype(F32), gkv_ref[...])
        dqa_ref[...] = dqa.astype(BF16)
        dkva_ref[...] = dkva.astype(BF16)

        @pl.when(pl.program_id(0) == 0)
        def _():
            dgq_ref[...] = jnp.zeros_like(dgq_ref)
            dgkv_ref[...] = jnp.zeros_like(dgkv_ref)

        dgq_ref[...] += jnp.sum(dgq, axis=0, keepdims=True)
        dgkv_ref[...] += jnp.sum(dgkv, axis=0, keepdims=True)

    return pl.pallas_call(
        body, name=name, grid=(t // tm,),
        out_shape=(jax.ShapeDtypeStruct((t, Q_LORA), BF16), jax.ShapeDtypeStruct((t, KV_LORA), BF16),
                   jax.ShapeDtypeStruct((1, Q_LORA), F32), jax.ShapeDtypeStruct((1, KV_LORA), F32)),
        in_specs=[_rb(tm, Q_LORA), _rb(tm, KV_LORA), _rb(tm, Q_LORA, q_off // Q_LORA), _rb(tm, KV_LORA, kv_off // KV_LORA),
                  _vec(Q_LORA), _vec(KV_LORA)],
        out_specs=(_rb(tm, Q_LORA), _rb(tm, KV_LORA), _vec(Q_LORA), _vec(KV_LORA)),
        compiler_params=_params("arbitrary"))(dqn, dkvn, z_all, z_all, g_q, g_kv)


def _rot(xv, cos_k, sin_a, sin_b, sign):
    return xv * cos_k + sign * (pltpu.roll(xv, LANE - 32, 1) * sin_a + pltpu.roll(xv, 32, 1) * sin_b)


def _rope_q(q_raw, tabs, n_heads, sign, out_dtype, name):
    t, w = q_raw.shape
    tm = _pick(t, (256, 128))

    def body(q_ref, cos_ref, sa_ref, sb_ref, o_ref):
        cos_k, sin_a, sin_b = cos_ref[...], sa_ref[...], sb_ref[...]
        for h in range(n_heads):
            lo = h * HEAD_PAD
            o_ref[:, lo:lo + LANE] = q_ref[:, lo:lo + LANE].astype(out_dtype)
            o_ref[:, lo + LANE:lo + HEAD_PAD] = _rot(
                q_ref[:, lo + LANE:lo + HEAD_PAD].astype(F32), cos_k, sin_a, sin_b, sign).astype(out_dtype)

    return pl.pallas_call(
        body, name=name, grid=(t // tm,), out_shape=jax.ShapeDtypeStruct((t, w), out_dtype),
        in_specs=[_rb(tm, w), _rb(tm, LANE), _rb(tm, LANE), _rb(tm, LANE)],
        out_specs=_rb(tm, w), compiler_params=_params("parallel"))(q_raw, *tabs)


def _rope_k(kv, z_all, tabs, n_heads, kr_off, name):
    t = kv.shape[0]
    tm = _pick(t, (256, 128))
    wk = n_heads * QK_NOPE

    def body(kn_ref, kr_ref, cos_ref, sa_ref, sb_ref, o_ref):
        krot = _rot(kr_ref[...].astype(F32), cos_ref[...], sa_ref[...], sb_ref[...], 1.0).astype(BF16)
        for h in range(n_heads):
            o_ref[:, h * HEAD_PAD:h * HEAD_PAD + LANE] = kn_ref[:, h * QK_NOPE:(h + 1) * QK_NOPE]
            o_ref[:, h * HEAD_PAD + LANE:(h + 1) * HEAD_PAD] = krot

    return pl.pallas_call(
        body, name=name, grid=(t // tm,), out_shape=jax.ShapeDtypeStruct((t, n_heads * HEAD_PAD), BF16),
        in_specs=[_rb(tm, wk), _rb(tm, LANE, kr_off // LANE), _rb(tm, LANE), _rb(tm, LANE), _rb(tm, LANE)],
        out_specs=_rb(tm, n_heads * HEAD_PAD), compiler_params=_params("parallel"))(kv, z_all, *tabs)


def _rope_k_bwd(dk_pad, dv, tabs, n_heads, name):
    t = dk_pad.shape[0]
    tm = _pick(t, (256, 128))
    wk = n_heads * QK_NOPE

    def body(dk_ref, dv_ref, cos_ref, sa_ref, sb_ref, dkv_ref, dkr_ref):
        acc = dk_ref[:, LANE:HEAD_PAD]
        dkv_ref[:, 0:QK_NOPE] = dk_ref[:, 0:LANE].astype(BF16)
        for h in range(1, n_heads):
            acc = acc + dk_ref[:, h * HEAD_PAD + LANE:(h + 1) * HEAD_PAD]
            dkv_ref[:, h * QK_NOPE:(h + 1) * QK_NOPE] = dk_ref[:, h * HEAD_PAD:h * HEAD_PAD + LANE].astype(BF16)
        dkv_ref[:, wk:] = dv_ref[...].astype(BF16)
        dkr_ref[...] = _rot(acc, cos_ref[...], sa_ref[...], sb_ref[...], -1.0).astype(BF16)

    return pl.pallas_call(
        body, name=name, grid=(t // tm,),
        out_shape=(jax.ShapeDtypeStruct((t, 2 * wk), BF16), jax.ShapeDtypeStruct((t, LANE), BF16)),
        in_specs=[_rb(tm, n_heads * HEAD_PAD), _rb(tm, wk), _rb(tm, LANE), _rb(tm, LANE), _rb(tm, LANE)],
        out_specs=(_rb(tm, 2 * wk), _rb(tm, LANE)), compiler_params=_params("parallel"))(dk_pad, dv, *tabs)


NT_DIMS = (((1,), (1,)), ((), ()))
TN_DIMS = (((0,), (0,)), ((), ()))


def _softmax_rows(q, k):
    s = lax.dot_general(q, k, NT_DIMS, preferred_element_type=F32) * SOFTMAX_SCALE
    e = jnp.exp(s - jnp.max(s, axis=-1, keepdims=True))
    return e * (1.0 / jnp.sum(e, axis=-1, keepdims=True))


def _attn_fwd(q_pad, k_pad, kv, n_heads, name):
    t = q_pad.shape[0]
    tq = _pick(t, (256, 128))

    def body(q_ref, k_ref, v_ref, o_ref):
        p = _softmax_rows(q_ref[...], k_ref[...]).astype(BF16)
        o_ref[...] = jnp.dot(p, v_ref[...], preferred_element_type=F32).astype(BF16)

    return pl.pallas_call(
        body, name=name, grid=(n_heads, t // tq),
        out_shape=jax.ShapeDtypeStruct((t, n_heads * V_HEAD), BF16),
        in_specs=[pl.BlockSpec((tq, HEAD_PAD), lambda h, i: (i, h)),
                  pl.BlockSpec((t, HEAD_PAD), lambda h, i: (0, h)),
                  pl.BlockSpec((t, V_HEAD), lambda h, i: (0, n_heads + h))],
        out_specs=pl.BlockSpec((tq, V_HEAD), lambda h, i: (i, h)),
        compiler_params=_params("parallel", "parallel"))(q_pad, k_pad, kv)


def _attn_bwd(q_pad, k_pad, kv, do, n_heads, name):
    t = q_pad.shape[0]
    tq = _pick(t, (256, 128))
    nq = t // tq

    def body(q_ref, k_ref, v_ref, do_ref, dq_ref, dk_ref, dv_ref):
        @pl.when(pl.program_id(1) == 0)
        def _():
            dk_ref[...] = jnp.zeros_like(dk_ref)
            dv_ref[...] = jnp.zeros_like(dv_ref)

        q, k, dout = q_ref[...], k_ref[...], do_ref[...]
        p = _softmax_rows(q, k)
        dp = lax.dot_general(dout, v_ref[...], NT_DIMS, preferred_element_type=F32)
        ds = (p * (dp - jnp.sum(p * dp, axis=-1, keepdims=True)) * SOFTMAX_SCALE).astype(BF16)
        dq_ref[...] = jnp.dot(ds, k, preferred_element_type=F32)
        dk_ref[...] += lax.dot_general(ds, q, TN_DIMS, preferred_element_type=F32)
        dv_ref[...] += lax.dot_general(p.astype(BF16), dout, TN_DIMS, preferred_element_type=F32)

    return pl.pallas_call(
        body, name=name, grid=(n_heads, nq),
        out_shape=(jax.ShapeDtypeStruct((t, n_heads * HEAD_PAD), F32), jax.ShapeDtypeStruct((t, n_heads * HEAD_PAD), F32),
                   jax.ShapeDtypeStruct((t, n_heads * V_HEAD), F32)),
        in_specs=[pl.BlockSpec((tq, HEAD_PAD), lambda h, i: (i, h)),
                  pl.BlockSpec((t, HEAD_PAD), lambda h, i: (0, h)),
                  pl.BlockSpec((t, V_HEAD), lambda h, i: (0, n_heads + h)),
                  pl.BlockSpec((tq, V_HEAD), lambda h, i: (i, h))],
        out_specs=(pl.BlockSpec((tq, HEAD_PAD), lambda h, i: (i, h)),
                   pl.BlockSpec((t, HEAD_PAD), lambda h, i: (0, h)),
                   pl.BlockSpec((t, V_HEAD), lambda h, i: (0, h))),
        compiler_params=_params("parallel", "arbitrary"))(q_pad, k_pad, kv, do)


def _shift_rows(u, t):
    row = lax.broadcasted_iota(jnp.int32, u.shape, 0)
    prev = jnp.where(row == 0, 0.0, pltpu.roll(u, 1, 0))
    nxt = jnp.where(row == t - 1, 0.0, pltpu.roll(u, t - 1, 0))
    return prev, nxt


def _conv_fwd(z_all, conv_w, cc, name):
    t = z_all.shape[0]
    nb = cc // LANE

    def body(cb_ref, cc_ref, ch_ref, w_ref, y_ref):
        u = cc_ref[...].astype(F32) * ch_ref[...].astype(F32)
        prev, nxt = _shift_rows(u, t)
        w = w_ref[...]
        conv = prev * w[0:1, :] + u * w[1:2, :] + nxt * w[2:3, :]
        y_ref[...] = (cb_ref[...].astype(F32) * conv).astype(BF16)

    col = lambda g: pl.BlockSpec((t, LANE), lambda j: (0, g * nb + j))
    return pl.pallas_call(
        body, name=name, grid=(nb,), out_shape=jax.ShapeDtypeStruct((t, cc), BF16),
        in_specs=[col(0), col(1), col(2), pl.BlockSpec((3, LANE), lambda j: (0, j))],
        out_specs=pl.BlockSpec((t, LANE), lambda j: (0, j)),
        compiler_params=_params("parallel"))(z_all, z_all, z_all, conv_w)


def _conv_bwd(dy, z_all, conv_w, cc, name):
    t = z_all.shape[0]
    nb = cc // LANE

    def body(dy_ref, cb_ref, cc_ref, ch_ref, w_ref, dcb_ref, dcc_ref, dch_ref, dw_ref):
        c_c, c_h = cc_ref[...].astype(F32), ch_ref[...].astype(F32)
        u = c_c * c_h
        prev, nxt = _shift_rows(u, t)
        w = w_ref[...]
        dyv = dy_ref[...].astype(F32)
        dcb_ref[...] = (dyv * (prev * w[0:1, :] + u * w[1:2, :] + nxt * w[2:3, :])).astype(BF16)
        dconv = dyv * cb_ref[...].astype(F32)
        dw_ref[0:1, :] = jnp.sum(dconv * prev, axis=0, keepdims=True)
        dw_ref[1:2, :] = jnp.sum(dconv * u, axis=0, keepdims=True)
        dw_ref[2:3, :] = jnp.sum(dconv * nxt, axis=0, keepdims=True)
        dprev, dnxt = _shift_rows(dconv, t)
        du = dnxt * w[0:1, :] + dconv * w[1:2, :] + dprev * w[2:3, :]
        dcc_ref[...] = (du * c_h).astype(BF16)
        dch_ref[...] = (du * c_c).astype(BF16)

    col = lambda g: pl.BlockSpec((t, LANE), lambda j: (0, g * nb + j))
    one = pl.BlockSpec((t, LANE), lambda j: (0, j))
    wsp = pl.BlockSpec((3, LANE), lambda j: (0, j))
    act = jax.ShapeDtypeStruct((t, cc), BF16)
    return pl.pallas_call(
        body, name=name, grid=(nb,),
        out_shape=(act, act, act, jax.ShapeDtypeStruct((3, cc), F32)),
        in_specs=[one, col(0), col(1), col(2), wsp],
        out_specs=(one, one, one, wsp),
        compiler_params=_params("parallel"))(dy, z_all, z_all, z_all, conv_w)


def _sigmoid(v):
    return 1.0 / (1.0 + jnp.exp(-v))


def _merge_fwd(z_all, b_gate, y_a, y_b, gate_off, name):
    t, d = y_a.shape
    tm = _pick(t, (128,))
    gb = gate_off // d

    def body(za_ref, zb_ref, ba_ref, bb_ref, ya_ref, yb_ref, m_ref):
        ga = _sigmoid(za_ref[...].astype(F32) + ba_ref[...])
        gbv = _sigmoid(zb_ref[...].astype(F32) + bb_ref[...])
        m_ref[...] = (ga * ya_ref[...].astype(F32) + gbv * yb_ref[...].astype(F32)).astype(BF16)

    return pl.pallas_call(
        body, name=name, grid=(t // tm,), out_shape=jax.ShapeDtypeStruct((t, d), BF16),
        in_specs=[_rb(tm, d, gb), _rb(tm, d, gb + 1), _vec(d, 0), _vec(d, 1), _rb(tm, d), _rb(tm, d)],
        out_specs=_rb(tm, d), compiler_params=_params("parallel"))(z_all, z_all, b_gate, b_gate, y_a, y_b)


def _merge_bwd(dm, z_all, b_gate, y_a, y_b, gate_off, name):
    t, d = y_a.shape
    tm = _pick(t, (128,))
    gb = gate_off // d

    def body(dm_ref, za_ref, zb_ref, ba_ref, bb_ref, ya_ref, yb_ref, dya_ref, dyb_ref, dzg_ref, db_ref):
        dmv = dm_ref[...].astype(F32)
        ga = _sigmoid(za_ref[...].astype(F32) + ba_ref[...])
        gbv = _sigmoid(zb_ref[...].astype(F32) + bb_ref[...])
        dya_ref[...] = (dmv * ga).astype(BF16)
        dyb_ref[...] = (dmv * gbv).astype(BF16)
        dza = dmv * ya_ref[...].astype(F32) * (ga * (1.0 - ga))
        dzb = dmv * yb_ref[...].astype(F32) * (gbv * (1.0 - gbv))
        dzg_ref[:, 0:d] = dza.astype(BF16)
        dzg_ref[:, d:2 * d] = dzb.astype(BF16)

        @pl.when(pl.program_id(0) == 0)
        def _():
            db_ref[...] = jnp.zeros_like(db_ref)

        db_ref[:, 0:d] += jnp.sum(dza, axis=0, keepdims=True)
        db_ref[:, d:2 * d] += jnp.sum(dzb, axis=0, keepdims=True)

    act = jax.ShapeDtypeStruct((t, d), BF16)
    return pl.pallas_call(
        body, name=name, grid=(t // tm,),
        out_shape=(act, act, jax.ShapeDtypeStruct((t, 2 * d), BF16), jax.ShapeDtypeStruct((1, 2 * d), F32)),
        in_specs=[_rb(tm, d), _rb(tm, d, gb), _rb(tm, d, gb + 1), _vec(d, 0), _vec(d, 1), _rb(tm, d), _rb(tm, d)],
        out_specs=(_rb(tm, d), _rb(tm, d), _rb(tm, 2 * d), _vec(2 * d)),
        compiler_params=_params("arbitrary"))(dm, z_all, z_all, b_gate, b_gate, y_a, y_b)


def _swiglu_fwd(gate, up, name):
    t, f = gate.shape
    tm = _pick(t, (128,))

    def body(g_ref, u_ref, a_ref):
        g = g_ref[...].astype(F32)
        a_ref[...] = (g * _sigmoid(g) * u_ref[...].astype(F32)).astype(BF16)

    return pl.pallas_call(
        body, name=name, grid=(t // tm,), out_shape=jax.ShapeDtypeStruct((t, f), BF16),
        in_specs=[_rb(tm, f), _rb(tm, f)], out_specs=_rb(tm, f), compiler_params=_params("parallel"))(gate, up)


def _swiglu_bwd(dact, gate, up, name):
    t, f = gate.shape
    tm = _pick(t, (128,))

    def body(da_ref, g_ref, u_ref, dg_ref, du_ref):
        g, da = g_ref[...].astype(F32), da_ref[...].astype(F32)
        sg = _sigmoid(g)
        dg_ref[...] = (da * u_ref[...].astype(F32) * (sg * (1.0 + g * (1.0 - sg)))).astype(BF16)
        du_ref[...] = (da * (g * sg)).astype(BF16)

    act = jax.ShapeDtypeStruct((t, f), BF16)
    return pl.pallas_call(
        body, name=name, grid=(t // tm,), out_shape=(act, act),
        in_specs=[_rb(tm, f)] * 3, out_specs=(_rb(tm, f), _rb(tm, f)), compiler_params=_params("parallel"))(dact, gate, up)


def _loss_head(x2, target, g, name):
    t, d = x2.shape
    tm = _pick(t, (128,))

    def body(x_ref, t_ref, g_ref, loss_ref, dx_ref, dxb_ref, dg_ref):
        xv, gv = x_ref[...], g_ref[...]
        err = xv * _rms_inv(xv) * gv - t_ref[...]
        dx, dgrow = _rms_bwd_rows(err * (1.0 / d), xv, gv)
        dx_ref[...] = dx
        dxb_ref[...] = dx.astype(BF16)

        @pl.when(pl.program_id(0) == 0)
        def _():
            loss_ref[...] = jnp.zeros_like(loss_ref)
            dg_ref[...] = jnp.zeros_like(dg_ref)

        loss_ref[...] += (0.5 / d) * jnp.sum(jnp.sum(err * err, axis=1, keepdims=True), axis=0, keepdims=True)
        dg_ref[...] += jnp.sum(dgrow, axis=0, keepdims=True)

    return pl.pallas_call(
        body, name=name, grid=(t // tm,),
        out_shape=(jax.ShapeDtypeStruct((1, 1), F32), jax.ShapeDtypeStruct((t, d), F32),
                   jax.ShapeDtypeStruct((t, d), BF16), jax.ShapeDtypeStruct((1, d), F32)),
        in_specs=[_rb(tm, d), _rb(tm, d), _vec(d)],
        out_specs=(pl.BlockSpec((1, 1), lambda i: (0, 0)), _rb(tm, d), _rb(tm, d), _vec(d)),
        compiler_params=_params("arbitrary"))(x2, target, g)


def _adamw(parts, w, m, v, name):
    r, c = w.shape
    tr = r if r * c <= ADAM_BLOCK_ELEMS else _pick(r, tuple(s for s in (512, 256, 128, 64, 32, 16, 8) if s * c <= ADAM_BLOCK_ELEMS))

    def body(p_ref, w_ref, m_ref, v_ref, g_ref, d_ref, nm_ref, nv_ref):
        g = p_ref[0].astype(F32)
        for s in range(1, N_DEV):
            g = g + p_ref[s].astype(F32)
        nm = ADAM_B1 * m_ref[...] + (1.0 - ADAM_B1) * g
        nv = ADAM_B2 * v_ref[...] + (1.0 - ADAM_B2) * (g * g)
        m_hat = nm / (1.0 - ADAM_B1 ** ADAM_STEP)
        v_hat = nv / (1.0 - ADAM_B2 ** ADAM_STEP)
        g_ref[...] = g
        d_ref[...] = -ADAM_LR * (m_hat / (jnp.sqrt(v_hat) + ADAM_EPS) + ADAM_WD * w_ref[...])
        nm_ref[...] = nm
        nv_ref[...] = nv

    blk = pl.BlockSpec((tr, c), lambda i: (i, 0))
    out = jax.ShapeDtypeStruct((r, c), F32)
    return pl.pallas_call(
        body, name=name, grid=(r // tr,), out_shape=(out, out, out, out),
        in_specs=[pl.BlockSpec((N_DEV, tr, c), lambda i: (0, i, 0)), blk, blk, blk],
        out_specs=(blk, blk, blk, blk), compiler_params=_params("parallel"))(parts, w, m, v)


def _gathered_cols(w, name):
    g = _all_gather(w.astype(BF16), name)
    return jnp.transpose(g, (1, 0, 2)).reshape(w.shape[0], N_DEV * w.shape[1])


def _col_parts(dw):
    k, n8 = dw.shape
    return jnp.transpose(dw.reshape(k, N_DEV, n8 // N_DEV), (1, 0, 2))


def kernel(x, positions, g_mix, w_in, b_gate, conv_w, g_q_a, w_q_b, g_kv_a, w_kv_b, w_branch, w_out, g_ffn, w_ffn_gate, w_ffn_up, w_ffn_down, g_final, loss_target, m_g_mix, m_w_in, m_b_gate, m_conv_w, m_g_q_a, m_w_q_b, m_g_kv_a, m_w_kv_b, m_w_branch, m_w_out, m_g_ffn, m_w_ffn_gate, m_w_ffn_up, m_w_ffn_down, m_g_final, v_g_mix, v_w_in, v_b_gate, v_conv_w, v_g_q_a, v_w_q_b, v_g_kv_a, v_w_kv_b, v_w_branch, v_w_out, v_g_ffn, v_w_ffn_gate, v_w_ffn_up, v_w_ffn_down, v_g_final):
    given = dict(locals())
    xs = x[0]
    t, d = xs.shape
    cc = d // 2
    n_heads = cc // V_HEAD
    in_cols = N_DEV * w_in.shape[2]
    q_off, kv_off, kr_off = 3 * cc, 3 * cc + Q_LORA, 3 * cc + Q_LORA + KV_LORA
    head_cols = kr_off + QK_ROPE
    gate_off = -(-(kr_off + LANE) // d) * d
    assert in_cols == head_cols + 2 * d and q_off % Q_LORA == 0 and kv_off % KV_LORA == 0 and kr_off % LANE == 0
    d_ff = N_DEV * w_ffn_gate.shape[2]
    ffp = -(-d_ff // 1024) * 1024

    w_in_full = _gathered_cols(w_in[0], "ag_w_in")
    w_all = jnp.concatenate([w_in_full[:, :head_cols], jnp.zeros((d, gate_off - head_cols), BF16),
                             w_in_full[:, head_cols:]], axis=1)
    wq = _gathered_cols(w_q_b[0], "ag_w_q_b").reshape(Q_LORA, n_heads, QK_NOPE + QK_ROPE)
    wq_pad = jnp.pad(wq, ((0, 0), (0, 0), (0, HEAD_PAD - QK_NOPE - QK_ROPE))).reshape(Q_LORA, n_heads * HEAD_PAD)
    wkv = _gathered_cols(w_kv_b[0], "ag_w_kv_b").reshape(KV_LORA, n_heads, 2, QK_NOPE)
    wkv_perm = jnp.transpose(wkv, (0, 2, 1, 3)).reshape(KV_LORA, 2 * n_heads * QK_NOPE)
    wbr = _gathered_cols(w_branch[0].reshape(2 * cc, -1), "ag_w_branch")
    wb_a, wb_b = wbr[:cc], wbr[cc:]
    wo = _all_gather(w_out[0].astype(BF16), "ag_w_out").reshape(d, d)
    wg = jnp.pad(_gathered_cols(w_ffn_gate[0], "ag_w_ffn_gate"), ((0, 0), (0, ffp - d_ff)))
    wu = jnp.pad(_gathered_cols(w_ffn_up[0], "ag_w_ffn_up"), ((0, 0), (0, ffp - d_ff)))
    wd = jnp.pad(_all_gather(w_ffn_down[0].astype(BF16), "ag_w_ffn_down").reshape(d_ff, d), ((0, ffp - d_ff), (0, 0)))
    cw = jnp.transpose(_all_gather(conv_w[0], "ag_conv_w"), (1, 0, 2)).reshape(3, cc)

    inv_freq = ROPE_THETA ** (-jnp.arange(0, QK_ROPE, 2, dtype=F32) / QK_ROPE)
    ang = positions[0].astype(F32)[:, None] * inv_freq[None, :]
    cos, sin = jnp.cos(ang), jnp.sin(ang)
    z32, z64 = jnp.zeros((t, 32), F32), jnp.zeros((t, 64), F32)
    tabs = (jnp.concatenate([cos, cos, jnp.ones((t, 64), F32)], axis=1),
            jnp.concatenate([-sin, z32, z64], axis=1),
            jnp.concatenate([z32, sin, z64], axis=1))

    h = _rms_fwd(xs, g_mix, "rms_mix")
    z_all = _matmul(h, w_all, name="mm_z")
    y_a = _conv_fwd(z_all, cw, cc, "conv_fwd")
    qn, kvn = _latent_norm(z_all, g_q_a, g_kv_a, q_off, kv_off, "latent_norm")
    q_pad = _rope_q(_matmul(qn, wq_pad, name="mm_q", out_dtype=F32), tabs, n_heads, 1.0, BF16, "rope_q")
    kv = _matmul(kvn, wkv_perm, name="mm_kv")
    k_pad = _rope_k(kv, z_all, tabs, n_heads, kr_off, "rope_k")
    y_b = _attn_fwd(q_pad, k_pad, kv, n_heads, "attn_fwd")
    ybr_a = _matmul(y_a, wb_a, name="mm_br_a")
    ybr_b = _matmul(y_b, wb_b, name="mm_br_b")
    merged = _merge_fwd(z_all, b_gate, ybr_a, ybr_b, gate_off, "merge_fwd")
    x1 = _matmul(merged, wo, name="mm_out", res=xs, out_dtype=F32)
    h2 = _rms_fwd(x1, g_ffn, "rms_ffn")
    gate = _matmul(h2, wg, name="mm_gate")
    up = _matmul(h2, wu, name="mm_up")
    act = _swiglu_fwd(gate, up, "swiglu_fwd")
    x2 = _matmul(act, wd, name="mm_down", res=x1, out_dtype=F32)
    loss_part, dx2, dx2b, dg_final = _loss_head(x2, loss_target[0], g_final.reshape(1, d), "loss_head")

    dact = _matmul(dx2b, wd, tb=True, name="mm_d_act")
    dwd = _matmul(act, dx2b, ta=True, name="mm_dw_down")
    dgate, dup = _swiglu_bwd(dact, gate, up, "swiglu_bwd")
    dh2 = _matmul(dgate, wg, tb=True, name="mm_d_h2_gate", out_dtype=F32)
    dh2 = _matmul(dup, wu, tb=True, name="mm_d_h2_up", res=dh2, out_dtype=F32)
    dwg = _matmul(h2, dgate, ta=True, name="mm_dw_gate")
    dwu = _matmul(h2, dup, ta=True, name="mm_dw_up")
    dx1, dx1b, dg_ffn = _rms_bwd(dh2, x1, g_ffn, dx2, "rms_ffn_bwd")
    dmerged = _matmul(dx1b, wo, tb=True, name="mm_d_merged")
    dwo = _matmul(merged, dx1b, ta=True, name="mm_dw_out")
    dybr_a, dybr_b, dzg, db_gate = _merge_bwd(dmerged, z_all, b_gate, ybr_a, ybr_b, gate_off, "merge_bwd")
    dy_a = _matmul(dybr_a, wb_a, tb=True, name="mm_d_y_a")
    dy_b = _matmul(dybr_b, wb_b, tb=True, name="mm_d_y_b")
    dwb_a = _matmul(y_a, dybr_a, ta=True, name="mm_dw_br_a")
    dwb_b = _matmul(y_b, dybr_b, ta=True, name="mm_dw_br_b")
    dq_pad, dk_pad, dv = _attn_bwd(q_pad, k_pad, kv, dy_b, n_heads, "attn_bwd")
    dq_raw = _rope_q(dq_pad, tabs, n_heads, -1.0, BF16, "rope_q_bwd")
    dkv, dkr = _rope_k_bwd(dk_pad, dv, tabs, n_heads, "rope_k_bwd")
    dwq = _matmul(qn, dq_raw, ta=True, name="mm_dw_q")
    dqn = _matmul(dq_raw, wq_pad, tb=True, name="mm_d_qn")
    dwkv = _matmul(kvn, dkv, ta=True, name="mm_dw_kv")
    dkvn = _matmul(dkv, wkv_perm, tb=True, name="mm_d_kvn")
    dqa, dkva, dg_q, dg_kv = _latent_norm_bwd(dqn, dkvn, z_all, g_q_a, g_kv_a, q_off, kv_off, "latent_norm_bwd")
    dcb, dcc, dch, dcw = _conv_bwd(dy_a, z_all, cw, cc, "conv_bwd")
    dz_all = jnp.concatenate([dcb, dcc, dch, dqa, dkva, dkr, jnp.zeros((t, gate_off - kr_off - LANE), BF16), dzg], axis=1)
    dw_all = _matmul(h, dz_all, ta=True, name="mm_dw_in")
    dh = _matmul(dz_all, w_all, tb=True, name="mm_d_h", out_dtype=F32)
    dx, _, dg_mix = _rms_bwd(dh, xs, g_mix, dx1, "rms_mix_bwd")

    def update(parts, wname, shard_shape):
        recv = _exchange(parts, "rs_" + wname)
        shp = given[wname].shape
        outs = _adamw(recv, given[wname].reshape(shard_shape), given["m_" + wname].reshape(shard_shape),
                      given["v_" + wname].reshape(shard_shape), "adamw_" + wname)
        return tuple(o.reshape(shp) for o in outs)

    results = {}
    dw_in_full = jnp.concatenate([dw_all[:, :head_cols], dw_all[:, gate_off:]], axis=1)
    results["w_in"] = update(_col_parts(dw_in_full), "w_in", w_in.shape[1:])
    dwq_full = dwq.reshape(Q_LORA, n_heads, HEAD_PAD)[:, :, :QK_NOPE + QK_ROPE].reshape(Q_LORA, -1)
    results["w_q_b"] = update(_col_parts(dwq_full), "w_q_b", w_q_b.shape[1:])
    dwkv_full = jnp.transpose(dwkv.reshape(KV_LORA, 2, n_heads, QK_NOPE), (0, 2, 1, 3)).reshape(KV_LORA, -1)
    results["w_kv_b"] = update(_col_parts(dwkv_full), "w_kv_b", w_kv_b.shape[1:])
    dwbr = _col_parts(jnp.concatenate([dwb_a, dwb_b], axis=0))
    results["w_branch"] = update(dwbr, "w_branch", (2 * cc, d // N_DEV))
    results["w_out"] = update(dwo.reshape(N_DEV, d // N_DEV, d), "w_out", w_out.shape[1:])
    results["w_ffn_gate"] = update(_col_parts(dwg[:, :d_ff]), "w_ffn_gate", w_ffn_gate.shape[1:])
    results["w_ffn_up"] = update(_col_parts(dwu[:, :d_ff]), "w_ffn_up", w_ffn_up.shape[1:])
    results["w_ffn_down"] = update(dwd[:d_ff].reshape(N_DEV, d_ff // N_DEV, d), "w_ffn_down", w_ffn_down.shape[1:])

    small = [("g_mix", dg_mix), ("b_gate", db_gate), ("g_q_a", dg_q), ("g_kv_a", dg_kv), ("g_ffn", dg_ffn),
             ("g_final", dg_final), ("conv_w", dcw.reshape(1, 3 * cc))]
    packed = _all_gather(jnp.concatenate([p for _, p in small], axis=1), "ag_small_grads")
    off = 0
    for wname, p in small:
        n = p.shape[1]
        parts = packed[:, :, off:off + n]
        off += n
        shp = given[wname].shape
        if wname == "conv_w":
            width = conv_w.shape[2]
            me = 4 * lax.axis_index("x") + 2 * lax.axis_index("y") + lax.axis_index("c")
            parts = lax.dynamic_slice_in_dim(parts.reshape(N_DEV, 3, cc), me * width, width, axis=2)
            flat = (3, width)
        else:
            flat = (1, n)
        outs = _adamw(parts, given[wname].reshape(flat), given["m_" + wname].reshape(flat),
                      given["v_" + wname].reshape(flat), "adamw_" + wname)
        results[wname] = tuple(o.reshape(shp) for o in outs)

    loss = lax.psum(loss_part[0, 0], MESH_AXES)
    order = ["g_mix", "w_in", "b_gate", "conv_w", "g_q_a", "w_q_b", "g_kv_a", "w_kv_b", "w_branch", "w_out", "g_ffn",
             "w_ffn_gate", "w_ffn_up", "w_ffn_down", "g_final"]
    out = [loss, dx[None]]
    for k in range(4):
        out += [results[n][k] for n in order]
    return tuple(out)
```

```python
import functools
import math

import jax
import jax.numpy as jnp
from jax import lax
from jax.experimental import pallas as pl
from jax.experimental.pallas import tpu as pltpu

F32 = jnp.float32
BF16 = jnp.bfloat16
N_DEV = 8
MESH_AXES = ("x", "y", "c")
MESH = pl.DeviceIdType.MESH

QK_NOPE = 128
QK_ROPE = 64
V_HEAD = 128
HEAD_PAD = 256
Q_LORA = 1024
KV_LORA = 512
ROPE_THETA = 10000.0
RMS_EPS = 1e-6
SOFTMAX_SCALE = 1.0 / math.sqrt(QK_NOPE + QK_ROPE)
ADAM_LR, ADAM_B1, ADAM_B2, ADAM_EPS, ADAM_WD, ADAM_STEP = 0.001, 0.9, 0.999, 1e-08, 0.01, 10

VMEM_LIMIT = 60 * 1024 * 1024
LANE = 128
ADAM_BLOCK_ELEMS = 1 << 18


def _pick(n, cands=(1024, 512, 256, 128)):
    for c in cands:
        if n % c == 0:
            return c
    return n


def _params(*sem):
    return pltpu.CompilerParams(dimension_semantics=sem, vmem_limit_bytes=VMEM_LIMIT)


def _rb(tm, c, cb=0):
    return pl.BlockSpec((tm, c), lambda i: (i, cb))


def _vec(c, cb=0):
    return pl.BlockSpec((1, c), lambda i: (0, cb))


def _all_gather(x, name):
    def body(x_ref, out_ref, send_sems, recv_sems, local_sem):
        x_, y_, c_ = lax.axis_index("x"), lax.axis_index("y"), lax.axis_index("c")
        me, sibling = (x_, y_, c_), (x_, y_, 1 - c_)
        chips = [(1 - x_, y_), (x_, 1 - y_), (1 - x_, 1 - y_)]

        def slot(px, py, pc):
            return out_ref.at[4 * px + 2 * py + pc]

        def copy(k, block, to, src=None):
            return pltpu.make_async_remote_copy(
                src_ref=slot(*block) if src is None else src, dst_ref=slot(*block),
                send_sem=send_sems.at[k], recv_sem=recv_sems.at[k], device_id=to, device_id_type=MESH)

        mine = pltpu.make_async_copy(x_ref, slot(*me), local_sem)
        mine.start()
        first = [copy(0, me, sibling, src=x_ref)]
        first += [copy(1 + j, me, (*chip, c_), src=x_ref) for j, chip in enumerate(chips)]
        for cp in first:
            cp.start()
        passed = [copy(4 + j, (*chip, c_), sibling) for j, chip in enumerate(chips)]
        for j, chip in enumerate(chips):
            copy(1 + j, (*chip, c_), me).wait_recv()
            passed[j].start()
        copy(0, sibling, me).wait_recv()
        for j, chip in enumerate(chips):
            copy(4 + j, (*chip, 1 - c_), me).wait_recv()
        for cp in first + passed:
            cp.wait_send()
        mine.wait()

    return pl.pallas_call(
        body, name=name,
        out_shape=jax.ShapeDtypeStruct((N_DEV,) + x.shape, x.dtype),
        in_specs=[pl.BlockSpec(memory_space=pl.ANY)],
        out_specs=pl.BlockSpec(memory_space=pl.ANY),
        scratch_shapes=[pltpu.SemaphoreType.DMA((7,)), pltpu.SemaphoreType.DMA((7,)), pltpu.SemaphoreType.DMA(())],
    )(x)


_HBM = pl.BlockSpec(memory_space=pltpu.HBM)
_SEM = pl.BlockSpec(memory_space=pltpu.SEMAPHORE)
_ANY = pl.BlockSpec(memory_space=pl.ANY)
_EFFECT = pltpu.SideEffectType.DATAFLOW_SIDE_EFFECTING


def _me_index():
    return 4 * lax.axis_index("x") + 2 * lax.axis_index("y") + lax.axis_index("c")


def _own_slot_only(block, me):
    return lax.dynamic_update_index_in_dim(lax.empty((N_DEV,) + block.shape, block.dtype), block, me, 0)


def _split_start(bufs, sem_shape, issue, name):
    n = len(bufs)
    rows, per_row = sem_shape
    sem = lambda sems, i, k: sems.at[i * per_row + k]

    def body(*refs):
        issue(refs[:n], refs[n], refs[n + 1], sem)
        refs[-1][...] = jnp.zeros_like(refs[-1])

    sems = pltpu.SemaphoreType.DMA((rows * per_row,))
    outs = pl.pallas_call(
        body, name=name,
        out_shape=(sems, sems) + tuple(pltpu.HBM(b.shape, b.dtype) for b in bufs) + (jax.ShapeDtypeStruct((8, LANE), F32),),
        in_specs=(_HBM,) * n, out_specs=(_SEM, _SEM) + (_HBM,) * n + (pl.BlockSpec(memory_space=pltpu.VMEM),),
        input_output_aliases={i: 2 + i for i in range(n)},
        compiler_params=pltpu.CompilerParams(has_side_effects=_EFFECT),
    )(*[pltpu.with_memory_space_constraint(b, pltpu.HBM) for b in bufs])
    return outs[0], outs[1], list(outs[2:2 + n]), outs[-1], sem_shape


def _split_wait(started, block_of, after, name):
    send_sems, recv_sems, bufs, _, sem_shape = started
    n = len(bufs)

    def body(*refs):
        x_, y_, c_ = lax.axis_index("x"), lax.axis_index("y"), lax.axis_index("c")
        for i in range(sem_shape[0]):
            blk = block_of(refs, i)
            for k in range(sem_shape[1]):
                cp = pltpu.make_async_remote_copy(
                    src_ref=blk, dst_ref=blk, send_sem=refs[n].at[i * sem_shape[1] + k], recv_sem=refs[n + 1].at[i * sem_shape[1] + k],
                    device_id=(x_, y_, c_), device_id_type=MESH)
                cp.wait_send()
                cp.wait_recv()

    outs = pl.pallas_call(
        body, name=name,
        out_shape=tuple(pltpu.HBM(b.shape, b.dtype) for b in bufs),
        in_specs=(_HBM,) * n + (_SEM, _SEM, _ANY), out_specs=(_HBM,) * n,
        input_output_aliases={i: i for i in range(n)},
        compiler_params=pltpu.CompilerParams(has_side_effects=_EFFECT),
    )(*bufs, send_sems, recv_sems, after)
    return list(outs)


def _gather_start(blocks, me, name):
    def issue(buf_refs, send_sems, recv_sems, sem):
        x_, y_, c_ = lax.axis_index("x"), lax.axis_index("y"), lax.axis_index("c")
        me_idx = 4 * x_ + 2 * y_ + c_
        targets = [(x_, y_, 1 - c_), (1 - x_, y_, c_), (x_, 1 - y_, c_), (1 - x_, 1 - y_, c_)]
        for i, buf in enumerate(buf_refs):
            for k, to in enumerate(targets):
                pltpu.make_async_remote_copy(
                    src_ref=buf.at[me_idx], dst_ref=buf.at[me_idx], send_sem=sem(send_sems, i, k), recv_sem=sem(recv_sems, i, k),
                    device_id=to, device_id_type=MESH).start()

    return _split_start([_own_slot_only(b, me) for b in blocks], (len(blocks), 4), issue, name)


def _forward_start(bufs, name):
    def issue(buf_refs, send_sems, recv_sems, sem):
        x_, y_, c_ = lax.axis_index("x"), lax.axis_index("y"), lax.axis_index("c")
        chips = [(1 - x_, y_), (x_, 1 - y_), (1 - x_, 1 - y_)]
        for i, buf in enumerate(buf_refs):
            for k, (px, py) in enumerate(chips):
                slot = buf.at[4 * px + 2 * py + c_]
                pltpu.make_async_remote_copy(
                    src_ref=slot, dst_ref=slot, send_sem=sem(send_sems, i, k), recv_sem=sem(recv_sems, i, k),
                    device_id=(x_, y_, 1 - c_), device_id_type=MESH).start()

    return _split_start(bufs, (len(bufs), 3), issue, name)


def _exchange_start(parts, me, name):
    n = len(parts)

    def issue(refs, send_sems, recv_sems, sem):
        x_, y_, c_ = lax.axis_index("x"), lax.axis_index("y"), lax.axis_index("c")
        me_idx = 4 * x_ + 2 * y_ + c_
        for i in range(n):
            for d in range(1, N_DEV):
                px = 1 - x_ if d & 4 else x_
                py = 1 - y_ if d & 2 else y_
                pc = 1 - c_ if d & 1 else c_
                pltpu.make_async_remote_copy(
                    src_ref=refs[i].at[4 * px + 2 * py + pc], dst_ref=refs[n + i].at[me_idx],
                    send_sem=sem(send_sems, i, d - 1), recv_sem=sem(recv_sems, i, d - 1),
                    device_id=(px, py, pc), device_id_type=MESH).start()

    lands = [_own_slot_only(lax.dynamic_index_in_dim(p, me, 0, keepdims=False), me) for p in parts]
    return _split_start(list(parts) + lands, (n, N_DEV - 1), issue, name)


def _matmul(a, b, *, name, ta=False, tb=False, res=None, out_dtype=BF16):
    (kdim, m) = a.shape if ta else a.shape[::-1]
    (n, kdim_b) = b.shape if tb else b.shape[::-1]
    assert kdim == kdim_b, (a.shape, b.shape, ta, tb)
    tm, tn, tk = _pick(m), _pick(n), _pick(kdim)
    nk = kdim // tk
    a_spec = pl.BlockSpec((tk, tm), lambda i, j, k: (k, i)) if ta else pl.BlockSpec((tm, tk), lambda i, j, k: (i, k))
    b_spec = pl.BlockSpec((tn, tk), lambda i, j, k: (j, k)) if tb else pl.BlockSpec((tk, tn), lambda i, j, k: (k, j))
    o_spec = pl.BlockSpec((tm, tn), lambda i, j, k: (i, j))
    dims = (((0 if ta else 1,), (1 if tb else 0,)), ((), ()))

    def body(*refs):
        if res is None:
            a_ref, b_ref, o_ref, acc = refs
            r_ref = None
        else:
            a_ref, b_ref, r_ref, o_ref, acc = refs
        k = pl.program_id(2)

        @pl.when(k == 0)
        def _():
            acc[...] = jnp.zeros_like(acc)

        acc[...] += lax.dot_general(a_ref[...], b_ref[...], dims, preferred_element_type=F32)

        @pl.when(k == nk - 1)
        def _():
            v = acc[...]
            if r_ref is not None:
                v = r_ref[...] + v
            o_ref[...] = v.astype(out_dtype)

    operands = (a, b) if res is None else (a, b, res)
    in_specs = [a_spec, b_spec] if res is None else [a_spec, b_spec, o_spec]
    return pl.pallas_call(
        body, name=name, grid=(m // tm, n // tn, nk),
        out_shape=jax.ShapeDtypeStruct((m, n), out_dtype),
        in_specs=in_specs, out_specs=o_spec,
        scratch_shapes=[pltpu.VMEM((tm, tn), F32)],
        compiler_params=_params("parallel", "parallel", "arbitrary"),
    )(*operands)


def _rms_inv(x):
    return lax.rsqrt(jnp.mean(x * x, axis=-1, keepdims=True) + RMS_EPS)


def _rms_fwd(x, g, name):
    t, d = x.shape
    tm = _pick(t, (256, 128))

    def body(x_ref, g_ref, h_ref):
        xv = x_ref[...]
        h_ref[...] = (xv * _rms_inv(xv) * g_ref[...]).astype(BF16)

    return pl.pallas_call(
        body, name=name, grid=(t // tm,), out_shape=jax.ShapeDtypeStruct((t, d), BF16),
        in_specs=[_rb(tm, d), _vec(d)], out_specs=_rb(tm, d), compiler_params=_params("parallel"))(x, g)


def _rms_bwd_rows(dy, xv, g):
    inv = _rms_inv(xv)
    xhat = xv * inv
    dxhat = dy * g
    dx = inv * (dxhat - xhat * jnp.mean(dxhat * xhat, axis=-1, keepdims=True))
    return dx, dy * xhat


def _rms_bwd(dy, x, g, res, name):
    t, d = x.shape
    tm = _pick(t, (128,))

    def body(dy_ref, x_ref, g_ref, r_ref, dx_ref, dxb_ref, dg_ref):
        dx, dgrow = _rms_bwd_rows(dy_ref[...].astype(F32), x_ref[...], g_ref[...])
        dx = r_ref[...] + dx
        dx_ref[...] = dx
        dxb_ref[...] = dx.astype(BF16)

        @pl.when(pl.program_id(0) == 0)
        def _():
            dg_ref[...] = jnp.zeros_like(dg_ref)

        dg_ref[...] += jnp.sum(dgrow, axis=0, keepdims=True)

    return pl.pallas_call(
        body, name=name, grid=(t // tm,),
        out_shape=(jax.ShapeDtypeStruct((t, d), F32), jax.ShapeDtypeStruct((t, d), BF16), jax.ShapeDtypeStruct((1, d), F32)),
        in_specs=[_rb(tm, d), _rb(tm, d), _vec(d), _rb(tm, d)],
        out_specs=(_rb(tm, d), _rb(tm, d), _vec(d)), compiler_params=_params("arbitrary"))(dy, x, g, res)


def _latent_norm(z_all, g_q, g_kv, q_off, kv_off, name):
    t = z_all.shape[0]
    tm = _pick(t, (256, 128))

    def body(qa_ref, kva_ref, gq_ref, gkv_ref, qn_ref, kvn_ref):
        qa = qa_ref[...].astype(F32)
        qn_ref[...] = (qa * _rms_inv(qa) * gq_ref[...]).astype(BF16)
        kva = kva_ref[...].astype(F32)
        kvn_ref[...] = (kva * _rms_inv(kva) * gkv_ref[...]).astype(BF16)

    return pl.pallas_call(
        body, name=name, grid=(t // tm,),
        out_shape=(jax.ShapeDtypeStruct((t, Q_LORA), BF16), jax.ShapeDtypeStruct((t, KV_LORA), BF16)),
        in_specs=[_rb(tm, Q_LORA, q_off // Q_LORA), _rb(tm, KV_LORA, kv_off // KV_LORA), _vec(Q_LORA), _vec(KV_LORA)],
        out_specs=(_rb(tm, Q_LORA), _rb(tm, KV_LORA)), compiler_params=_params("parallel"))(z_all, z_all, g_q, g_kv)


def _latent_norm_bwd(dqn, dkvn, z_all, g_q, g_kv, q_off, kv_off, name):
    t = z_all.shape[0]
    tm = _pick(t, (256, 128))

    def body(dqn_ref, dkvn_ref, qa_ref, kva_ref, gq_ref, gkv_ref, dqa_ref, dkva_ref, dgq_ref, dgkv_ref):
        dqa, dgq = _rms_bwd_rows(dqn_ref[...].astype(F32), qa_ref[...].astype(F32), gq_ref[...])
        dkva, dgkv = _rms_bwd_rows(dkvn_ref[...].astype(F32), kva_ref[...].astype(F32), gkv_ref[...])
        dqa_ref[...] = dqa.astype(BF16)
        dkva_ref[...] = dkva.astype(BF16)

        @pl.when(pl.program_id(0) == 0)
        def _():
            dgq_ref[...] = jnp.zeros_like(dgq_ref)
            dgkv_ref[...] = jnp.zeros_like(dgkv_ref)

        dgq_ref[...] += jnp.sum(dgq, axis=0, keepdims=True)
        dgkv_ref[...] += jnp.sum(dgkv, axis=0, keepdims=True)

    return pl.pallas_call(
        body, name=name, grid=(t // tm,),
        out_shape=(jax.ShapeDtypeStruct((t, Q_LORA), BF16), jax.ShapeDtypeStruct((t, KV_LORA), BF16),
                   jax.ShapeDtypeStruct((1, Q_LORA), F32), jax.ShapeDtypeStruct((1, KV_LORA), F32)),
        in_specs=[_rb(tm, Q_LORA), _rb(tm, KV_LORA), _rb(tm, Q_LORA, q_off // Q_LORA), _rb(tm, KV_LORA, kv_off // KV_LORA),
                  _vec(Q_LORA), _vec(KV_LORA)],
        out_specs=(_rb(tm, Q_LORA), _rb(tm, KV_LORA), _vec(Q_LORA), _vec(KV_LORA)),
        compiler_params=_params("arbitrary"))(dqn, dkvn, z_all, z_all, g_q, g_kv)


def _rot(xv, cos_k, sin_a, sin_b, sign):
    return xv * cos_k + sign * (pltpu.roll(xv, LANE - 32, 1) * sin_a + pltpu.roll(xv, 32, 1) * sin_b)


def _rope_q(q_raw, tabs, n_heads, sign, out_dtype, name):
    t, w = q_raw.shape
    tm = _pick(t, (256, 128))

    def body(q_ref, cos_ref, sa_ref, sb_ref, o_ref):
        cos_k, sin_a, sin_b = cos_ref[...], sa_ref[...], sb_ref[...]
        for h in range(n_heads):
            lo = h * HEAD_PAD
            o_ref[:, lo:lo + LANE] = q_ref[:, lo:lo + LANE].astype(out_dtype)
            o_ref[:, lo + LANE:lo + HEAD_PAD] = _rot(
                q_ref[:, lo + LANE:lo + HEAD_PAD].astype(F32), cos_k, sin_a, sin_b, sign).astype(out_dtype)

    return pl.pallas_call(
        body, name=name, grid=(t // tm,), out_shape=jax.ShapeDtypeStruct((t, w), out_dtype),
        in_specs=[_rb(tm, w), _rb(tm, LANE), _rb(tm, LANE), _rb(tm, LANE)],
        out_specs=_rb(tm, w), compiler_params=_params("parallel"))(q_raw, *tabs)


def _rope_k(kv, z_all, tabs, n_heads, kr_off, name):
    t = kv.shape[0]
    tm = _pick(t, (256, 128))
    wk = n_heads * QK_NOPE

    def body(kn_ref, kr_ref, cos_ref, sa_ref, sb_ref, o_ref):
        krot = _rot(kr_ref[...].astype(F32), cos_ref[...], sa_ref[...], sb_ref[...], 1.0).astype(BF16)
        for h in range(n_heads):
            o_ref[:, h * HEAD_PAD:h * HEAD_PAD + LANE] = kn_ref[:, h * QK_NOPE:(h + 1) * QK_NOPE]
            o_ref[:, h * HEAD_PAD + LANE:(h + 1) * HEAD_PAD] = krot

    return pl.pallas_call(
        body, name=name, grid=(t // tm,), out_shape=jax.ShapeDtypeStruct((t, n_heads * HEAD_PAD), BF16),
        in_specs=[_rb(tm, wk), _rb(tm, LANE, kr_off // LANE), _rb(tm, LANE), _rb(tm, LANE), _rb(tm, LANE)],
        out_specs=_rb(tm, n_heads * HEAD_PAD), compiler_params=_params("parallel"))(kv, z_all, *tabs)


def _rope_k_bwd(dk_pad, dv, tabs, n_heads, name):
    t = dk_pad.shape[0]
    tm = _pick(t, (256, 128))
    wk = n_heads * QK_NOPE

    def body(dk_ref, dv_ref, cos_ref, sa_ref, sb_ref, dkv_ref, dkr_ref):
        acc = dk_ref[:, LANE:HEAD_PAD]
        dkv_ref[:, 0:QK_NOPE] = dk_ref[:, 0:LANE].astype(BF16)
        for h in range(1, n_heads):
            acc = acc + dk_ref[:, h * HEAD_PAD + LANE:(h + 1) * HEAD_PAD]
            dkv_ref[:, h * QK_NOPE:(h + 1) * QK_NOPE] = dk_ref[:, h * HEAD_PAD:h * HEAD_PAD + LANE].astype(BF16)
        dkv_ref[:, wk:] = dv_ref[...].astype(BF16)
        dkr_ref[...] = _rot(acc, cos_ref[...], sa_ref[...], sb_ref[...], -1.0).astype(BF16)

    return pl.pallas_call(
        body, name=name, grid=(t // tm,),
        out_shape=(jax.ShapeDtypeStruct((t, 2 * wk), BF16), jax.ShapeDtypeStruct((t, LANE), BF16)),
        in_specs=[_rb(tm, n_heads * HEAD_PAD), _rb(tm, wk), _rb(tm, LANE), _rb(tm, LANE), _rb(tm, LANE)],
        out_specs=(_rb(tm, 2 * wk), _rb(tm, LANE)), compiler_params=_params("parallel"))(dk_pad, dv, *tabs)


NT_DIMS = (((1,), (1,)), ((), ()))
TN_DIMS = (((0,), (0,)), ((), ()))


def _softmax_rows(q, k):
    s = lax.dot_general(q, k, NT_DIMS, preferred_element_type=F32) * SOFTMAX_SCALE
    e = jnp.exp(s - jnp.max(s, axis=-1, keepdims=True))
    return e * (1.0 / jnp.sum(e, axis=-1, keepdims=True))


def _attn_fwd(q_pad, k_pad, kv, n_heads, name):
    t = q_pad.shape[0]
    tq = _pick(t, (256, 128))

    def body(q_ref, k_ref, v_ref, o_ref):
        p = _softmax_rows(q_ref[...], k_ref[...]).astype(BF16)
        o_ref[...] = jnp.dot(p, v_ref[...], preferred_element_type=F32).astype(BF16)

    return pl.pallas_call(
        body, name=name, grid=(n_heads, t // tq),
        out_shape=jax.ShapeDtypeStruct((t, n_heads * V_HEAD), BF16),
        in_specs=[pl.BlockSpec((tq, HEAD_PAD), lambda h, i: (i, h)),
                  pl.BlockSpec((t, HEAD_PAD), lambda h, i: (0, h)),
                  pl.BlockSpec((t, V_HEAD), lambda h, i: (0, n_heads + h))],
        out_specs=pl.BlockSpec((tq, V_HEAD), lambda h, i: (i, h)),
        compiler_params=_params("parallel", "parallel"))(q_pad, k_pad, kv)


def _attn_bwd(q_pad, k_pad, kv, do, n_heads, name):
    t = q_pad.shape[0]
    tq = _pick(t, (256, 128))
    nq = t // tq

    def body(q_ref, k_ref, v_ref, do_ref, dq_ref, dk_ref, dv_ref):
        @pl.when(pl.program_id(1) == 0)
        def _():
            dk_ref[...] = jnp.zeros_like(dk_ref)
            dv_ref[...] = jnp.zeros_like(dv_ref)

        q, k, dout = q_ref[...], k_ref[...], do_ref[...]
        p = _softmax_rows(q, k)
        dp = lax.dot_general(dout, v_ref[...], NT_DIMS, preferred_element_type=F32)
        ds = (p * (dp - jnp.sum(p * dp, axis=-1, keepdims=True)) * SOFTMAX_SCALE).astype(BF16)
        dq_ref[...] = jnp.dot(ds, k, preferred_element_type=F32)
        dk_ref[...] += lax.dot_general(ds, q, TN_DIMS, preferred_element_type=F32)
        dv_ref[...] += lax.dot_general(p.astype(BF16), dout, TN_DIMS, preferred_element_type=F32)

    return pl.pallas_call(
        body, name=name, grid=(n_heads, nq),
        out_shape=(jax.ShapeDtypeStruct((t, n_heads * HEAD_PAD), F32), jax.ShapeDtypeStruct((t, n_heads * HEAD_PAD), F32),
                   jax.ShapeDtypeStruct((t, n_heads * V_HEAD), F32)),
        in_specs=[pl.BlockSpec((tq, HEAD_PAD), lambda h, i: (i, h)),
                  pl.BlockSpec((t, HEAD_PAD), lambda h, i: (0, h)),
                  pl.BlockSpec((t, V_HEAD), lambda h, i: (0, n_heads + h)),
                  pl.BlockSpec((tq, V_HEAD), lambda h, i: (i, h))],
        out_specs=(pl.BlockSpec((tq, HEAD_PAD), lambda h, i: (i, h)),
                   pl.BlockSpec((t, HEAD_PAD), lambda h, i: (0, h)),
                   pl.BlockSpec((t, V_HEAD), lambda h, i: (0, h))),
        compiler_params=_params("parallel", "arbitrary"))(q_pad, k_pad, kv, do)


def _shift_rows(u, t):
    row = lax.broadcasted_iota(jnp.int32, u.shape, 0)
    prev = jnp.where(row == 0, 0.0, pltpu.roll(u, 1, 0))
    nxt = jnp.where(row == t - 1, 0.0, pltpu.roll(u, t - 1, 0))
    return prev, nxt


def _conv_fwd(z_all, conv_w, cc, name):
    t = z_all.shape[0]
    nb = cc // LANE

    def body(cb_ref, cc_ref, ch_ref, w_ref, y_ref):
        u = cc_ref[...].astype(F32) * ch_ref[...].astype(F32)
        prev, nxt = _shift_rows(u, t)
        w = w_ref[...]
        conv = prev * w[0:1, :] + u * w[1:2, :] + nxt * w[2:3, :]
        y_ref[...] = (cb_ref[...].astype(F32) * conv).astype(BF16)

    col = lambda g: pl.BlockSpec((t, LANE), lambda j: (0, g * nb + j))
    return pl.pallas_call(
        body, name=name, grid=(nb,), out_shape=jax.ShapeDtypeStruct((t, cc), BF16),
        in_specs=[col(0), col(1), col(2), pl.BlockSpec((3, LANE), lambda j: (0, j))],
        out_specs=pl.BlockSpec((t, LANE), lambda j: (0, j)),
        compiler_params=_params("parallel"))(z_all, z_all, z_all, conv_w)


def _conv_bwd(dy, z_all, conv_w, cc, name):
    t = z_all.shape[0]
    nb = cc // LANE

    def body(dy_ref, cb_ref, cc_ref, ch_ref, w_ref, dcb_ref, dcc_ref, dch_ref, dw_ref):
        c_c, c_h = cc_ref[...].astype(F32), ch_ref[...].astype(F32)
        u = c_c * c_h
        prev, nxt = _shift_rows(u, t)
        w = w_ref[...]
        dyv = dy_ref[...].astype(F32)
        dcb_ref[...] = (dyv * (prev * w[0:1, :] + u * w[1:2, :] + nxt * w[2:3, :])).astype(BF16)
        dconv = dyv * cb_ref[...].astype(F32)
        dw_ref[0:1, :] = jnp.sum(dconv * prev, axis=0, keepdims=True)
        dw_ref[1:2, :] = jnp.sum(dconv * u, axis=0, keepdims=True)
        dw_ref[2:3, :] = jnp.sum(dconv * nxt, axis=0, keepdims=True)
        dprev, dnxt = _shift_rows(dconv, t)
        du = dnxt * w[0:1, :] + dconv * w[1:2, :] + dprev * w[2:3, :]
        dcc_ref[...] = (du * c_h).astype(BF16)
        dch_ref[...] = (du * c_c).astype(BF16)

    col = lambda g: pl.BlockSpec((t, LANE), lambda j: (0, g * nb + j))
    one = pl.BlockSpec((t, LANE), lambda j: (0, j))
    wsp = pl.BlockSpec((3, LANE), lambda j: (0, j))
    act = jax.ShapeDtypeStruct((t, cc), BF16)
    return pl.pallas_call(
        body, name=name, grid=(nb,),
        out_shape=(act, act, act, jax.ShapeDtypeStruct((3, cc), F32)),
        in_specs=[one, col(0), col(1), col(2), wsp],
        out_specs=(one, one, one, wsp),
        compiler_params=_params("parallel"))(dy, z_all, z_all, z_all, conv_w)


def _sigmoid(v):
    return 1.0 / (1.0 + jnp.exp(-v))


def _merge_fwd(z_all, b_gate, y_a, y_b, gate_off, name):
    t, d = y_a.shape
    tm = _pick(t, (128,))
    gb = gate_off // d

    def body(za_ref, zb_ref, ba_ref, bb_ref, ya_ref, yb_ref, m_ref):
        ga = _sigmoid(za_ref[...].astype(F32) + ba_ref[...])
        gbv = _sigmoid(zb_ref[...].astype(F32) + bb_ref[...])
        m_ref[...] = (ga * ya_ref[...].astype(F32) + gbv * yb_ref[...].astype(F32)).astype(BF16)

    return pl.pallas_call(
        body, name=name, grid=(t // tm,), out_shape=jax.ShapeDtypeStruct((t, d), BF16),
        in_specs=[_rb(tm, d, gb), _rb(tm, d, gb + 1), _vec(d, 0), _vec(d, 1), _rb(tm, d), _rb(tm, d)],
        out_specs=_rb(tm, d), compiler_params=_params("parallel"))(z_all, z_all, b_gate, b_gate, y_a, y_b)


def _merge_bwd(dm, z_all, b_gate, y_a, y_b, gate_off, name):
    t, d = y_a.shape
    tm = _pick(t, (128,))
    gb = gate_off // d

    def body(dm_ref, za_ref, zb_ref, ba_ref, bb_ref, ya_ref, yb_ref, dya_ref, dyb_ref, dzg_ref, db_ref):
        dmv = dm_ref[...].astype(F32)
        ga = _sigmoid(za_ref[...].astype(F32) + ba_ref[...])
        gbv = _sigmoid(zb_ref[...].astype(F32) + bb_ref[...])
        dya_ref[...] = (dmv * ga).astype(BF16)
        dyb_ref[...] = (dmv * gbv).astype(BF16)
        dza = dmv * ya_ref[...].astype(F32) * (ga * (1.0 - ga))
        dzb = dmv * yb_ref[...].astype(F32) * (gbv * (1.0 - gbv))
        dzg_ref[:, 0:d] = dza.astype(BF16)
        dzg_ref[:, d:2 * d] = dzb.astype(BF16)

        @pl.when(pl.program_id(0) == 0)
        def _():
            db_ref[...] = jnp.zeros_like(db_ref)

        db_ref[:, 0:d] += jnp.sum(dza, axis=0, keepdims=True)
        db_ref[:, d:2 * d] += jnp.sum(dzb, axis=0, keepdims=True)

    act = jax.ShapeDtypeStruct((t, d), BF16)
    return pl.pallas_call(
        body, name=name, grid=(t // tm,),
        out_shape=(act, act, jax.ShapeDtypeStruct((t, 2 * d), BF16), jax.ShapeDtypeStruct((1, 2 * d), F32)),
        in_specs=[_rb(tm, d), _rb(tm, d, gb), _rb(tm, d, gb + 1), _vec(d, 0), _vec(d, 1), _rb(tm, d), _rb(tm, d)],
        out_specs=(_rb(tm, d), _rb(tm, d), _rb(tm, 2 * d), _vec(2 * d)),
        compiler_params=_params("arbitrary"))(dm, z_all, z_all, b_gate, b_gate, y_a, y_b)


def _swiglu_fwd(gate, up, name):
    t, f = gate.shape
    tm = _pick(t, (128,))

    def body(g_ref, u_ref, a_ref):
        g = g_ref[...].astype(F32)
        a_ref[...] = (g * _sigmoid(g) * u_ref[...].astype(F32)).astype(BF16)

    return pl.pallas_call(
        body, name=name, grid=(t // tm,), out_shape=jax.ShapeDtypeStruct((t, f), BF16),
        in_specs=[_rb(tm, f), _rb(tm, f)], out_specs=_rb(tm, f), compiler_params=_params("parallel"))(gate, up)


def _swiglu_bwd(dact, gate, up, name):
    t, f = gate.shape
    tm = _pick(t, (128,))

    def body(da_ref, g_ref, u_ref, dg_ref, du_ref):
        g, da = g_ref[...].astype(F32), da_ref[...].astype(F32)
        sg = _sigmoid(g)
        dg_ref[...] = (da * u_ref[...].astype(F32) * (sg * (1.0 + g * (1.0 - sg)))).astype(BF16)
        du_ref[...] = (da * (g * sg)).astype(BF16)

    act = jax.ShapeDtypeStruct((t, f), BF16)
    return pl.pallas_call(
        body, name=name, grid=(t // tm,), out_shape=(act, act),
        in_specs=[_rb(tm, f)] * 3, out_specs=(_rb(tm, f), _rb(tm, f)), compiler_params=_params("parallel"))(dact, gate, up)


def _loss_head(x2, target, g, name):
    t, d = x2.shape
    tm = _pick(t, (128,))

    def body(x_ref, t_ref, g_ref, loss_ref, dx_ref, dxb_ref, dg_ref):
        xv, gv = x_ref[...], g_ref[...]
        err = xv * _rms_inv(xv) * gv - t_ref[...]
        dx, dgrow = _rms_bwd_rows(err * (1.0 / d), xv, gv)
        dx_ref[...] = dx
        dxb_ref[...] = dx.astype(BF16)

        @pl.when(pl.program_id(0) == 0)
        def _():
            loss_ref[...] = jnp.zeros_like(loss_ref)
            dg_ref[...] = jnp.zeros_like(dg_ref)

        loss_ref[...] += (0.5 / d) * jnp.sum(jnp.sum(err * err, axis=1, keepdims=True), axis=0, keepdims=True)
        dg_ref[...] += jnp.sum(dgrow, axis=0, keepdims=True)

    return pl.pallas_call(
        body, name=name, grid=(t // tm,),
        out_shape=(jax.ShapeDtypeStruct((1, 1), F32), jax.ShapeDtypeStruct((t, d), F32),
                   jax.ShapeDtypeStruct((t, d), BF16), jax.ShapeDtypeStruct((1, d), F32)),
        in_specs=[_rb(tm, d), _rb(tm, d), _vec(d)],
        out_specs=(pl.BlockSpec((1, 1), lambda i: (0, 0)), _rb(tm, d), _rb(tm, d), _vec(d)),
        compiler_params=_params("arbitrary"))(x2, target, g)


def _adamw(parts, w, m, v, name):
    r, c = w.shape
    tr = r if r * c <= ADAM_BLOCK_ELEMS else _pick(r, tuple(s for s in (512, 256, 128, 64, 32, 16, 8) if s * c <= ADAM_BLOCK_ELEMS))

    def body(p_ref, w_ref, m_ref, v_ref, g_ref, d_ref, nm_ref, nv_ref):
        g = p_ref[0].astype(F32)
        for s in range(1, N_DEV):
            g = g + p_ref[s].astype(F32)
        nm = ADAM_B1 * m_ref[...] + (1.0 - ADAM_B1) * g
        nv = ADAM_B2 * v_ref[...] + (1.0 - ADAM_B2) * (g * g)
        m_hat = nm / (1.0 - ADAM_B1 ** ADAM_STEP)
        v_hat = nv / (1.0 - ADAM_B2 ** ADAM_STEP)
        g_ref[...] = g
        d_ref[...] = -ADAM_LR * (m_hat / (jnp.sqrt(v_hat) + ADAM_EPS) + ADAM_WD * w_ref[...])
        nm_ref[...] = nm
        nv_ref[...] = nv

    blk = pl.BlockSpec((tr, c), lambda i: (i, 0))
    out = jax.ShapeDtypeStruct((r, c), F32)
    return pl.pallas_call(
        body, name=name, grid=(r // tr,), out_shape=(out, out, out, out),
        in_specs=[pl.BlockSpec((N_DEV, tr, c), lambda i: (0, i, 0)), blk, blk, blk],
        out_specs=(blk, blk, blk, blk), compiler_params=_params("parallel"))(parts, w, m, v)


def _cols_of(g):
    return jnp.transpose(g, (1, 0, 2)).reshape(g.shape[1], N_DEV * g.shape[2])


def _tie(value, token):
    return lax.optimization_barrier((value, token))[0]


def _col_parts(dw):
    k, n8 = dw.shape
    return jnp.transpose(dw.reshape(k, N_DEV, n8 // N_DEV), (1, 0, 2))


def kernel(x, positions, g_mix, w_in, b_gate, conv_w, g_q_a, w_q_b, g_kv_a, w_kv_b, w_branch, w_out, g_ffn, w_ffn_gate, w_ffn_up, w_ffn_down, g_final, loss_target, m_g_mix, m_w_in, m_b_gate, m_conv_w, m_g_q_a, m_w_q_b, m_g_kv_a, m_w_kv_b, m_w_branch, m_w_out, m_g_ffn, m_w_ffn_gate, m_w_ffn_up, m_w_ffn_down, m_g_final, v_g_mix, v_w_in, v_b_gate, v_conv_w, v_g_q_a, v_w_q_b, v_g_kv_a, v_w_kv_b, v_w_branch, v_w_out, v_g_ffn, v_w_ffn_gate, v_w_ffn_up, v_w_ffn_down, v_g_final):
    given = dict(locals())
    xs = x[0]
    t, d = xs.shape
    cc = d // 2
    n_heads = cc // V_HEAD
    in_cols = N_DEV * w_in.shape[2]
    q_off, kv_off, kr_off = 3 * cc, 3 * cc + Q_LORA, 3 * cc + Q_LORA + KV_LORA
    head_cols = kr_off + QK_ROPE
    head_pad = -(-(kr_off + LANE) // 1024) * 1024
    assert in_cols == head_cols + 2 * d and q_off % Q_LORA == 0 and kv_off % KV_LORA == 0 and kr_off % LANE == 0
    fs = w_ffn_gate.shape[2]
    fsp = -(-fs // LANE) * LANE
    ffp = N_DEV * fsp
    me = _me_index()
    bf = lambda a: a.astype(BF16)
    one_slot = lambda refs, i: refs[i].at[0]

    g_in = _gather_start([bf(w_in[0]), conv_w[0]], me, "ag1_start_in")
    g_mix_w = _gather_start([_tie(bf(w_q_b[0]), g_in[3]), bf(w_kv_b[0]), bf(w_branch[0].reshape(2 * cc, -1)), bf(w_out[0])],
                            me, "ag1_start_mix")
    g_ffn_w = _gather_start([_tie(bf(jnp.pad(w_ffn_gate[0], ((0, 0), (0, fsp - fs)))), g_mix_w[3]),
                             bf(jnp.pad(w_ffn_up[0], ((0, 0), (0, fsp - fs)))),
                             bf(jnp.pad(w_ffn_down[0], ((0, fsp - fs), (0, 0))))], me, "ag1_start_ffn")
    f_in = _forward_start(_split_wait(g_in, one_slot, g_ffn_w[3], "ag1_wait_in"), "ag2_start_in")
    w_in_g, cw_g = _split_wait(f_in, one_slot, f_in[3], "ag2_wait_in")
    w_in_full = _cols_of(w_in_g)
    w_head = jnp.concatenate([w_in_full[:, :head_cols], jnp.zeros((d, head_pad - head_cols), BF16)], axis=1)
    w_gate = w_in_full[:, head_cols:]
    cw = _cols_of(cw_g)

    inv_freq = ROPE_THETA ** (-jnp.arange(0, QK_ROPE, 2, dtype=F32) / QK_ROPE)
    ang = positions[0].astype(F32)[:, None] * inv_freq[None, :]
    cos, sin = jnp.cos(ang), jnp.sin(ang)
    z32, z64 = jnp.zeros((t, 32), F32), jnp.zeros((t, 64), F32)
    tabs = (jnp.concatenate([cos, cos, jnp.ones((t, 64), F32)], axis=1),
            jnp.concatenate([-sin, z32, z64], axis=1),
            jnp.concatenate([z32, sin, z64], axis=1))

    h = _rms_fwd(xs, g_mix, "rms_mix")
    z_head = _matmul(h, w_head, name="mm_z_head")
    zg = _matmul(h, w_gate, name="mm_z_gate")
    f_mix = _forward_start(_split_wait(g_mix_w, one_slot, zg, "ag1_wait_mix"), "ag2_start_mix")
    z_head = _tie(z_head, f_mix[3])
    y_a = _conv_fwd(z_head, cw, cc, "conv_fwd")
    qn, kvn = _latent_norm(z_head, g_q_a, g_kv_a, q_off, kv_off, "latent_norm")
    wq_g, wkv_g, wbr_g, wo_g = _split_wait(f_mix, one_slot, qn, "ag2_wait_mix")
    wq = _cols_of(wq_g).reshape(Q_LORA, n_heads, QK_NOPE + QK_ROPE)
    wq_pad = jnp.pad(wq, ((0, 0), (0, 0), (0, HEAD_PAD - QK_NOPE - QK_ROPE))).reshape(Q_LORA, n_heads * HEAD_PAD)
    wkv = _cols_of(wkv_g).reshape(KV_LORA, n_heads, 2, QK_NOPE)
    wkv_perm = jnp.transpose(wkv, (0, 2, 1, 3)).reshape(KV_LORA, 2 * n_heads * QK_NOPE)
    wbr = _cols_of(wbr_g)
    wb_a, wb_b = wbr[:cc], wbr[cc:]
    wo = wo_g.reshape(d, d)
    q_pad = _rope_q(_matmul(qn, wq_pad, name="mm_q", out_dtype=F32), tabs, n_heads, 1.0, BF16, "rope_q")
    kv = _matmul(kvn, wkv_perm, name="mm_kv")
    k_pad = _rope_k(kv, z_head, tabs, n_heads, kr_off, "rope_k")
    y_b = _attn_fwd(q_pad, k_pad, kv, n_heads, "attn_fwd")
    f_ffn = _forward_start(_split_wait(g_ffn_w, one_slot, y_b, "ag1_wait_ffn"), "ag2_start_ffn")
    ybr_a = _matmul(_tie(y_a, f_ffn[3]), wb_a, name="mm_br_a")
    ybr_b = _matmul(y_b, wb_b, name="mm_br_b")
    merged = _merge_fwd(zg, b_gate, ybr_a, ybr_b, 0, "merge_fwd")
    x1 = _matmul(merged, wo, name="mm_out", res=xs, out_dtype=F32)
    h2 = _rms_fwd(x1, g_ffn, "rms_ffn")
    wg_g, wu_g, wd_g = _split_wait(f_ffn, one_slot, h2, "ag2_wait_ffn")
    wg, wu, wd = _cols_of(wg_g), _cols_of(wu_g), wd_g.reshape(ffp, d)
    gate = _matmul(h2, wg, name="mm_gate")
    up = _matmul(h2, wu, name="mm_up")
    act = _swiglu_fwd(gate, up, "swiglu_fwd")
    x2 = _matmul(act, wd, name="mm_down", res=x1, out_dtype=F32)
    loss_part, dx2, dx2b, dg_final = _loss_head(x2, loss_target[0], g_final.reshape(1, d), "loss_head")

    dact = _matmul(dx2b, wd, tb=True, name="mm_d_act")
    dwd = _matmul(act, dx2b, ta=True, name="mm_dw_down")
    r_down = _exchange_start([dwd.reshape(N_DEV, fsp, d)[:, :fs]], me, "rs_start_down")
    dgate, dup = _swiglu_bwd(_tie(dact, r_down[3]), gate, up, "swiglu_bwd")
    dwg = _matmul(h2, dgate, ta=True, name="mm_dw_gate")
    dwu = _matmul(h2, dup, ta=True, name="mm_dw_up")
    ffn_parts = lambda dw: jnp.transpose(dw.reshape(d, N_DEV, fsp)[:, :, :fs], (1, 0, 2))
    r_gate_up = _exchange_start([ffn_parts(dwg), ffn_parts(dwu)], me, "rs_start_gate_up")
    dh2 = _matmul(_tie(dgate, r_gate_up[3]), wg, tb=True, name="mm_d_h2_gate", out_dtype=F32)
    dh2 = _matmul(dup, wu, tb=True, name="mm_d_h2_up", res=dh2, out_dtype=F32)
    dx1, dx1b, dg_ffn = _rms_bwd(dh2, x1, g_ffn, dx2, "rms_ffn_bwd")
    dwo = _matmul(merged, dx1b, ta=True, name="mm_dw_out")
    r_out = _exchange_start([dwo.reshape(N_DEV, d // N_DEV, d)], me, "rs_start_out")
    dmerged = _matmul(_tie(dx1b, r_out[3]), wo, tb=True, name="mm_d_merged")
    dybr_a, dybr_b, dzg, db_gate = _merge_bwd(dmerged, zg, b_gate, ybr_a, ybr_b, 0, "merge_bwd")
    dwb_a = _matmul(y_a, dybr_a, ta=True, name="mm_dw_br_a")
    dwb_b = _matmul(y_b, dybr_b, ta=True, name="mm_dw_br_b")
    r_br = _exchange_start([_col_parts(jnp.concatenate([dwb_a, dwb_b], axis=0))], me, "rs_start_branch")
    dy_a = _matmul(_tie(dybr_a, r_br[3]), wb_a, tb=True, name="mm_d_y_a")
    dy_b = _matmul(dybr_b, wb_b, tb=True, name="mm_d_y_b")
    dw_gate = _matmul(h, dzg, ta=True, name="mm_dw_in_gate")
    dh_gate = _matmul(dzg, w_gate, tb=True, name="mm_d_h_gate", out_dtype=F32)
    dq_pad, dk_pad, dv = _attn_bwd(q_pad, k_pad, kv, dy_b, n_heads, "attn_bwd")
    dq_raw = _rope_q(dq_pad, tabs, n_heads, -1.0, BF16, "rope_q_bwd")
    dkv, dkr = _rope_k_bwd(dk_pad, dv, tabs, n_heads, "rope_k_bwd")
    dwq = _matmul(qn, dq_raw, ta=True, name="mm_dw_q")
    dwkv = _matmul(kvn, dkv, ta=True, name="mm_dw_kv")
    dwq_full = dwq.reshape(Q_LORA, n_heads, HEAD_PAD)[:, :, :QK_NOPE + QK_ROPE].reshape(Q_LORA, -1)
    dwkv_full = jnp.transpose(dwkv.reshape(KV_LORA, 2, n_heads, QK_NOPE), (0, 2, 1, 3)).reshape(KV_LORA, -1)
    r_qkv = _exchange_start([_col_parts(dwq_full), _col_parts(dwkv_full)], me, "rs_start_q_kv")
    dqn = _matmul(_tie(dq_raw, r_qkv[3]), wq_pad, tb=True, name="mm_d_qn")
    dkvn = _matmul(dkv, wkv_perm, tb=True, name="mm_d_kvn")
    dqa, dkva, dg_q, dg_kv = _latent_norm_bwd(dqn, dkvn, z_head, g_q_a, g_kv_a, q_off, kv_off, "latent_norm_bwd")
    dcb, dcc, dch, dcw = _conv_bwd(dy_a, z_head, cw, cc, "conv_bwd")
    dz_head = jnp.concatenate([dcb, dcc, dch, dqa, dkva, dkr, jnp.zeros((t, head_pad - kr_off - LANE), BF16)], axis=1)
    dw_head = _matmul(h, dz_head, ta=True, name="mm_dw_in_head")
    r_in = _exchange_start([_col_parts(jnp.concatenate([dw_head[:, :head_cols], dw_gate], axis=1))], me, "rs_start_in")
    dh = _matmul(_tie(dz_head, r_in[3]), w_head, tb=True, name="mm_d_h_head", res=dh_gate, out_dtype=F32)
    dx, _, dg_mix = _rms_bwd(dh, xs, g_mix, dx1, "rms_mix_bwd")

    results = {}
    last = dx

    def update(started, wnames, shard_shapes):
        nonlocal last
        n = len(wnames)
        recvs = _split_wait(started, one_slot, last, "rs_wait_" + wnames[0])[n:]
        for recv, wname, shard_shape in zip(recvs, wnames, shard_shapes):
            shp = given[wname].shape
            outs = _adamw(recv, given[wname].reshape(shard_shape), given["m_" + wname].reshape(shard_shape),
                          given["v_" + wname].reshape(shard_shape), "adamw_" + wname)
            results[wname] = tuple(o.reshape(shp) for o in outs)
            last = outs[0]

    update(r_down, ["w_ffn_down"], [w_ffn_down.shape[1:]])
    update(r_gate_up, ["w_ffn_gate", "w_ffn_up"], [w_ffn_gate.shape[1:], w_ffn_up.shape[1:]])
    update(r_out, ["w_out"], [w_out.shape[1:]])
    update(r_br, ["w_branch"], [(2 * cc, d // N_DEV)])
    update(r_qkv, ["w_q_b", "w_kv_b"], [w_q_b.shape[1:], w_kv_b.shape[1:]])

    small = [("g_mix", dg_mix), ("b_gate", db_gate), ("g_q_a", dg_q), ("g_kv_a", dg_kv), ("g_ffn", dg_ffn),
             ("g_final", dg_final), ("conv_w", dcw.reshape(1, 3 * cc))]
    packed = _all_gather(_tie(jnp.concatenate([p for _, p in small], axis=1), last), "ag_small_grads")
    off = 0
    for wname, p in small:
        n = p.shape[1]
        parts = packed[:, :, off:off + n]
        off += n
        shp = given[wname].shape
        if wname == "conv_w":
            width = conv_w.shape[2]
            me = 4 * lax.axis_index("x") + 2 * lax.axis_index("y") + lax.axis_index("c")
            parts = lax.dynamic_slice_in_dim(parts.reshape(N_DEV, 3, cc), me * width, width, axis=2)
            flat = (3, width)
        else:
            flat = (1, n)
        outs = _adamw(parts, given[wname].reshape(flat), given["m_" + wname].reshape(flat),
                      given["v_" + wname].reshape(flat), "adamw_" + wname)
        results[wname] = tuple(o.reshape(shp) for o in outs)
        last = outs[0]
    update(r_in, ["w_in"], [w_in.shape[1:]])

    loss = lax.psum(loss_part[0, 0], MESH_AXES)
    order = ["g_mix", "w_in", "b_gate", "conv_w", "g_q_a", "w_q_b", "g_kv_a", "w_kv_b", "w_branch", "w_out", "g_ffn",
             "w_ffn_gate", "w_ffn_up", "w_ffn_down", "g_final"]
    out = [loss, dx[None]]
    for k in range(4):
        out += [results[n][k] for n in order]
    return tuple(out)
```

```python
import functools
import math

import jax
import jax.numpy as jnp
from jax import lax
from jax.experimental import pallas as pl
from jax.experimental.pallas import tpu as pltpu

F32 = jnp.float32
BF16 = jnp.bfloat16
N_DEV = 8
MESH_AXES = ("x", "y", "c")
MESH = pl.DeviceIdType.MESH

QK_NOPE = 128
QK_ROPE = 64
V_HEAD = 128
HEAD_PAD = 256
Q_LORA = 1024
KV_LORA = 512
ROPE_THETA = 10000.0
RMS_EPS = 1e-6
SOFTMAX_SCALE = 1.0 / math.sqrt(QK_NOPE + QK_ROPE)
ADAM_LR, ADAM_B1, ADAM_B2, ADAM_EPS, ADAM_WD, ADAM_STEP = 0.001, 0.9, 0.999, 1e-08, 0.01, 10

VMEM_LIMIT = 60 * 1024 * 1024
LANE = 128
ADAM_BLOCK_ELEMS = 1 << 18


def _pick(n, cands=(1024, 512, 256, 128)):
    for c in cands:
        if n % c == 0:
            return c
    return n


def _params(*sem):
    return pltpu.CompilerParams(dimension_semantics=sem, vmem_limit_bytes=VMEM_LIMIT)


def _rb(tm, c, cb=0):
    return pl.BlockSpec((tm, c), lambda i: (i, cb))


def _vec(c, cb=0):
    return pl.BlockSpec((1, c), lambda i: (0, cb))


def _all_gather(x, name, after):
    def body(x_ref, after_ref, out_ref, send_sems, recv_sems, local_sem):
        x_, y_, c_ = lax.axis_index("x"), lax.axis_index("y"), lax.axis_index("c")
        me, sibling = (x_, y_, c_), (x_, y_, 1 - c_)
        chips = [(1 - x_, y_), (x_, 1 - y_), (1 - x_, 1 - y_)]

        def slot(px, py, pc):
            return out_ref.at[4 * px + 2 * py + pc]

        def copy(k, block, to, src=None):
            return pltpu.make_async_remote_copy(
                src_ref=slot(*block) if src is None else src, dst_ref=slot(*block),
                send_sem=send_sems.at[k], recv_sem=recv_sems.at[k], device_id=to, device_id_type=MESH)

        mine = pltpu.make_async_copy(x_ref, slot(*me), local_sem)
        mine.start()
        first = [copy(0, me, sibling, src=x_ref)]
        first += [copy(1 + j, me, (*chip, c_), src=x_ref) for j, chip in enumerate(chips)]
        for cp in first:
            cp.start()
        passed = [copy(4 + j, (*chip, c_), sibling) for j, chip in enumerate(chips)]
        for j, chip in enumerate(chips):
            copy(1 + j, (*chip, c_), me).wait_recv()
            passed[j].start()
        copy(0, sibling, me).wait_recv()
        for j, chip in enumerate(chips):
            copy(4 + j, (*chip, 1 - c_), me).wait_recv()
        for cp in first + passed:
            cp.wait_send()
        mine.wait()

    return pl.pallas_call(
        body, name=name,
        out_shape=jax.ShapeDtypeStruct((N_DEV,) + x.shape, x.dtype),
        in_specs=[pl.BlockSpec(memory_space=pl.ANY), pl.BlockSpec(memory_space=pl.ANY)],
        out_specs=pl.BlockSpec(memory_space=pl.ANY),
        scratch_shapes=[pltpu.SemaphoreType.DMA((7,)), pltpu.SemaphoreType.DMA((7,)), pltpu.SemaphoreType.DMA(())],
    )(x, after)


_HBM = pl.BlockSpec(memory_space=pltpu.HBM)
_SEM = pl.BlockSpec(memory_space=pltpu.SEMAPHORE)
_ANY = pl.BlockSpec(memory_space=pl.ANY)
_EFFECT = pltpu.SideEffectType.DATAFLOW_SIDE_EFFECTING


def _me_index():
    return 4 * lax.axis_index("x") + 2 * lax.axis_index("y") + lax.axis_index("c")


def _own_slot_only(block, me):
    return lax.dynamic_update_index_in_dim(lax.empty((N_DEV,) + block.shape, block.dtype), block, me, 0)


def _split_start(bufs, sem_shape, issue, name, after=None):
    n = len(bufs)
    rows, per_row = sem_shape
    sem = lambda sems, i, k: sems.at[i * per_row + k]
    first_out = n + (after is not None)

    def body(*refs):
        issue(refs[:n], refs[first_out], refs[first_out + 1], sem)
        refs[-1][...] = jnp.zeros_like(refs[-1])

    sems = pltpu.SemaphoreType.DMA((rows * per_row,))
    operands = [pltpu.with_memory_space_constraint(b, pltpu.HBM) for b in bufs] + ([] if after is None else [after])
    outs = pl.pallas_call(
        body, name=name,
        out_shape=(sems, sems) + tuple(pltpu.HBM(b.shape, b.dtype) for b in bufs) + (jax.ShapeDtypeStruct((8, LANE), F32),),
        in_specs=(_HBM,) * n + (_ANY,) * (after is not None),
        out_specs=(_SEM, _SEM) + (_HBM,) * n + (pl.BlockSpec(memory_space=pltpu.VMEM),),
        input_output_aliases={i: 2 + i for i in range(n)},
        compiler_params=pltpu.CompilerParams(has_side_effects=_EFFECT),
    )(*operands)
    return outs[0], outs[1], list(outs[2:2 + n]), outs[-1], sem_shape


def _split_wait(started, block_of, after, name):
    send_sems, recv_sems, bufs, _, sem_shape = started
    n = len(bufs)

    def body(*refs):
        x_, y_, c_ = lax.axis_index("x"), lax.axis_index("y"), lax.axis_index("c")
        for i in range(sem_shape[0]):
            blk = block_of(refs, i)
            for k in range(sem_shape[1]):
                cp = pltpu.make_async_remote_copy(
                    src_ref=blk, dst_ref=blk, send_sem=refs[n].at[i * sem_shape[1] + k], recv_sem=refs[n + 1].at[i * sem_shape[1] + k],
                    device_id=(x_, y_, c_), device_id_type=MESH)
                cp.wait_send()
                cp.wait_recv()

    outs = pl.pallas_call(
        body, name=name,
        out_shape=tuple(pltpu.HBM(b.shape, b.dtype) for b in bufs),
        in_specs=(_HBM,) * n + (_SEM, _SEM, _ANY), out_specs=(_HBM,) * n,
        input_output_aliases={i: i for i in range(n)},
        compiler_params=pltpu.CompilerParams(has_side_effects=_EFFECT),
    )(*bufs, send_sems, recv_sems, after)
    return list(outs)


def _gather_start(blocks, me, name, after=None):
    def issue(buf_refs, send_sems, recv_sems, sem):
        x_, y_, c_ = lax.axis_index("x"), lax.axis_index("y"), lax.axis_index("c")
        me_idx = 4 * x_ + 2 * y_ + c_
        targets = [(x_, y_, 1 - c_), (1 - x_, y_, c_), (x_, 1 - y_, c_), (1 - x_, 1 - y_, c_)]
        for i, buf in enumerate(buf_refs):
            for k, to in enumerate(targets):
                pltpu.make_async_remote_copy(
                    src_ref=buf.at[me_idx], dst_ref=buf.at[me_idx], send_sem=sem(send_sems, i, k), recv_sem=sem(recv_sems, i, k),
                    device_id=to, device_id_type=MESH).start()

    return _split_start([_own_slot_only(b, me) for b in blocks], (len(blocks), 4), issue, name, after)


def _forward_start(bufs, name):
    def issue(buf_refs, send_sems, recv_sems, sem):
        x_, y_, c_ = lax.axis_index("x"), lax.axis_index("y"), lax.axis_index("c")
        chips = [(1 - x_, y_), (x_, 1 - y_), (1 - x_, 1 - y_)]
        for i, buf in enumerate(buf_refs):
            for k, (px, py) in enumerate(chips):
                slot = buf.at[4 * px + 2 * py + c_]
                pltpu.make_async_remote_copy(
                    src_ref=slot, dst_ref=slot, send_sem=sem(send_sems, i, k), recv_sem=sem(recv_sems, i, k),
                    device_id=(x_, y_, 1 - c_), device_id_type=MESH).start()

    return _split_start(bufs, (len(bufs), 3), issue, name)


def _exchange_start(parts, me, name):
    n = len(parts)

    def issue(refs, send_sems, recv_sems, sem):
        x_, y_, c_ = lax.axis_index("x"), lax.axis_index("y"), lax.axis_index("c")
        me_idx = 4 * x_ + 2 * y_ + c_
        for i in range(n):
            for d in range(1, N_DEV):
                px = 1 - x_ if d & 4 else x_
                py = 1 - y_ if d & 2 else y_
                pc = 1 - c_ if d & 1 else c_
                pltpu.make_async_remote_copy(
                    src_ref=refs[i].at[4 * px + 2 * py + pc], dst_ref=refs[n + i].at[me_idx],
                    send_sem=sem(send_sems, i, d - 1), recv_sem=sem(recv_sems, i, d - 1),
                    device_id=(px, py, pc), device_id_type=MESH).start()

    lands = [_own_slot_only(lax.dynamic_index_in_dim(p, me, 0, keepdims=False), me) for p in parts]
    return _split_start(list(parts) + lands, (n, N_DEV - 1), issue, name)


def _matmul(a, b, *, name, ta=False, tb=False, res=None, out_dtype=BF16, after=None):
    (kdim, m) = a.shape if ta else a.shape[::-1]
    (n, kdim_b) = b.shape if tb else b.shape[::-1]
    assert kdim == kdim_b, (a.shape, b.shape, ta, tb)
    tm, tn, tk = _pick(m), _pick(n), _pick(kdim)
    nk = kdim // tk
    a_spec = pl.BlockSpec((tk, tm), lambda i, j, k: (k, i)) if ta else pl.BlockSpec((tm, tk), lambda i, j, k: (i, k))
    b_spec = pl.BlockSpec((tn, tk), lambda i, j, k: (j, k)) if tb else pl.BlockSpec((tk, tn), lambda i, j, k: (k, j))
    o_spec = pl.BlockSpec((tm, tn), lambda i, j, k: (i, j))
    dims = (((0 if ta else 1,), (1 if tb else 0,)), ((), ()))

    def body(*refs):
        a_ref, b_ref = refs[:2]
        r_ref = None if res is None else refs[2]
        o_ref, acc = refs[-2:]
        k = pl.program_id(2)

        @pl.when(k == 0)
        def _():
            acc[...] = jnp.zeros_like(acc)

        acc[...] += lax.dot_general(a_ref[...], b_ref[...], dims, preferred_element_type=F32)

        @pl.when(k == nk - 1)
        def _():
            v = acc[...]
            if r_ref is not None:
                v = r_ref[...] + v
            o_ref[...] = v.astype(out_dtype)

    operands = [a, b] + ([] if res is None else [res]) + ([] if after is None else [after])
    in_specs = [a_spec, b_spec] + ([] if res is None else [o_spec]) + ([] if after is None else [_ANY])
    return pl.pallas_call(
        body, name=name, grid=(m // tm, n // tn, nk),
        out_shape=jax.ShapeDtypeStruct((m, n), out_dtype),
        in_specs=in_specs, out_specs=o_spec,
        scratch_shapes=[pltpu.VMEM((tm, tn), F32)],
        compiler_params=_params("parallel", "parallel", "arbitrary"),
    )(*operands)


def _rms_inv(x):
    return lax.rsqrt(jnp.mean(x * x, axis=-1, keepdims=True) + RMS_EPS)


def _rms_fwd(x, g, name, after=None):
    t, d = x.shape
    tm = _pick(t, (256, 128))

    def body(x_ref, g_ref, *rest):
        xv = x_ref[...]
        rest[-1][...] = (xv * _rms_inv(xv) * g_ref[...]).astype(BF16)

    return pl.pallas_call(
        body, name=name, grid=(t // tm,), out_shape=jax.ShapeDtypeStruct((t, d), BF16),
        in_specs=[_rb(tm, d), _vec(d)] + ([] if after is None else [_ANY]), out_specs=_rb(tm, d),
        compiler_params=_params("parallel"))(x, g, *([] if after is None else [after]))


def _rms_bwd_rows(dy, xv, g):
    inv = _rms_inv(xv)
    xhat = xv * inv
    dxhat = dy * g
    dx = inv * (dxhat - xhat * jnp.mean(dxhat * xhat, axis=-1, keepdims=True))
    return dx, dy * xhat


def _rms_bwd(dy, x, g, res, name):
    t, d = x.shape
    tm = _pick(t, (128,))

    def body(dy_ref, x_ref, g_ref, r_ref, dx_ref, dxb_ref, dg_ref):
        dx, dgrow = _rms_bwd_rows(dy_ref[...].astype(F32), x_ref[...], g_ref[...])
        dx = r_ref[...] + dx
        dx_ref[...] = dx
        dxb_ref[...] = dx.astype(BF16)

        @pl.when(pl.program_id(0) == 0)
        def _():
            dg_ref[...] = jnp.zeros_like(dg_ref)

        dg_ref[...] += jnp.sum(dgrow, axis=0, keepdims=True)

    return pl.pallas_call(
        body, name=name, grid=(t // tm,),
        out_shape=(jax.ShapeDtypeStruct((t, d), F32), jax.ShapeDtypeStruct((t, d), BF16), jax.ShapeDtypeStruct((1, d), F32)),
        in_specs=[_rb(tm, d), _rb(tm, d), _vec(d), _rb(tm, d)],
        out_specs=(_rb(tm, d), _rb(tm, d), _vec(d)), compiler_params=_params("arbitrary"))(dy, x, g, res)


def _latent_norm(z_all, g_q, g_kv, q_off, kv_off, name):
    t = z_all.shape[0]
    tm = _pick(t, (256, 128))

    def body(qa_ref, kva_ref, gq_ref, gkv_ref, qn_ref, kvn_ref):
        qa = qa_ref[...].astype(F32)
        qn_ref[...] = (qa * _rms_inv(qa) * gq_ref[...]).astype(BF16)
        kva = kva_ref[...].astype(F32)
        kvn_ref[...] = (kva * _rms_inv(kva) * gkv_ref[...]).astype(BF16)

    return pl.pallas_call(
        body, name=name, grid=(t // tm,),
        out_shape=(jax.ShapeDtypeStruct((t, Q_LORA), BF16), jax.ShapeDtypeStruct((t, KV_LORA), BF16)),
        in_specs=[_rb(tm, Q_LORA, q_off // Q_LORA), _rb(tm, KV_LORA, kv_off // KV_LORA), _vec(Q_LORA), _vec(KV_LORA)],
        out_specs=(_rb(tm, Q_LORA), _rb(tm, KV_LORA)), compiler_params=_params("parallel"))(z_all, z_all, g_q, g_kv)


def _latent_norm_bwd(dqn, dkvn, z_all, g_q, g_kv, q_off, kv_off, name):
    t = z_all.shape[0]
    tm = _pick(t, (256, 128))

    def body(dqn_ref, dkvn_ref, qa_ref, kva_ref, gq_ref, gkv_ref, dqa_ref, dkva_ref, dgq_ref, dgkv_ref):
        dqa, dgq = _rms_bwd_rows(dqn_ref[...].astype(F32), qa_ref[...].astype(F32), gq_ref[...])
        dkva, dgkv = _rms_bwd_rows(dkvn_ref[...].astype(F32), kva_ref[...].astype(F32), gkv_ref[...])
        dqa_ref[...] = dqa.astype(BF16)
        dkva_ref[...] = dkva.astype(BF16)

        @pl.when(pl.program_id(0) == 0)
        def _():
            dgq_ref[...] = jnp.zeros_like(dgq_ref)
            dgkv_ref[...] = jnp.zeros_like(dgkv_ref)

        dgq_ref[...] += jnp.sum(dgq, axis=0, keepdims=True)
        dgkv_ref[...] += jnp.sum(dgkv, axis=0, keepdims=True)

    return pl.pallas_call(
        body, name=name, grid=(t // tm,),
        out_shape=(jax.ShapeDtypeStruct((t, Q_LORA), BF16), jax.ShapeDtypeStruct((t, KV_LORA), BF16),
                   jax.ShapeDtypeStruct((1, Q_LORA), F32), jax.ShapeDtypeStruct((1, KV_LORA), F32)),
        in_specs=[_rb(tm, Q_LORA), _rb(tm, KV_LORA), _rb(tm, Q_LORA, q_off // Q_LORA), _rb(tm, KV_LORA, kv_off // KV_LORA),
                  _vec(Q_LORA), _vec(KV_LORA)],
        out_specs=(_rb(tm, Q_LORA), _rb(tm, KV_LORA), _vec(Q_LORA), _vec(KV_LORA)),
        compiler_params=_params("arbitrary"))(dqn, dkvn, z_all, z_all, g_q, g_kv)


def _rot(xv, cos_k, sin_a, sin_b, sign):
    return xv * cos_k + sign * (pltpu.roll(xv, LANE - 32, 1) * sin_a + pltpu.roll(xv, 32, 1) * sin_b)


def _rope_q(q_raw, tabs, n_heads, sign, out_dtype, name):
    t, w = q_raw.shape
    tm = _pick(t, (256, 128))

    def body(q_ref, cos_ref, sa_ref, sb_ref, o_ref):
        cos_k, sin_a, sin_b = cos_ref[...], sa_ref[...], sb_ref[...]
        for h in range(n_heads):
            lo = h * HEAD_PAD
            o_ref[:, lo:lo + LANE] = q_ref[:, lo:lo + LANE].astype(out_dtype)
            o_ref[:, lo + LANE:lo + HEAD_PAD] = _rot(
                q_ref[:, lo + LANE:lo + HEAD_PAD].astype(F32), cos_k, sin_a, sin_b, sign).astype(out_dtype)

    return pl.pallas_call(
        body, name=name, grid=(t // tm,), out_shape=jax.ShapeDtypeStruct((t, w), out_dtype),
        in_specs=[_rb(tm, w), _rb(tm, LANE), _rb(tm, LANE), _rb(tm, LANE)],
        out_specs=_rb(tm, w), compiler_params=_params("parallel"))(q_raw, *tabs)


def _rope_k(kv, z_all, tabs, n_heads, kr_off, name):
    t = kv.shape[0]
    tm = _pick(t, (256, 128))
    wk = n_heads * QK_NOPE

    def body(kn_ref, kr_ref, cos_ref, sa_ref, sb_ref, o_ref):
        krot = _rot(kr_ref[...].astype(F32), cos_ref[...], sa_ref[...], sb_ref[...], 1.0).astype(BF16)
        for h in range(n_heads):
            o_ref[:, h * HEAD_PAD:h * HEAD_PAD + LANE] = kn_ref[:, h * QK_NOPE:(h + 1) * QK_NOPE]
            o_ref[:, h * HEAD_PAD + LANE:(h + 1) * HEAD_PAD] = krot

    return pl.pallas_call(
        body, name=name, grid=(t // tm,), out_shape=jax.ShapeDtypeStruct((t, n_heads * HEAD_PAD), BF16),
        in_specs=[_rb(tm, wk), _rb(tm, LANE, kr_off // LANE), _rb(tm, LANE), _rb(tm, LANE), _rb(tm, LANE)],
        out_specs=_rb(tm, n_heads * HEAD_PAD), compiler_params=_params("parallel"))(kv, z_all, *tabs)


def _rope_k_bwd(dk_pad, dv, tabs, n_heads, name):
    t = dk_pad.shape[0]
    tm = _pick(t, (256, 128))
    wk = n_heads * QK_NOPE

    def body(dk_ref, dv_ref, cos_ref, sa_ref, sb_ref, dkv_ref, dkr_ref):
        acc = dk_ref[:, LANE:HEAD_PAD]
        dkv_ref[:, 0:QK_NOPE] = dk_ref[:, 0:LANE].astype(BF16)
        for h in range(1, n_heads):
            acc = acc + dk_ref[:, h * HEAD_PAD + LANE:(h + 1) * HEAD_PAD]
            dkv_ref[:, h * QK_NOPE:(h + 1) * QK_NOPE] = dk_ref[:, h * HEAD_PAD:h * HEAD_PAD + LANE].astype(BF16)
        dkv_ref[:, wk:] = dv_ref[...].astype(BF16)
        dkr_ref[...] = _rot(acc, cos_ref[...], sa_ref[...], sb_ref[...], -1.0).astype(BF16)

    return pl.pallas_call(
        body, name=name, grid=(t // tm,),
        out_shape=(jax.ShapeDtypeStruct((t, 2 * wk), BF16), jax.ShapeDtypeStruct((t, LANE), BF16)),
        in_specs=[_rb(tm, n_heads * HEAD_PAD), _rb(tm, wk), _rb(tm, LANE), _rb(tm, LANE), _rb(tm, LANE)],
        out_specs=(_rb(tm, 2 * wk), _rb(tm, LANE)), compiler_params=_params("parallel"))(dk_pad, dv, *tabs)


NT_DIMS = (((1,), (1,)), ((), ()))
TN_DIMS = (((0,), (0,)), ((), ()))


def _softmax_rows(q, k):
    s = lax.dot_general(q, k, NT_DIMS, preferred_element_type=F32) * SOFTMAX_SCALE
    e = jnp.exp(s - jnp.max(s, axis=-1, keepdims=True))
    return e * (1.0 / jnp.sum(e, axis=-1, keepdims=True))


def _attn_fwd(q_pad, k_pad, kv, n_heads, name):
    t = q_pad.shape[0]
    tq = _pick(t, (256, 128))

    def body(q_ref, k_ref, v_ref, o_ref):
        p = _softmax_rows(q_ref[...], k_ref[...]).astype(BF16)
        o_ref[...] = jnp.dot(p, v_ref[...], preferred_element_type=F32).astype(BF16)

    return pl.pallas_call(
        body, name=name, grid=(n_heads, t // tq),
        out_shape=jax.ShapeDtypeStruct((t, n_heads * V_HEAD), BF16),
        in_specs=[pl.BlockSpec((tq, HEAD_PAD), lambda h, i: (i, h)),
                  pl.BlockSpec((t, HEAD_PAD), lambda h, i: (0, h)),
                  pl.BlockSpec((t, V_HEAD), lambda h, i: (0, n_heads + h))],
        out_specs=pl.BlockSpec((tq, V_HEAD), lambda h, i: (i, h)),
        compiler_params=_params("parallel", "parallel"))(q_pad, k_pad, kv)


def _attn_bwd(q_pad, k_pad, kv, do, n_heads, name):
    t = q_pad.shape[0]
    tq = _pick(t, (256, 128))
    nq = t // tq

    def body(q_ref, k_ref, v_ref, do_ref, dq_ref, dk_ref, dv_ref):
        @pl.when(pl.program_id(1) == 0)
        def _():
            dk_ref[...] = jnp.zeros_like(dk_ref)
            dv_ref[...] = jnp.zeros_like(dv_ref)

        q, k, dout = q_ref[...], k_ref[...], do_ref[...]
        p = _softmax_rows(q, k)
        dp = lax.dot_general(dout, v_ref[...], NT_DIMS, preferred_element_type=F32)
        ds = (p * (dp - jnp.sum(p * dp, axis=-1, keepdims=True)) * SOFTMAX_SCALE).astype(BF16)
        dq_ref[...] = jnp.dot(ds, k, preferred_element_type=F32)
        dk_ref[...] += lax.dot_general(ds, q, TN_DIMS, preferred_element_type=F32)
        dv_ref[...] += lax.dot_general(p.astype(BF16), dout, TN_DIMS, preferred_element_type=F32)

    return pl.pallas_call(
        body, name=name, grid=(n_heads, nq),
        out_shape=(jax.ShapeDtypeStruct((t, n_heads * HEAD_PAD), F32), jax.ShapeDtypeStruct((t, n_heads * HEAD_PAD), F32),
                   jax.ShapeDtypeStruct((t, n_heads * V_HEAD), F32)),
        in_specs=[pl.BlockSpec((tq, HEAD_PAD), lambda h, i: (i, h)),
                  pl.BlockSpec((t, HEAD_PAD), lambda h, i: (0, h)),
                  pl.BlockSpec((t, V_HEAD), lambda h, i: (0, n_heads + h)),
                  pl.BlockSpec((tq, V_HEAD), lambda h, i: (i, h))],
        out_specs=(pl.BlockSpec((tq, HEAD_PAD), lambda h, i: (i, h)),
                   pl.BlockSpec((t, HEAD_PAD), lambda h, i: (0, h)),
                   pl.BlockSpec((t, V_HEAD), lambda h, i: (0, h))),
        compiler_params=_params("parallel", "arbitrary"))(q_pad, k_pad, kv, do)


def _shift_rows(u, t):
    row = lax.broadcasted_iota(jnp.int32, u.shape, 0)
    prev = jnp.where(row == 0, 0.0, pltpu.roll(u, 1, 0))
    nxt = jnp.where(row == t - 1, 0.0, pltpu.roll(u, t - 1, 0))
    return prev, nxt


def _conv_fwd(z_all, conv_w, cc, name):
    t = z_all.shape[0]
    nb = cc // LANE

    def body(cb_ref, cc_ref, ch_ref, w_ref, y_ref):
        u = cc_ref[...].astype(F32) * ch_ref[...].astype(F32)
        prev, nxt = _shift_rows(u, t)
        w = w_ref[...]
        conv = prev * w[0:1, :] + u * w[1:2, :] + nxt * w[2:3, :]
        y_ref[...] = (cb_ref[...].astype(F32) * conv).astype(BF16)

    col = lambda g: pl.BlockSpec((t, LANE), lambda j: (0, g * nb + j))
    return pl.pallas_call(
        body, name=name, grid=(nb,), out_shape=jax.ShapeDtypeStruct((t, cc), BF16),
        in_specs=[col(0), col(1), col(2), pl.BlockSpec((3, LANE), lambda j: (0, j))],
        out_specs=pl.BlockSpec((t, LANE), lambda j: (0, j)),
        compiler_params=_params("parallel"))(z_all, z_all, z_all, conv_w)


def _conv_bwd(dy, z_all, conv_w, cc, name):
    t = z_all.shape[0]
    nb = cc // LANE

    def body(dy_ref, cb_ref, cc_ref, ch_ref, w_ref, dcb_ref, dcc_ref, dch_ref, dw_ref):
        c_c, c_h = cc_ref[...].astype(F32), ch_ref[...].astype(F32)
        u = c_c * c_h
        prev, nxt = _shift_rows(u, t)
        w = w_ref[...]
        dyv = dy_ref[...].astype(F32)
        dcb_ref[...] = (dyv * (prev * w[0:1, :] + u * w[1:2, :] + nxt * w[2:3, :])).astype(BF16)
        dconv = dyv * cb_ref[...].astype(F32)
        dw_ref[0:1, :] = jnp.sum(dconv * prev, axis=0, keepdims=True)
        dw_ref[1:2, :] = jnp.sum(dconv * u, axis=0, keepdims=True)
        dw_ref[2:3, :] = jnp.sum(dconv * nxt, axis=0, keepdims=True)
        dprev, dnxt = _shift_rows(dconv, t)
        du = dnxt * w[0:1, :] + dconv * w[1:2, :] + dprev * w[2:3, :]
        dcc_ref[...] = (du * c_h).astype(BF16)
        dch_ref[...] = (du * c_c).astype(BF16)

    col = lambda g: pl.BlockSpec((t, LANE), lambda j: (0, g * nb + j))
    one = pl.BlockSpec((t, LANE), lambda j: (0, j))
    wsp = pl.BlockSpec((3, LANE), lambda j: (0, j))
    act = jax.ShapeDtypeStruct((t, cc), BF16)
    return pl.pallas_call(
        body, name=name, grid=(nb,),
        out_shape=(act, act, act, jax.ShapeDtypeStruct((3, cc), F32)),
        in_specs=[one, col(0), col(1), col(2), wsp],
        out_specs=(one, one, one, wsp),
        compiler_params=_params("parallel"))(dy, z_all, z_all, z_all, conv_w)


def _sigmoid(v):
    return 1.0 / (1.0 + jnp.exp(-v))


def _merge_fwd(z_all, b_gate, y_a, y_b, gate_off, name):
    t, d = y_a.shape
    tm = _pick(t, (128,))
    gb = gate_off // d

    def body(za_ref, zb_ref, ba_ref, bb_ref, ya_ref, yb_ref, m_ref):
        ga = _sigmoid(za_ref[...].astype(F32) + ba_ref[...])
        gbv = _sigmoid(zb_ref[...].astype(F32) + bb_ref[...])
        m_ref[...] = (ga * ya_ref[...].astype(F32) + gbv * yb_ref[...].astype(F32)).astype(BF16)

    return pl.pallas_call(
        body, name=name, grid=(t // tm,), out_shape=jax.ShapeDtypeStruct((t, d), BF16),
        in_specs=[_rb(tm, d, gb), _rb(tm, d, gb + 1), _vec(d, 0), _vec(d, 1), _rb(tm, d), _rb(tm, d)],
        out_specs=_rb(tm, d), compiler_params=_params("parallel"))(z_all, z_all, b_gate, b_gate, y_a, y_b)


def _merge_bwd(dm, z_all, b_gate, y_a, y_b, gate_off, name):
    t, d = y_a.shape
    tm = _pick(t, (128,))
    gb = gate_off // d

    def body(dm_ref, za_ref, zb_ref, ba_ref, bb_ref, ya_ref, yb_ref, dya_ref, dyb_ref, dzg_ref, db_ref):
        dmv = dm_ref[...].astype(F32)
        ga = _sigmoid(za_ref[...].astype(F32) + ba_ref[...])
        gbv = _sigmoid(zb_ref[...].astype(F32) + bb_ref[...])
        dya_ref[...] = (dmv * ga).astype(BF16)
        dyb_ref[...] = (dmv * gbv).astype(BF16)
        dza = dmv * ya_ref[...].astype(F32) * (ga * (1.0 - ga))
        dzb = dmv * yb_ref[...].astype(F32) * (gbv * (1.0 - gbv))
        dzg_ref[:, 0:d] = dza.astype(BF16)
        dzg_ref[:, d:2 * d] = dzb.astype(BF16)

        @pl.when(pl.program_id(0) == 0)
        def _():
            db_ref[...] = jnp.zeros_like(db_ref)

        db_ref[:, 0:d] += jnp.sum(dza, axis=0, keepdims=True)
        db_ref[:, d:2 * d] += jnp.sum(dzb, axis=0, keepdims=True)

    act = jax.ShapeDtypeStruct((t, d), BF16)
    return pl.pallas_call(
        body, name=name, grid=(t // tm,),
        out_shape=(act, act, jax.ShapeDtypeStruct((t, 2 * d), BF16), jax.ShapeDtypeStruct((1, 2 * d), F32)),
        in_specs=[_rb(tm, d), _rb(tm, d, gb), _rb(tm, d, gb + 1), _vec(d, 0), _vec(d, 1), _rb(tm, d), _rb(tm, d)],
        out_specs=(_rb(tm, d), _rb(tm, d), _rb(tm, 2 * d), _vec(2 * d)),
        compiler_params=_params("arbitrary"))(dm, z_all, z_all, b_gate, b_gate, y_a, y_b)


def _swiglu_fwd(gate, up, name):
    t, f = gate.shape
    tm = _pick(t, (128,))

    def body(g_ref, u_ref, a_ref):
        g = g_ref[...].astype(F32)
        a_ref[...] = (g * _sigmoid(g) * u_ref[...].astype(F32)).astype(BF16)

    return pl.pallas_call(
        body, name=name, grid=(t // tm,), out_shape=jax.ShapeDtypeStruct((t, f), BF16),
        in_specs=[_rb(tm, f), _rb(tm, f)], out_specs=_rb(tm, f), compiler_params=_params("parallel"))(gate, up)


def _swiglu_bwd(dact, gate, up, name):
    t, f = gate.shape
    tm = _pick(t, (128,))

    def body(da_ref, g_ref, u_ref, dg_ref, du_ref):
        g, da = g_ref[...].astype(F32), da_ref[...].astype(F32)
        sg = _sigmoid(g)
        dg_ref[...] = (da * u_ref[...].astype(F32) * (sg * (1.0 + g * (1.0 - sg)))).astype(BF16)
        du_ref[...] = (da * (g * sg)).astype(BF16)

    act = jax.ShapeDtypeStruct((t, f), BF16)
    return pl.pallas_call(
        body, name=name, grid=(t // tm,), out_shape=(act, act),
        in_specs=[_rb(tm, f)] * 3, out_specs=(_rb(tm, f), _rb(tm, f)), compiler_params=_params("parallel"))(dact, gate, up)


def _loss_head(x2, target, g, name):
    t, d = x2.shape
    tm = _pick(t, (128,))

    def body(x_ref, t_ref, g_ref, loss_ref, dx_ref, dxb_ref, dg_ref):
        xv, gv = x_ref[...], g_ref[...]
        err = xv * _rms_inv(xv) * gv - t_ref[...]
        dx, dgrow = _rms_bwd_rows(err * (1.0 / d), xv, gv)
        dx_ref[...] = dx
        dxb_ref[...] = dx.astype(BF16)

        @pl.when(pl.program_id(0) == 0)
        def _():
            loss_ref[...] = jnp.zeros_like(loss_ref)
            dg_ref[...] = jnp.zeros_like(dg_ref)

        loss_ref[...] += (0.5 / d) * jnp.sum(jnp.sum(err * err, axis=1, keepdims=True), axis=0, keepdims=True)
        dg_ref[...] += jnp.sum(dgrow, axis=0, keepdims=True)

    return pl.pallas_call(
        body, name=name, grid=(t // tm,),
        out_shape=(jax.ShapeDtypeStruct((1, 1), F32), jax.ShapeDtypeStruct((t, d), F32),
                   jax.ShapeDtypeStruct((t, d), BF16), jax.ShapeDtypeStruct((1, d), F32)),
        in_specs=[_rb(tm, d), _rb(tm, d), _vec(d)],
        out_specs=(pl.BlockSpec((1, 1), lambda i: (0, 0)), _rb(tm, d), _rb(tm, d), _vec(d)),
        compiler_params=_params("arbitrary"))(x2, target, g)


def _adamw(parts, w, m, v, name):
    r, c = w.shape
    tr = r if r * c <= ADAM_BLOCK_ELEMS else _pick(r, tuple(s for s in (512, 256, 128, 64, 32, 16, 8) if s * c <= ADAM_BLOCK_ELEMS))

    def body(p_ref, w_ref, m_ref, v_ref, g_ref, d_ref, nm_ref, nv_ref):
        g = p_ref[0].astype(F32)
        for s in range(1, N_DEV):
            g = g + p_ref[s].astype(F32)
        nm = ADAM_B1 * m_ref[...] + (1.0 - ADAM_B1) * g
        nv = ADAM_B2 * v_ref[...] + (1.0 - ADAM_B2) * (g * g)
        m_hat = nm / (1.0 - ADAM_B1 ** ADAM_STEP)
        v_hat = nv / (1.0 - ADAM_B2 ** ADAM_STEP)
        g_ref[...] = g
        d_ref[...] = -ADAM_LR * (m_hat / (jnp.sqrt(v_hat) + ADAM_EPS) + ADAM_WD * w_ref[...])
        nm_ref[...] = nm
        nv_ref[...] = nv

    blk = pl.BlockSpec((tr, c), lambda i: (i, 0))
    out = jax.ShapeDtypeStruct((r, c), F32)
    return pl.pallas_call(
        body, name=name, grid=(r // tr,), out_shape=(out, out, out, out),
        in_specs=[pl.BlockSpec((N_DEV, tr, c), lambda i: (0, i, 0)), blk, blk, blk],
        out_specs=(blk, blk, blk, blk), compiler_params=_params("parallel"))(parts, w, m, v)


def _cols_of(g):
    return jnp.transpose(g, (1, 0, 2)).reshape(g.shape[1], N_DEV * g.shape[2])


def _col_parts(dw):
    k, n8 = dw.shape
    return jnp.transpose(dw.reshape(k, N_DEV, n8 // N_DEV), (1, 0, 2))


def kernel(x, positions, g_mix, w_in, b_gate, conv_w, g_q_a, w_q_b, g_kv_a, w_kv_b, w_branch, w_out, g_ffn, w_ffn_gate, w_ffn_up, w_ffn_down, g_final, loss_target, m_g_mix, m_w_in, m_b_gate, m_conv_w, m_g_q_a, m_w_q_b, m_g_kv_a, m_w_kv_b, m_w_branch, m_w_out, m_g_ffn, m_w_ffn_gate, m_w_ffn_up, m_w_ffn_down, m_g_final, v_g_mix, v_w_in, v_b_gate, v_conv_w, v_g_q_a, v_w_q_b, v_g_kv_a, v_w_kv_b, v_w_branch, v_w_out, v_g_ffn, v_w_ffn_gate, v_w_ffn_up, v_w_ffn_down, v_g_final):
    given = dict(locals())
    xs = x[0]
    t, d = xs.shape
    cc = d // 2
    n_heads = cc // V_HEAD
    in_cols = N_DEV * w_in.shape[2]
    q_off, kv_off, kr_off = 3 * cc, 3 * cc + Q_LORA, 3 * cc + Q_LORA + KV_LORA
    head_cols = kr_off + QK_ROPE
    head_pad = -(-(kr_off + LANE) // 1024) * 1024
    assert in_cols == head_cols + 2 * d and q_off % Q_LORA == 0 and kv_off % KV_LORA == 0 and kr_off % LANE == 0
    fs = w_ffn_gate.shape[2]
    fsp = -(-fs // LANE) * LANE
    ffp = N_DEV * fsp
    me = _me_index()
    bf = lambda a: a.astype(BF16)
    one_slot = lambda refs, i: refs[i].at[0]

    g_in = _gather_start([bf(w_in[0]), conv_w[0]], me, "ag1_start_in")
    h = _rms_fwd(xs, g_mix, "rms_mix", after=g_in[3])
    f_in = _forward_start(_split_wait(g_in, one_slot, h, "ag1_wait_in"), "ag2_start_in")
    g_mix_w = _gather_start([bf(w_q_b[0]), bf(w_kv_b[0]), bf(w_branch[0].reshape(2 * cc, -1)), bf(w_out[0])],
                            me, "ag1_start_mix", after=f_in[3])
    w_in_g, cw_g = _split_wait(f_in, one_slot, g_mix_w[3], "ag2_wait_in")
    w_in_full = _cols_of(w_in_g)
    w_head = jnp.concatenate([w_in_full[:, :head_cols], jnp.zeros((d, head_pad - head_cols), BF16)], axis=1)
    w_gate = w_in_full[:, head_cols:]
    cw = _cols_of(cw_g)

    inv_freq = ROPE_THETA ** (-jnp.arange(0, QK_ROPE, 2, dtype=F32) / QK_ROPE)
    ang = positions[0].astype(F32)[:, None] * inv_freq[None, :]
    cos, sin = jnp.cos(ang), jnp.sin(ang)
    z32, z64 = jnp.zeros((t, 32), F32), jnp.zeros((t, 64), F32)
    tabs = (jnp.concatenate([cos, cos, jnp.ones((t, 64), F32)], axis=1),
            jnp.concatenate([-sin, z32, z64], axis=1),
            jnp.concatenate([z32, sin, z64], axis=1))

    z_head = _matmul(h, w_head, name="mm_z_head")
    zg = _matmul(h, w_gate, name="mm_z_gate")
    f_mix = _forward_start(_split_wait(g_mix_w, one_slot, zg, "ag1_wait_mix"), "ag2_start_mix")
    g_ffn_w = _gather_start([bf(jnp.pad(w_ffn_gate[0], ((0, 0), (0, fsp - fs)))), bf(jnp.pad(w_ffn_up[0], ((0, 0), (0, fsp - fs)))),
                             bf(jnp.pad(w_ffn_down[0], ((0, fsp - fs), (0, 0))))], me, "ag1_start_ffn", after=f_mix[3])
    y_a = _conv_fwd(z_head, cw, cc, "conv_fwd")
    qn, kvn = _latent_norm(z_head, g_q_a, g_kv_a, q_off, kv_off, "latent_norm")
    wq_g, wkv_g, wbr_g, wo_g = _split_wait(f_mix, one_slot, g_ffn_w[3], "ag2_wait_mix")
    wq = _cols_of(wq_g).reshape(Q_LORA, n_heads, QK_NOPE + QK_ROPE)
    wq_pad = jnp.pad(wq, ((0, 0), (0, 0), (0, HEAD_PAD - QK_NOPE - QK_ROPE))).reshape(Q_LORA, n_heads * HEAD_PAD)
    wkv = _cols_of(wkv_g).reshape(KV_LORA, n_heads, 2, QK_NOPE)
    wkv_perm = jnp.transpose(wkv, (0, 2, 1, 3)).reshape(KV_LORA, 2 * n_heads * QK_NOPE)
    wbr = _cols_of(wbr_g)
    wb_a, wb_b = wbr[:cc], wbr[cc:]
    wo = wo_g.reshape(d, d)
    q_pad = _rope_q(_matmul(qn, wq_pad, name="mm_q", out_dtype=F32), tabs, n_heads, 1.0, BF16, "rope_q")
    kv = _matmul(kvn, wkv_perm, name="mm_kv")
    k_pad = _rope_k(kv, z_head, tabs, n_heads, kr_off, "rope_k")
    y_b = _attn_fwd(q_pad, k_pad, kv, n_heads, "attn_fwd")
    f_ffn = _forward_start(_split_wait(g_ffn_w, one_slot, y_b, "ag1_wait_ffn"), "ag2_start_ffn")
    ybr_a = _matmul(y_a, wb_a, name="mm_br_a", after=f_ffn[3])
    ybr_b = _matmul(y_b, wb_b, name="mm_br_b")
    merged = _merge_fwd(zg, b_gate, ybr_a, ybr_b, 0, "merge_fwd")
    x1 = _matmul(merged, wo, name="mm_out", res=xs, out_dtype=F32)
    h2 = _rms_fwd(x1, g_ffn, "rms_ffn")
    wg_g, wu_g, wd_g = _split_wait(f_ffn, one_slot, h2, "ag2_wait_ffn")
    wg, wu, wd = _cols_of(wg_g), _cols_of(wu_g), wd_g.reshape(ffp, d)
    gate = _matmul(h2, wg, name="mm_gate")
    up = _matmul(h2, wu, name="mm_up")
    act = _swiglu_fwd(gate, up, "swiglu_fwd")
    x2 = _matmul(act, wd, name="mm_down", res=x1, out_dtype=F32)
    loss_part, dx2, dx2b, dg_final = _loss_head(x2, loss_target[0], g_final.reshape(1, d), "loss_head")

    dact = _matmul(dx2b, wd, tb=True, name="mm_d_act")
    dwd = _matmul(act, dx2b, ta=True, name="mm_dw_down")
    r_down = _exchange_start([dwd.reshape(N_DEV, fsp, d)[:, :fs]], me, "rs_start_down")
    dgate, dup = _swiglu_bwd(dact, gate, up, "swiglu_bwd")
    dwg = _matmul(h2, dgate, ta=True, name="mm_dw_gate", after=r_down[3])
    dwu = _matmul(h2, dup, ta=True, name="mm_dw_up")
    ffn_parts = lambda dw: jnp.transpose(dw.reshape(d, N_DEV, fsp)[:, :, :fs], (1, 0, 2))
    r_gate_up = _exchange_start([ffn_parts(dwg), ffn_parts(dwu)], me, "rs_start_gate_up")
    dh2 = _matmul(dgate, wg, tb=True, name="mm_d_h2_gate", out_dtype=F32, after=r_gate_up[3])
    dh2 = _matmul(dup, wu, tb=True, name="mm_d_h2_up", res=dh2, out_dtype=F32)
    dx1, dx1b, dg_ffn = _rms_bwd(dh2, x1, g_ffn, dx2, "rms_ffn_bwd")
    dwo = _matmul(merged, dx1b, ta=True, name="mm_dw_out")
    r_out = _exchange_start([dwo.reshape(N_DEV, d // N_DEV, d)], me, "rs_start_out")
    dmerged = _matmul(dx1b, wo, tb=True, name="mm_d_merged", after=r_out[3])
    dybr_a, dybr_b, dzg, db_gate = _merge_bwd(dmerged, zg, b_gate, ybr_a, ybr_b, 0, "merge_bwd")
    dwb_a = _matmul(y_a, dybr_a, ta=True, name="mm_dw_br_a")
    dwb_b = _matmul(y_b, dybr_b, ta=True, name="mm_dw_br_b")
    r_br = _exchange_start([_col_parts(jnp.concatenate([dwb_a, dwb_b], axis=0))], me, "rs_start_branch")
    dy_a = _matmul(dybr_a, wb_a, tb=True, name="mm_d_y_a", after=r_br[3])
    dy_b = _matmul(dybr_b, wb_b, tb=True, name="mm_d_y_b")
    dw_gate = _matmul(h, dzg, ta=True, name="mm_dw_in_gate")
    dh_gate = _matmul(dzg, w_gate, tb=True, name="mm_d_h_gate", out_dtype=F32)
    dq_pad, dk_pad, dv = _attn_bwd(q_pad, k_pad, kv, dy_b, n_heads, "attn_bwd")
    dq_raw = _rope_q(dq_pad, tabs, n_heads, -1.0, BF16, "rope_q_bwd")
    dkv, dkr = _rope_k_bwd(dk_pad, dv, tabs, n_heads, "rope_k_bwd")
    dwq = _matmul(qn, dq_raw, ta=True, name="mm_dw_q")
    dwkv = _matmul(kvn, dkv, ta=True, name="mm_dw_kv")
    dwq_full = dwq.reshape(Q_LORA, n_heads, HEAD_PAD)[:, :, :QK_NOPE + QK_ROPE].reshape(Q_LORA, -1)
    dwkv_full = jnp.transpose(dwkv.reshape(KV_LORA, 2, n_heads, QK_NOPE), (0, 2, 1, 3)).reshape(KV_LORA, -1)
    r_qkv = _exchange_start([_col_parts(dwq_full), _col_parts(dwkv_full)], me, "rs_start_q_kv")
    dqn = _matmul(dq_raw, wq_pad, tb=True, name="mm_d_qn", after=r_qkv[3])
    dkvn = _matmul(dkv, wkv_perm, tb=True, name="mm_d_kvn")
    dqa, dkva, dg_q, dg_kv = _latent_norm_bwd(dqn, dkvn, z_head, g_q_a, g_kv_a, q_off, kv_off, "latent_norm_bwd")
    dcb, dcc, dch, dcw = _conv_bwd(dy_a, z_head, cw, cc, "conv_bwd")
    dz_head = jnp.concatenate([dcb, dcc, dch, dqa, dkva, dkr, jnp.zeros((t, head_pad - kr_off - LANE), BF16)], axis=1)
    dw_head = _matmul(h, dz_head, ta=True, name="mm_dw_in_head")
    r_in = _exchange_start([_col_parts(jnp.concatenate([dw_head[:, :head_cols], dw_gate], axis=1))], me, "rs_start_in")
    dh = _matmul(dz_head, w_head, tb=True, name="mm_d_h_head", res=dh_gate, out_dtype=F32, after=r_in[3])
    dx, _, dg_mix = _rms_bwd(dh, xs, g_mix, dx1, "rms_mix_bwd")

    results = {}
    last = dx

    def update(started, wnames, shard_shapes):
        nonlocal last
        n = len(wnames)
        recvs = _split_wait(started, one_slot, last, "rs_wait_" + wnames[0])[n:]
        for recv, wname, shard_shape in zip(recvs, wnames, shard_shapes):
            shp = given[wname].shape
            outs = _adamw(recv, given[wname].reshape(shard_shape), given["m_" + wname].reshape(shard_shape),
                          given["v_" + wname].reshape(shard_shape), "adamw_" + wname)
            results[wname] = tuple(o.reshape(shp) for o in outs)
            last = outs[0]

    update(r_down, ["w_ffn_down"], [w_ffn_down.shape[1:]])
    update(r_gate_up, ["w_ffn_gate", "w_ffn_up"], [w_ffn_gate.shape[1:], w_ffn_up.shape[1:]])
    update(r_out, ["w_out"], [w_out.shape[1:]])
    update(r_br, ["w_branch"], [(2 * cc, d // N_DEV)])
    update(r_qkv, ["w_q_b", "w_kv_b"], [w_q_b.shape[1:], w_kv_b.shape[1:]])

    small = [("g_mix", dg_mix), ("b_gate", db_gate), ("g_q_a", dg_q), ("g_kv_a", dg_kv), ("g_ffn", dg_ffn),
             ("g_final", dg_final), ("conv_w", dcw.reshape(1, 3 * cc))]
    packed = _all_gather(jnp.concatenate([p for _, p in small], axis=1), "ag_small_grads", last)
    off = 0
    for wname, p in small:
        n = p.shape[1]
        parts = packed[:, :, off:off + n]
        off += n
        shp = given[wname].shape
        if wname == "conv_w":
            width = conv_w.shape[2]
            me = 4 * lax.axis_index("x") + 2 * lax.axis_index("y") + lax.axis_index("c")
            parts = lax.dynamic_slice_in_dim(parts.reshape(N_DEV, 3, cc), me * width, width, axis=2)
            flat = (3, width)
        else:
            flat = (1, n)
        outs = _adamw(parts, given[wname].reshape(flat), given["m_" + wname].reshape(flat),
                      given["v_" + wname].reshape(flat), "adamw_" + wname)
        results[wname] = tuple(o.reshape(shp) for o in outs)
        last = outs[0]
    update(r_in, ["w_in"], [w_in.shape[1:]])

    loss = lax.psum(loss_part[0, 0], MESH_AXES)
    order = ["g_mix", "w_in", "b_gate", "conv_w", "g_q_a", "w_q_b", "g_kv_a", "w_kv_b", "w_branch", "w_out", "g_ffn",
             "w_ffn_gate", "w_ffn_up", "w_ffn_down", "g_final"]
    out = [loss, dx[None]]
    for k in range(4):
        out += [results[n][k] for n in order]
    return tuple(out)
```

```python
import functools
import math

import jax
import jax.numpy as jnp
from jax import lax
from jax.experimental import pallas as pl
from jax.experimental.pallas import tpu as pltpu

F32 = jnp.float32
BF16 = jnp.bfloat16
N_DEV = 8
MESH_AXES = ("x", "y", "c")
MESH = pl.DeviceIdType.MESH

QK_NOPE = 128
QK_ROPE = 64
V_HEAD = 128
HEAD_PAD = 256
Q_LORA = 1024
KV_LORA = 512
ROPE_THETA = 10000.0
RMS_EPS = 1e-6
SOFTMAX_SCALE = 1.0 / math.sqrt(QK_NOPE + QK_ROPE)
ADAM_LR, ADAM_B1, ADAM_B2, ADAM_EPS, ADAM_WD, ADAM_STEP = 0.001, 0.9, 0.999, 1e-08, 0.01, 10

VMEM_LIMIT = 60 * 1024 * 1024
LANE = 128
ADAM_BLOCK_ELEMS = 1 << 18
MATMUL_K_TILES = (4096, 2816, 2048, 1024, 512, 256, 128)
MATMUL_VMEM_BUDGET = 48 * 1024 * 1024


def _pick(n, cands=(1024, 512, 256, 128)):
    for c in cands:
        if n % c == 0:
            return c
    return n


def _params(*sem):
    return pltpu.CompilerParams(dimension_semantics=sem, vmem_limit_bytes=VMEM_LIMIT)


def _rb(tm, c, cb=0):
    return pl.BlockSpec((tm, c), lambda i: (i, cb))


def _vec(c, cb=0):
    return pl.BlockSpec((1, c), lambda i: (0, cb))


def _all_gather(x, name, after):
    def body(x_ref, after_ref, out_ref, send_sems, recv_sems, local_sem):
        x_, y_, c_ = lax.axis_index("x"), lax.axis_index("y"), lax.axis_index("c")
        me, sibling = (x_, y_, c_), (x_, y_, 1 - c_)
        chips = [(1 - x_, y_), (x_, 1 - y_), (1 - x_, 1 - y_)]

        def slot(px, py, pc):
            return out_ref.at[4 * px + 2 * py + pc]

        def copy(k, block, to, src=None):
            return pltpu.make_async_remote_copy(
                src_ref=slot(*block) if src is None else src, dst_ref=slot(*block),
                send_sem=send_sems.at[k], recv_sem=recv_sems.at[k], device_id=to, device_id_type=MESH)

        mine = pltpu.make_async_copy(x_ref, slot(*me), local_sem)
        mine.start()
        first = [copy(0, me, sibling, src=x_ref)]
        first += [copy(1 + j, me, (*chip, c_), src=x_ref) for j, chip in enumerate(chips)]
        for cp in first:
            cp.start()
        passed = [copy(4 + j, (*chip, c_), sibling) for j, chip in enumerate(chips)]
        for j, chip in enumerate(chips):
            copy(1 + j, (*chip, c_), me).wait_recv()
            passed[j].start()
        copy(0, sibling, me).wait_recv()
        for j, chip in enumerate(chips):
            copy(4 + j, (*chip, 1 - c_), me).wait_recv()
        for cp in first + passed:
            cp.wait_send()
        mine.wait()

    return pl.pallas_call(
        body, name=name,
        out_shape=jax.ShapeDtypeStruct((N_DEV,) + x.shape, x.dtype),
        in_specs=[pl.BlockSpec(memory_space=pl.ANY), pl.BlockSpec(memory_space=pl.ANY)],
        out_specs=pl.BlockSpec(memory_space=pl.ANY),
        scratch_shapes=[pltpu.SemaphoreType.DMA((7,)), pltpu.SemaphoreType.DMA((7,)), pltpu.SemaphoreType.DMA(())],
    )(x, after)


_HBM = pl.BlockSpec(memory_space=pltpu.HBM)
_SEM = pl.BlockSpec(memory_space=pltpu.SEMAPHORE)
_ANY = pl.BlockSpec(memory_space=pl.ANY)
_EFFECT = pltpu.SideEffectType.DATAFLOW_SIDE_EFFECTING


def _me_index():
    return 4 * lax.axis_index("x") + 2 * lax.axis_index("y") + lax.axis_index("c")


def _window(ref, idx, cols):
    if cols is None:
        return ref.at[idx]
    return ref.at[:, pl.ds(idx * cols if isinstance(idx, int) else pl.multiple_of(idx * cols, LANE), cols)]


def _own_block_only(src, name, cols=None, src_cols=False):
    block_shape = src.shape if src_cols is False else ((src.shape[0], src_cols) if src_cols else src.shape[1:])
    assert cols is None or (block_shape[1] == cols and cols % LANE == 0)
    out_shape = (N_DEV,) + tuple(block_shape) if cols is None else (block_shape[0], N_DEV * cols)

    def body(src_ref, dst_ref, sem):
        me_idx = _me_index()
        mine = src_ref if src_cols is False else _window(src_ref, me_idx, src_cols)
        cp = pltpu.make_async_copy(mine, _window(dst_ref, me_idx, cols), sem)
        cp.start()
        cp.wait()

    return pl.pallas_call(
        body, name=name, out_shape=jax.ShapeDtypeStruct(out_shape, src.dtype),
        in_specs=[_ANY], out_specs=_ANY, scratch_shapes=[pltpu.SemaphoreType.DMA(())])(src)


def _split_start(bufs, sem_shape, issue, name, after=None):
    n = len(bufs)
    rows, per_row = sem_shape
    sem = lambda sems, i, k: sems.at[i * per_row + k]
    first_out = n + (after is not None)

    def body(*refs):
        issue(refs[:n], refs[first_out], refs[first_out + 1], sem)
        refs[-1][...] = jnp.zeros_like(refs[-1])

    sems = pltpu.SemaphoreType.DMA((rows * per_row,))
    operands = [pltpu.with_memory_space_constraint(b, pltpu.HBM) for b in bufs] + ([] if after is None else [after])
    outs = pl.pallas_call(
        body, name=name,
        out_shape=(sems, sems) + tuple(pltpu.HBM(b.shape, b.dtype) for b in bufs) + (jax.ShapeDtypeStruct((8, LANE), F32),),
        in_specs=(_HBM,) * n + (_ANY,) * (after is not None),
        out_specs=(_SEM, _SEM) + (_HBM,) * n + (pl.BlockSpec(memory_space=pltpu.VMEM),),
        input_output_aliases={i: 2 + i for i in range(n)},
        compiler_params=pltpu.CompilerParams(has_side_effects=_EFFECT),
    )(*operands)
    return outs[0], outs[1], list(outs[2:2 + n]), outs[-1], sem_shape


def _split_wait(started, block_of, after, name):
    send_sems, recv_sems, bufs, _, sem_shape = started
    n = len(bufs)

    def body(*refs):
        x_, y_, c_ = lax.axis_index("x"), lax.axis_index("y"), lax.axis_index("c")
        for i in range(sem_shape[0]):
            blk = block_of(refs, i)
            for k in range(sem_shape[1]):
                cp = pltpu.make_async_remote_copy(
                    src_ref=blk, dst_ref=blk, send_sem=refs[n].at[i * sem_shape[1] + k], recv_sem=refs[n + 1].at[i * sem_shape[1] + k],
                    device_id=(x_, y_, c_), device_id_type=MESH)
                cp.wait_send()
                cp.wait_recv()

    outs = pl.pallas_call(
        body, name=name,
        out_shape=tuple(pltpu.HBM(b.shape, b.dtype) for b in bufs),
        in_specs=(_HBM,) * n + (_SEM, _SEM, _ANY), out_specs=(_HBM,) * n,
        input_output_aliases={i: i for i in range(n)},
        compiler_params=pltpu.CompilerParams(has_side_effects=_EFFECT),
    )(*bufs, send_sems, recv_sems, after)
    return list(outs)


def _first_window(cols):
    return lambda refs, i: _window(refs[i], 0, cols[i])


def _gather_start(blocks, me, name, after=None, cols=None):
    cols = cols or [None] * len(blocks)

    def issue(buf_refs, send_sems, recv_sems, sem):
        x_, y_, c_ = lax.axis_index("x"), lax.axis_index("y"), lax.axis_index("c")
        me_idx = 4 * x_ + 2 * y_ + c_
        targets = [(x_, y_, 1 - c_), (1 - x_, y_, c_), (x_, 1 - y_, c_), (1 - x_, 1 - y_, c_)]
        for i, buf in enumerate(buf_refs):
            mine = _window(buf, me_idx, cols[i])
            for k, to in enumerate(targets):
                pltpu.make_async_remote_copy(
                    src_ref=mine, dst_ref=mine, send_sem=sem(send_sems, i, k), recv_sem=sem(recv_sems, i, k),
                    device_id=to, device_id_type=MESH).start()

    bufs = [_own_block_only(b, "%s_own%d" % (name, i), c) for i, (b, c) in enumerate(zip(blocks, cols))]
    return _split_start(bufs, (len(blocks), 4), issue, name, after)


def _forward_start(bufs, name, after=None, cols=None):
    cols = cols or [None] * len(bufs)

    def issue(buf_refs, send_sems, recv_sems, sem):
        x_, y_, c_ = lax.axis_index("x"), lax.axis_index("y"), lax.axis_index("c")
        chips = [(1 - x_, y_), (x_, 1 - y_), (1 - x_, 1 - y_)]
        for i, buf in enumerate(buf_refs):
            for k, (px, py) in enumerate(chips):
                landed = _window(buf, 4 * px + 2 * py + c_, cols[i])
                pltpu.make_async_remote_copy(
                    src_ref=landed, dst_ref=landed, send_sem=sem(send_sems, i, k), recv_sem=sem(recv_sems, i, k),
                    device_id=(x_, y_, 1 - c_), device_id_type=MESH).start()

    return _split_start(bufs, (len(bufs), 3), issue, name, after)


def _exchange_start(parts, me, name, cols=None):
    n = len(parts)
    cols = cols or [None] * n

    def issue(refs, send_sems, recv_sems, sem):
        x_, y_, c_ = lax.axis_index("x"), lax.axis_index("y"), lax.axis_index("c")
        me_idx = 4 * x_ + 2 * y_ + c_
        for i in range(n):
            for d in range(1, N_DEV):
                px = 1 - x_ if d & 4 else x_
                py = 1 - y_ if d & 2 else y_
                pc = 1 - c_ if d & 1 else c_
                pltpu.make_async_remote_copy(
                    src_ref=_window(refs[i], 4 * px + 2 * py + pc, cols[i]), dst_ref=refs[n + i].at[me_idx],
                    send_sem=sem(send_sems, i, d - 1), recv_sem=sem(recv_sems, i, d - 1),
                    device_id=(px, py, pc), device_id_type=MESH).start()

    lands = [_own_block_only(p, "%s_own%d" % (name, i), src_cols=c) for i, (p, c) in enumerate(zip(parts, cols))]
    return _split_start(list(parts) + lands, (n, N_DEV - 1), issue, name)


def _land_block(refs_offset):
    return lambda refs, i: refs[refs_offset + i].at[0]


def _matmul(a, b, *, name, ta=False, tb=False, res=None, out_dtype=BF16, after=None):
    (kdim, m) = a.shape if ta else a.shape[::-1]
    (n, kdim_b) = b.shape if tb else b.shape[::-1]
    assert kdim == kdim_b, (a.shape, b.shape, ta, tb)
    tm, tn = _pick(m), _pick(n)
    out_bytes = jnp.dtype(out_dtype).itemsize

    def vmem_bytes(tk):
        return 2 * (2 * tk * (tm + tn) + tm * tn * (out_bytes + (4 if res is not None else 0))) + (4 * tm * tn if tk < kdim else 0)

    fitting = [c for c in MATMUL_K_TILES if kdim % c == 0 and vmem_bytes(c) <= MATMUL_VMEM_BUDGET]
    tk = fitting[0] if fitting else kdim
    nk = kdim // tk
    a_spec = pl.BlockSpec((tk, tm), lambda i, j, k: (k, i)) if ta else pl.BlockSpec((tm, tk), lambda i, j, k: (i, k))
    b_spec = pl.BlockSpec((tn, tk), lambda i, j, k: (j, k)) if tb else pl.BlockSpec((tk, tn), lambda i, j, k: (k, j))
    o_spec = pl.BlockSpec((tm, tn), lambda i, j, k: (i, j))
    dims = (((0 if ta else 1,), (1 if tb else 0,)), ((), ()))

    def body(*refs):
        a_ref, b_ref = refs[:2]
        r_ref = None if res is None else refs[2]
        def finish(v):
            if r_ref is not None:
                v = r_ref[...] + v
            o_ref[...] = v.astype(out_dtype)

        part = lax.dot_general(a_ref[...], b_ref[...], dims, preferred_element_type=F32)
        if nk == 1:
            o_ref = refs[-1]
            finish(part)
            return
        o_ref, acc = refs[-2:]
        k = pl.program_id(2)

        @pl.when(k == 0)
        def _():
            acc[...] = part

        @pl.when(k > 0)
        def _():
            acc[...] += part

        @pl.when(k == nk - 1)
        def _():
            finish(acc[...])

    operands = [a, b] + ([] if res is None else [res]) + ([] if after is None else [after])
    in_specs = [a_spec, b_spec] + ([] if res is None else [o_spec]) + ([] if after is None else [_ANY])
    return pl.pallas_call(
        body, name=name, grid=(m // tm, n // tn, nk),
        out_shape=jax.ShapeDtypeStruct((m, n), out_dtype),
        in_specs=in_specs, out_specs=o_spec,
        scratch_shapes=[pltpu.VMEM((tm, tn), F32)] if nk > 1 else [],
        compiler_params=_params("parallel", "parallel", "arbitrary"),
    )(*operands)


def _rms_inv(x):
    return lax.rsqrt(jnp.mean(x * x, axis=-1, keepdims=True) + RMS_EPS)


def _rms_fwd(x, g, name, after=None):
    t, d = x.shape
    tm = _pick(t, (256, 128))

    def body(x_ref, g_ref, *rest):
        xv = x_ref[...]
        rest[-1][...] = (xv * _rms_inv(xv) * g_ref[...]).astype(BF16)

    return pl.pallas_call(
        body, name=name, grid=(t // tm,), out_shape=jax.ShapeDtypeStruct((t, d), BF16),
        in_specs=[_rb(tm, d), _vec(d)] + ([] if after is None else [_ANY]), out_specs=_rb(tm, d),
        compiler_params=_params("parallel"))(x, g, *([] if after is None else [after]))


def _rms_bwd_rows(dy, xv, g):
    inv = _rms_inv(xv)
    xhat = xv * inv
    dxhat = dy * g
    dx = inv * (dxhat - xhat * jnp.mean(dxhat * xhat, axis=-1, keepdims=True))
    return dx, dy * xhat


def _rms_bwd(dy, x, g, res, name):
    t, d = x.shape
    tm = _pick(t, (128,))

    def body(dy_ref, x_ref, g_ref, r_ref, dx_ref, dxb_ref, dg_ref):
        dx, dgrow = _rms_bwd_rows(dy_ref[...].astype(F32), x_ref[...], g_ref[...])
        dx = r_ref[...] + dx
        dx_ref[...] = dx
        dxb_ref[...] = dx.astype(BF16)

        @pl.when(pl.program_id(0) == 0)
        def _():
            dg_ref[...] = jnp.zeros_like(dg_ref)

        dg_ref[...] += jnp.sum(dgrow, axis=0, keepdims=True)

    return pl.pallas_call(
        body, name=name, grid=(t // tm,),
        out_shape=(jax.ShapeDtypeStruct((t, d), F32), jax.ShapeDtypeStruct((t, d), BF16), jax.ShapeDtypeStruct((1, d), F32)),
        in_specs=[_rb(tm, d), _rb(tm, d), _vec(d), _rb(tm, d)],
        out_specs=(_rb(tm, d), _rb(tm, d), _vec(d)), compiler_params=_params("arbitrary"))(dy, x, g, res)


def _latent_norm(z_all, g_q, g_kv, q_off, kv_off, name):
    t = z_all.shape[0]
    tm = _pick(t, (256, 128))

    def body(qa_ref, kva_ref, gq_ref, gkv_ref, qn_ref, kvn_ref):
        qa = qa_ref[...].astype(F32)
        qn_ref[...] = (qa * _rms_inv(qa) * gq_ref[...]).astype(BF16)
        kva = kva_ref[...].astype(F32)
        kvn_ref[...] = (kva * _rms_inv(kva) * gkv_ref[...]).astype(BF16)

    return pl.pallas_call(
        body, name=name, grid=(t // tm,),
        out_shape=(jax.ShapeDtypeStruct((t, Q_LORA), BF16), jax.ShapeDtypeStruct((t, KV_LORA), BF16)),
        in_specs=[_rb(tm, Q_LORA, q_off // Q_LORA), _rb(tm, KV_LORA, kv_off // KV_LORA), _vec(Q_LORA), _vec(KV_LORA)],
        out_specs=(_rb(tm, Q_LORA), _rb(tm, KV_LORA)), compiler_params=_params("parallel"))(z_all, z_all, g_q, g_kv)


def _latent_norm_bwd(dqn, dkvn, z_all, g_q, g_kv, q_off, kv_off, name):
    t = z_all.shape[0]
    tm = _pick(t, (256, 128))

    def body(dqn_ref, dkvn_ref, qa_ref, kva_ref, gq_ref, gkv_ref, dqa_ref, dkva_ref, dgq_ref, dgkv_ref):
        dqa, dgq = _rms_bwd_rows(dqn_ref[...].astype(F32), qa_ref[...].astype(F32), gq_ref[...])
        dkva, dgkv = _rms_bwd_rows(dkvn_ref[...].astype(F32), kva_ref[...].astype(F32), gkv_ref[...])
        dqa_ref[...] = dqa.astype(BF16)
        dkva_ref[...] = dkva.astype(BF16)

        @pl.when(pl.program_id(0) == 0)
        def _():
            dgq_ref[...] = jnp.zeros_like(dgq_ref)
            dgkv_ref[...] = jnp.zeros_like(dgkv_ref)

        dgq_ref[...] += jnp.sum(dgq, axis=0, keepdims=True)
        dgkv_ref[...] += jnp.sum(dgkv, axis=0, keepdims=True)

    return pl.pallas_call(
        body, name=name, grid=(t // tm,),
        out_shape=(jax.ShapeDtypeStruct((t, Q_LORA), BF16), jax.ShapeDtypeStruct((t, KV_LORA), BF16),
                   jax.ShapeDtypeStruct((1, Q_LORA), F32), jax.ShapeDtypeStruct((1, KV_LORA), F32)),
        in_specs=[_rb(tm, Q_LORA), _rb(tm, KV_LORA), _rb(tm, Q_LORA, q_off // Q_LORA), _rb(tm, KV_LORA, kv_off // KV_LORA),
                  _vec(Q_LORA), _vec(KV_LORA)],
        out_specs=(_rb(tm, Q_LORA), _rb(tm, KV_LORA), _vec(Q_LORA), _vec(KV_LORA)),
        compiler_params=_params("arbitrary"))(dqn, dkvn, z_all, z_all, g_q, g_kv)


def _rot(xv, cos_k, sin_a, sin_b, sign):
    return xv * cos_k + sign * (pltpu.roll(xv, LANE - 32, 1) * sin_a + pltpu.roll(xv, 32, 1) * sin_b)


def _rope_q(q_raw, tabs, n_heads, sign, out_dtype, name):
    t, w = q_raw.shape
    tm = _pick(t, (256, 128))

    def body(q_ref, cos_ref, sa_ref, sb_ref, o_ref):
        cos_k, sin_a, sin_b = cos_ref[...], sa_ref[...], sb_ref[...]
        for h in range(n_heads):
            lo = h * HEAD_PAD
            o_ref[:, lo:lo + LANE] = q_ref[:, lo:lo + LANE].astype(out_dtype)
            o_ref[:, lo + LANE:lo + HEAD_PAD] = _rot(
                q_ref[:, lo + LANE:lo + HEAD_PAD].astype(F32), cos_k, sin_a, sin_b, sign).astype(out_dtype)

    return pl.pallas_call(
        body, name=name, grid=(t // tm,), out_shape=jax.ShapeDtypeStruct((t, w), out_dtype),
        in_specs=[_rb(tm, w), _rb(tm, LANE), _rb(tm, LANE), _rb(tm, LANE)],
        out_specs=_rb(tm, w), compiler_params=_params("parallel"))(q_raw, *tabs)


def _rope_k(kv, z_all, tabs, n_heads, kr_off, name):
    t = kv.shape[0]
    tm = _pick(t, (256, 128))
    wk = n_heads * QK_NOPE

    def body(kn_ref, kr_ref, cos_ref, sa_ref, sb_ref, o_ref):
        krot = _rot(kr_ref[...].astype(F32), cos_ref[...], sa_ref[...], sb_ref[...], 1.0).astype(BF16)
        for h in range(n_heads):
            o_ref[:, h * HEAD_PAD:h * HEAD_PAD + LANE] = kn_ref[:, h * QK_NOPE:(h + 1) * QK_NOPE]
            o_ref[:, h * HEAD_PAD + LANE:(h + 1) * HEAD_PAD] = krot

    return pl.pallas_call(
        body, name=name, grid=(t // tm,), out_shape=jax.ShapeDtypeStruct((t, n_heads * HEAD_PAD), BF16),
        in_specs=[_rb(tm, wk), _rb(tm, LANE, kr_off // LANE), _rb(tm, LANE), _rb(tm, LANE), _rb(tm, LANE)],
        out_specs=_rb(tm, n_heads * HEAD_PAD), compiler_params=_params("parallel"))(kv, z_all, *tabs)


def _rope_k_bwd(dk_pad, dv, tabs, n_heads, name):
    t = dk_pad.shape[0]
    tm = _pick(t, (256, 128))
    wk = n_heads * QK_NOPE

    def body(dk_ref, dv_ref, cos_ref, sa_ref, sb_ref, dkv_ref, dkr_ref):
        acc = dk_ref[:, LANE:HEAD_PAD]
        dkv_ref[:, 0:QK_NOPE] = dk_ref[:, 0:LANE].astype(BF16)
        for h in range(1, n_heads):
            acc = acc + dk_ref[:, h * HEAD_PAD + LANE:(h + 1) * HEAD_PAD]
            dkv_ref[:, h * QK_NOPE:(h + 1) * QK_NOPE] = dk_ref[:, h * HEAD_PAD:h * HEAD_PAD + LANE].astype(BF16)
        dkv_ref[:, wk:] = dv_ref[...].astype(BF16)
        dkr_ref[...] = _rot(acc, cos_ref[...], sa_ref[...], sb_ref[...], -1.0).astype(BF16)

    return pl.pallas_call(
        body, name=name, grid=(t // tm,),
        out_shape=(jax.ShapeDtypeStruct((t, 2 * wk), BF16), jax.ShapeDtypeStruct((t, LANE), BF16)),
        in_specs=[_rb(tm, n_heads * HEAD_PAD), _rb(tm, wk), _rb(tm, LANE), _rb(tm, LANE), _rb(tm, LANE)],
        out_specs=(_rb(tm, 2 * wk), _rb(tm, LANE)), compiler_params=_params("parallel"))(dk_pad, dv, *tabs)


NT_DIMS = (((1,), (1,)), ((), ()))
TN_DIMS = (((0,), (0,)), ((), ()))


def _softmax_rows(q, k):
    s = lax.dot_general(q, k, NT_DIMS, preferred_element_type=F32) * SOFTMAX_SCALE
    e = jnp.exp(s - jnp.max(s, axis=-1, keepdims=True))
    return e * (1.0 / jnp.sum(e, axis=-1, keepdims=True))


def _attn_fwd(q_pad, k_pad, kv, n_heads, name):
    t = q_pad.shape[0]
    tq = _pick(t, (256, 128))

    def body(q_ref, k_ref, v_ref, o_ref):
        p = _softmax_rows(q_ref[...], k_ref[...]).astype(BF16)
        o_ref[...] = jnp.dot(p, v_ref[...], preferred_element_type=F32).astype(BF16)

    return pl.pallas_call(
        body, name=name, grid=(n_heads, t // tq),
        out_shape=jax.ShapeDtypeStruct((t, n_heads * V_HEAD), BF16),
        in_specs=[pl.BlockSpec((tq, HEAD_PAD), lambda h, i: (i, h)),
                  pl.BlockSpec((t, HEAD_PAD), lambda h, i: (0, h)),
                  pl.BlockSpec((t, V_HEAD), lambda h, i: (0, n_heads + h))],
        out_specs=pl.BlockSpec((tq, V_HEAD), lambda h, i: (i, h)),
        compiler_params=_params("parallel", "parallel"))(q_pad, k_pad, kv)


def _attn_bwd(q_pad, k_pad, kv, do, n_heads, name):
    t = q_pad.shape[0]
    tq = _pick(t, (256, 128))
    nq = t // tq

    def body(q_ref, k_ref, v_ref, do_ref, dq_ref, dk_ref, dv_ref):
        @pl.when(pl.program_id(1) == 0)
        def _():
            dk_ref[...] = jnp.zeros_like(dk_ref)
            dv_ref[...] = jnp.zeros_like(dv_ref)

        q, k, dout = q_ref[...], k_ref[...], do_ref[...]
        p = _softmax_rows(q, k)
        dp = lax.dot_general(dout, v_ref[...], NT_DIMS, preferred_element_type=F32)
        ds = (p * (dp - jnp.sum(p * dp, axis=-1, keepdims=True)) * SOFTMAX_SCALE).astype(BF16)
        dq_ref[...] = jnp.dot(ds, k, preferred_element_type=F32)
        dk_ref[...] += lax.dot_general(ds, q, TN_DIMS, preferred_element_type=F32)
        dv_ref[...] += lax.dot_general(p.astype(BF16), dout, TN_DIMS, preferred_element_type=F32)

    return pl.pallas_call(
        body, name=name, grid=(n_heads, nq),
        out_shape=(jax.ShapeDtypeStruct((t, n_heads * HEAD_PAD), F32), jax.ShapeDtypeStruct((t, n_heads * HEAD_PAD), F32),
                   jax.ShapeDtypeStruct((t, n_heads * V_HEAD), F32)),
        in_specs=[pl.BlockSpec((tq, HEAD_PAD), lambda h, i: (i, h)),
                  pl.BlockSpec((t, HEAD_PAD), lambda h, i: (0, h)),
                  pl.BlockSpec((t, V_HEAD), lambda h, i: (0, n_heads + h)),
                  pl.BlockSpec((tq, V_HEAD), lambda h, i: (i, h))],
        out_specs=(pl.BlockSpec((tq, HEAD_PAD), lambda h, i: (i, h)),
                   pl.BlockSpec((t, HEAD_PAD), lambda h, i: (0, h)),
                   pl.BlockSpec((t, V_HEAD), lambda h, i: (0, h))),
        compiler_params=_params("parallel", "arbitrary"))(q_pad, k_pad, kv, do)


def _shift_rows(u, t):
    row = lax.broadcasted_iota(jnp.int32, u.shape, 0)
    prev = jnp.where(row == 0, 0.0, pltpu.roll(u, 1, 0))
    nxt = jnp.where(row == t - 1, 0.0, pltpu.roll(u, t - 1, 0))
    return prev, nxt


def _conv_fwd(z_all, conv_w, cc, name):
    t = z_all.shape[0]
    nb = cc // LANE

    def body(cb_ref, cc_ref, ch_ref, w_ref, y_ref):
        u = cc_ref[...].astype(F32) * ch_ref[...].astype(F32)
        prev, nxt = _shift_rows(u, t)
        w = w_ref[...]
        conv = prev * w[0:1, :] + u * w[1:2, :] + nxt * w[2:3, :]
        y_ref[...] = (cb_ref[...].astype(F32) * conv).astype(BF16)

    col = lambda g: pl.BlockSpec((t, LANE), lambda j: (0, g * nb + j))
    return pl.pallas_call(
        body, name=name, grid=(nb,), out_shape=jax.ShapeDtypeStruct((t, cc), BF16),
        in_specs=[col(0), col(1), col(2), pl.BlockSpec((3, LANE), lambda j: (0, j))],
        out_specs=pl.BlockSpec((t, LANE), lambda j: (0, j)),
        compiler_params=_params("parallel"))(z_all, z_all, z_all, conv_w)


def _conv_bwd(dy, z_all, conv_w, cc, name):
    t = z_all.shape[0]
    nb = cc // LANE

    def body(dy_ref, cb_ref, cc_ref, ch_ref, w_ref, dcb_ref, dcc_ref, dch_ref, dw_ref):
        c_c, c_h = cc_ref[...].astype(F32), ch_ref[...].astype(F32)
        u = c_c * c_h
        prev, nxt = _shift_rows(u, t)
        w = w_ref[...]
        dyv = dy_ref[...].astype(F32)
        dcb_ref[...] = (dyv * (prev * w[0:1, :] + u * w[1:2, :] + nxt * w[2:3, :])).astype(BF16)
        dconv = dyv * cb_ref[...].astype(F32)
        dw_ref[0:1, :] = jnp.sum(dconv * prev, axis=0, keepdims=True)
        dw_ref[1:2, :] = jnp.sum(dconv * u, axis=0, keepdims=True)
        dw_ref[2:3, :] = jnp.sum(dconv * nxt, axis=0, keepdims=True)
        dprev, dnxt = _shift_rows(dconv, t)
        du = dnxt * w[0:1, :] + dconv * w[1:2, :] + dprev * w[2:3, :]
        dcc_ref[...] = (du * c_h).astype(BF16)
        dch_ref[...] = (du * c_c).astype(BF16)

    col = lambda g: pl.BlockSpec((t, LANE), lambda j: (0, g * nb + j))
    one = pl.BlockSpec((t, LANE), lambda j: (0, j))
    wsp = pl.BlockSpec((3, LANE), lambda j: (0, j))
    act = jax.ShapeDtypeStruct((t, cc), BF16)
    return pl.pallas_call(
        body, name=name, grid=(nb,),
        out_shape=(act, act, act, jax.ShapeDtypeStruct((3, cc), F32)),
        in_specs=[one, col(0), col(1), col(2), wsp],
        out_specs=(one, one, one, wsp),
        compiler_params=_params("parallel"))(dy, z_all, z_all, z_all, conv_w)


def _sigmoid(v):
    return 1.0 / (1.0 + jnp.exp(-v))


def _merge_fwd(z_all, b_gate, y_a, y_b, gate_off, name):
    t, d = y_a.shape
    tm = _pick(t, (128,))
    gb = gate_off // d

    def body(za_ref, zb_ref, ba_ref, bb_ref, ya_ref, yb_ref, m_ref):
        ga = _sigmoid(za_ref[...].astype(F32) + ba_ref[...])
        gbv = _sigmoid(zb_ref[...].astype(F32) + bb_ref[...])
        m_ref[...] = (ga * ya_ref[...].astype(F32) + gbv * yb_ref[...].astype(F32)).astype(BF16)

    return pl.pallas_call(
        body, name=name, grid=(t // tm,), out_shape=jax.ShapeDtypeStruct((t, d), BF16),
        in_specs=[_rb(tm, d, gb), _rb(tm, d, gb + 1), _vec(d, 0), _vec(d, 1), _rb(tm, d), _rb(tm, d)],
        out_specs=_rb(tm, d), compiler_params=_params("parallel"))(z_all, z_all, b_gate, b_gate, y_a, y_b)


def _merge_bwd(dm, z_all, b_gate, y_a, y_b, gate_off, name):
    t, d = y_a.shape
    tm = _pick(t, (128,))
    gb = gate_off // d

    def body(dm_ref, za_ref, zb_ref, ba_ref, bb_ref, ya_ref, yb_ref, dya_ref, dyb_ref, dzg_ref, db_ref):
        dmv = dm_ref[...].astype(F32)
        ga = _sigmoid(za_ref[...].astype(F32) + ba_ref[...])
        gbv = _sigmoid(zb_ref[...].astype(F32) + bb_ref[...])
        dya_ref[...] = (dmv * ga).astype(BF16)
        dyb_ref[...] = (dmv * gbv).astype(BF16)
        dza = dmv * ya_ref[...].astype(F32) * (ga * (1.0 - ga))
        dzb = dmv * yb_ref[...].astype(F32) * (gbv * (1.0 - gbv))
        dzg_ref[:, 0:d] = dza.astype(BF16)
        dzg_ref[:, d:2 * d] = dzb.astype(BF16)

        @pl.when(pl.program_id(0) == 0)
        def _():
            db_ref[...] = jnp.zeros_like(db_ref)

        db_ref[:, 0:d] += jnp.sum(dza, axis=0, keepdims=True)
        db_ref[:, d:2 * d] += jnp.sum(dzb, axis=0, keepdims=True)

    act = jax.ShapeDtypeStruct((t, d), BF16)
    return pl.pallas_call(
        body, name=name, grid=(t // tm,),
        out_shape=(act, act, jax.ShapeDtypeStruct((t, 2 * d), BF16), jax.ShapeDtypeStruct((1, 2 * d), F32)),
        in_specs=[_rb(tm, d), _rb(tm, d, gb), _rb(tm, d, gb + 1), _vec(d, 0), _vec(d, 1), _rb(tm, d), _rb(tm, d)],
        out_specs=(_rb(tm, d), _rb(tm, d), _rb(tm, 2 * d), _vec(2 * d)),
        compiler_params=_params("arbitrary"))(dm, z_all, z_all, b_gate, b_gate, y_a, y_b)


def _swiglu_fwd(gate, up, name):
    t, f = gate.shape
    tm = _pick(t, (128,))

    def body(g_ref, u_ref, a_ref):
        g = g_ref[...].astype(F32)
        a_ref[...] = (g * _sigmoid(g) * u_ref[...].astype(F32)).astype(BF16)

    return pl.pallas_call(
        body, name=name, grid=(t // tm,), out_shape=jax.ShapeDtypeStruct((t, f), BF16),
        in_specs=[_rb(tm, f), _rb(tm, f)], out_specs=_rb(tm, f), compiler_params=_params("parallel"))(gate, up)


def _swiglu_bwd(dact, gate, up, name):
    t, f = gate.shape
    tm = _pick(t, (128,))

    def body(da_ref, g_ref, u_ref, dg_ref, du_ref):
        g, da = g_ref[...].astype(F32), da_ref[...].astype(F32)
        sg = _sigmoid(g)
        dg_ref[...] = (da * u_ref[...].astype(F32) * (sg * (1.0 + g * (1.0 - sg)))).astype(BF16)
        du_ref[...] = (da * (g * sg)).astype(BF16)

    act = jax.ShapeDtypeStruct((t, f), BF16)
    return pl.pallas_call(
        body, name=name, grid=(t // tm,), out_shape=(act, act),
        in_specs=[_rb(tm, f)] * 3, out_specs=(_rb(tm, f), _rb(tm, f)), compiler_params=_params("parallel"))(dact, gate, up)


def _loss_head(x2, target, g, name):
    t, d = x2.shape
    tm = _pick(t, (128,))

    def body(x_ref, t_ref, g_ref, loss_ref, dx_ref, dxb_ref, dg_ref):
        xv, gv = x_ref[...], g_ref[...]
        err = xv * _rms_inv(xv) * gv - t_ref[...]
        dx, dgrow = _rms_bwd_rows(err * (1.0 / d), xv, gv)
        dx_ref[...] = dx
        dxb_ref[...] = dx.astype(BF16)

        @pl.when(pl.program_id(0) == 0)
        def _():
            loss_ref[...] = jnp.zeros_like(loss_ref)
            dg_ref[...] = jnp.zeros_like(dg_ref)

        loss_ref[...] += (0.5 / d) * jnp.sum(jnp.sum(err * err, axis=1, keepdims=True), axis=0, keepdims=True)
        dg_ref[...] += jnp.sum(dgrow, axis=0, keepdims=True)

    return pl.pallas_call(
        body, name=name, grid=(t // tm,),
        out_shape=(jax.ShapeDtypeStruct((1, 1), F32), jax.ShapeDtypeStruct((t, d), F32),
                   jax.ShapeDtypeStruct((t, d), BF16), jax.ShapeDtypeStruct((1, d), F32)),
        in_specs=[_rb(tm, d), _rb(tm, d), _vec(d)],
        out_specs=(pl.BlockSpec((1, 1), lambda i: (0, 0)), _rb(tm, d), _rb(tm, d), _vec(d)),
        compiler_params=_params("arbitrary"))(x2, target, g)


def _adamw(parts, w, m, v, name):
    r, c = w.shape
    cp = parts.shape[2]
    assert parts.shape[1] >= r and cp >= c
    tr = r if r * c <= ADAM_BLOCK_ELEMS else _pick(r, tuple(s for s in (512, 256, 128, 64, 32, 16, 8) if s * c <= ADAM_BLOCK_ELEMS))
    assert tr == parts.shape[1] or tr % 8 == 0

    def body(p_ref, w_ref, m_ref, v_ref, g_ref, d_ref, nm_ref, nv_ref):
        g = p_ref[0, :, 0:c].astype(F32)
        for s in range(1, N_DEV):
            g = g + p_ref[s, :, 0:c].astype(F32)
        nm = ADAM_B1 * m_ref[...] + (1.0 - ADAM_B1) * g
        nv = ADAM_B2 * v_ref[...] + (1.0 - ADAM_B2) * (g * g)
        m_hat = nm / (1.0 - ADAM_B1 ** ADAM_STEP)
        v_hat = nv / (1.0 - ADAM_B2 ** ADAM_STEP)
        g_ref[...] = g
        d_ref[...] = -ADAM_LR * (m_hat / (jnp.sqrt(v_hat) + ADAM_EPS) + ADAM_WD * w_ref[...])
        nm_ref[...] = nm
        nv_ref[...] = nv

    blk = pl.BlockSpec((tr, c), lambda i: (i, 0))
    out = jax.ShapeDtypeStruct((r, c), F32)
    return pl.pallas_call(
        body, name=name, grid=(r // tr,), out_shape=(out, out, out, out),
        in_specs=[pl.BlockSpec((N_DEV, tr, cp), lambda i: (0, i, 0)), blk, blk, blk],
        out_specs=(blk, blk, blk, blk), compiler_params=_params("parallel"))(parts, w, m, v)


def _cols_of(g):
    return jnp.transpose(g, (1, 0, 2)).reshape(g.shape[1], N_DEV * g.shape[2])


def _col_parts(dw):
    k, n8 = dw.shape
    return jnp.transpose(dw.reshape(k, N_DEV, n8 // N_DEV), (1, 0, 2))


def kernel(x, positions, g_mix, w_in, b_gate, conv_w, g_q_a, w_q_b, g_kv_a, w_kv_b, w_branch, w_out, g_ffn, w_ffn_gate, w_ffn_up, w_ffn_down, g_final, loss_target, m_g_mix, m_w_in, m_b_gate, m_conv_w, m_g_q_a, m_w_q_b, m_g_kv_a, m_w_kv_b, m_w_branch, m_w_out, m_g_ffn, m_w_ffn_gate, m_w_ffn_up, m_w_ffn_down, m_g_final, v_g_mix, v_w_in, v_b_gate, v_conv_w, v_g_q_a, v_w_q_b, v_g_kv_a, v_w_kv_b, v_w_branch, v_w_out, v_g_ffn, v_w_ffn_gate, v_w_ffn_up, v_w_ffn_down, v_g_final):
    given = dict(locals())
    xs = x[0]
    t, d = xs.shape
    cc = d // 2
    n_heads = cc // V_HEAD
    in_cols = N_DEV * w_in.shape[2]
    q_off, kv_off, kr_off = 3 * cc, 3 * cc + Q_LORA, 3 * cc + Q_LORA + KV_LORA
    head_cols = kr_off + QK_ROPE
    head_pad = -(-(kr_off + LANE) // 1024) * 1024
    assert in_cols == head_cols + 2 * d and q_off % Q_LORA == 0 and kv_off % KV_LORA == 0 and kr_off % LANE == 0
    fs = w_ffn_gate.shape[2]
    fsp = -(-fs // LANE) * LANE
    ffp = N_DEV * fsp
    me = _me_index()
    bf = lambda a: a.astype(BF16)
    one_slot = lambda refs, i: refs[i].at[0]

    g_in = _gather_start([bf(w_in[0]), conv_w[0]], me, "ag1_start_in")
    h = _rms_fwd(xs, g_mix, "rms_mix", after=g_in[3])
    f_in = _forward_start(_split_wait(g_in, one_slot, h, "ag1_wait_in"), "ag2_start_in")
    mix_cols = [w_q_b.shape[2], w_kv_b.shape[2], w_branch.shape[3], None]
    zero = g_in[3][0, 0]
    bf_later = lambda a: (a + zero).astype(BF16)
    g_mix_w = _gather_start([bf_later(w_q_b[0]), bf_later(w_kv_b[0]), bf_later(w_branch[0].reshape(2 * cc, -1)), bf_later(w_out[0])],
                            me, "ag1_start_mix", after=f_in[3], cols=mix_cols)
    w_in_g, cw_g = _split_wait(f_in, one_slot, g_mix_w[3], "ag2_wait_in")
    w_in_full = _cols_of(w_in_g)
    w_head = jnp.concatenate([w_in_full[:, :head_cols], jnp.zeros((d, head_pad - head_cols), BF16)], axis=1)
    w_gate = w_in_full[:, head_cols:]
    cw = _cols_of(cw_g)

    inv_freq = ROPE_THETA ** (-jnp.arange(0, QK_ROPE, 2, dtype=F32) / QK_ROPE)
    ang = positions[0].astype(F32)[:, None] * inv_freq[None, :]
    cos, sin = jnp.cos(ang), jnp.sin(ang)
    z32, z64 = jnp.zeros((t, 32), F32), jnp.zeros((t, 64), F32)
    tabs = (jnp.concatenate([cos, cos, jnp.ones((t, 64), F32)], axis=1),
            jnp.concatenate([-sin, z32, z64], axis=1),
            jnp.concatenate([z32, sin, z64], axis=1))

    z_head = _matmul(h, w_head, name="mm_z_head")
    zg = _matmul(h, w_gate, name="mm_z_gate")
    f_mix = _forward_start(_split_wait(g_mix_w, _first_window(mix_cols), zg, "ag1_wait_mix"), "ag2_start_mix", cols=mix_cols)
    ffn_cols = [fsp, fsp, None]
    g_ffn_w = _gather_start([bf_later(jnp.pad(w_ffn_gate[0], ((0, 0), (0, fsp - fs)))), bf_later(jnp.pad(w_ffn_up[0], ((0, 0), (0, fsp - fs)))),
                             bf_later(jnp.pad(w_ffn_down[0], ((0, fsp - fs), (0, 0))))], me, "ag1_start_ffn", after=f_mix[3], cols=ffn_cols)
    y_a = _conv_fwd(z_head, cw, cc, "conv_fwd")
    qn, kvn = _latent_norm(z_head, g_q_a, g_kv_a, q_off, kv_off, "latent_norm")
    wq_full, wkv_full, wbr, wo_g = _split_wait(f_mix, _first_window(mix_cols), g_ffn_w[3], "ag2_wait_mix")
    wq = wq_full.reshape(Q_LORA, n_heads, QK_NOPE + QK_ROPE)
    wq_pad = jnp.pad(wq, ((0, 0), (0, 0), (0, HEAD_PAD - QK_NOPE - QK_ROPE))).reshape(Q_LORA, n_heads * HEAD_PAD)
    wkv = wkv_full.reshape(KV_LORA, n_heads, 2, QK_NOPE)
    wkv_perm = jnp.transpose(wkv, (0, 2, 1, 3)).reshape(KV_LORA, 2 * n_heads * QK_NOPE)
    wb_a, wb_b = wbr[:cc], wbr[cc:]
    wo = wo_g.reshape(d, d)
    q_pad = _rope_q(_matmul(qn, wq_pad, name="mm_q", out_dtype=F32), tabs, n_heads, 1.0, BF16, "rope_q")
    kv = _matmul(kvn, wkv_perm, name="mm_kv")
    k_pad = _rope_k(kv, z_head, tabs, n_heads, kr_off, "rope_k")
    y_b = _attn_fwd(q_pad, k_pad, kv, n_heads, "attn_fwd")
    f_ffn = _forward_start(_split_wait(g_ffn_w, _first_window(ffn_cols), y_b, "ag1_wait_ffn"), "ag2_start_ffn", cols=ffn_cols)
    ybr_a = _matmul(y_a, wb_a, name="mm_br_a", after=f_ffn[3])
    ybr_b = _matmul(y_b, wb_b, name="mm_br_b")
    merged = _merge_fwd(zg, b_gate, ybr_a, ybr_b, 0, "merge_fwd")
    x1 = _matmul(merged, wo, name="mm_out", res=xs, out_dtype=F32)
    h2 = _rms_fwd(x1, g_ffn, "rms_ffn")
    wg, wu, wd_g = _split_wait(f_ffn, _first_window(ffn_cols), h2, "ag2_wait_ffn")
    wd = wd_g.reshape(ffp, d)
    gate = _matmul(h2, wg, name="mm_gate")
    up = _matmul(h2, wu, name="mm_up")
    act = _swiglu_fwd(gate, up, "swiglu_fwd")
    x2 = _matmul(act, wd, name="mm_down", res=x1, out_dtype=F32)
    loss_part, dx2, dx2b, dg_final = _loss_head(x2, loss_target[0], g_final.reshape(1, d), "loss_head")

    dact = _matmul(dx2b, wd, tb=True, name="mm_d_act")
    dwd = _matmul(act, dx2b, ta=True, name="mm_dw_down")
    r_down = _exchange_start([dwd.reshape(N_DEV, fsp, d)], me, "rs_start_down")
    dgate, dup = _swiglu_bwd(dact, gate, up, "swiglu_bwd")
    dwg = _matmul(h2, dgate, ta=True, name="mm_dw_gate", after=r_down[3])
    dwu = _matmul(h2, dup, ta=True, name="mm_dw_up")
    r_gate_up = _exchange_start([dwg, dwu], me, "rs_start_gate_up", cols=[fsp, fsp])
    dh2 = _matmul(dgate, wg, tb=True, name="mm_d_h2_gate", out_dtype=F32, after=r_gate_up[3])
    dh2 = _matmul(dup, wu, tb=True, name="mm_d_h2_up", res=dh2, out_dtype=F32)
    dx1, dx1b, dg_ffn = _rms_bwd(dh2, x1, g_ffn, dx2, "rms_ffn_bwd")
    dwo = _matmul(merged, dx1b, ta=True, name="mm_dw_out")
    r_out = _exchange_start([dwo.reshape(N_DEV, d // N_DEV, d)], me, "rs_start_out")
    dmerged = _matmul(dx1b, wo, tb=True, name="mm_d_merged", after=r_out[3])
    dybr_a, dybr_b, dzg, db_gate = _merge_bwd(dmerged, zg, b_gate, ybr_a, ybr_b, 0, "merge_bwd")
    dwb_a = _matmul(y_a, dybr_a, ta=True, name="mm_dw_br_a")
    dwb_b = _matmul(y_b, dybr_b, ta=True, name="mm_dw_br_b")
    r_br = _exchange_start([dwb_a, dwb_b], me, "rs_start_branch", cols=[d // N_DEV] * 2)
    dy_a = _matmul(dybr_a, wb_a, tb=True, name="mm_d_y_a", after=r_br[3])
    dy_b = _matmul(dybr_b, wb_b, tb=True, name="mm_d_y_b")
    dw_gate = _matmul(h, dzg, ta=True, name="mm_dw_in_gate")
    dh_gate = _matmul(dzg, w_gate, tb=True, name="mm_d_h_gate", out_dtype=F32)
    dq_pad, dk_pad, dv = _attn_bwd(q_pad, k_pad, kv, dy_b, n_heads, "attn_bwd")
    dq_raw = _rope_q(dq_pad, tabs, n_heads, -1.0, BF16, "rope_q_bwd")
    dkv, dkr = _rope_k_bwd(dk_pad, dv, tabs, n_heads, "rope_k_bwd")
    dwq = _matmul(qn, dq_raw, ta=True, name="mm_dw_q")
    dwkv = _matmul(kvn, dkv, ta=True, name="mm_dw_kv")
    dwq_full = dwq.reshape(Q_LORA, n_heads, HEAD_PAD)[:, :, :QK_NOPE + QK_ROPE].reshape(Q_LORA, -1)
    dwkv_full = jnp.transpose(dwkv.reshape(KV_LORA, 2, n_heads, QK_NOPE), (0, 2, 1, 3)).reshape(KV_LORA, -1)
    r_qkv = _exchange_start([_col_parts(dwq_full), _col_parts(dwkv_full)], me, "rs_start_q_kv")
    dqn = _matmul(dq_raw, wq_pad, tb=True, name="mm_d_qn", after=r_qkv[3])
    dkvn = _matmul(dkv, wkv_perm, tb=True, name="mm_d_kvn")
    dqa, dkva, dg_q, dg_kv = _latent_norm_bwd(dqn, dkvn, z_head, g_q_a, g_kv_a, q_off, kv_off, "latent_norm_bwd")
    dcb, dcc, dch, dcw = _conv_bwd(dy_a, z_head, cw, cc, "conv_bwd")
    dz_head = jnp.concatenate([dcb, dcc, dch, dqa, dkva, dkr, jnp.zeros((t, head_pad - kr_off - LANE), BF16)], axis=1)
    dw_head = _matmul(h, dz_head, ta=True, name="mm_dw_in_head")
    r_in = _exchange_start([_col_parts(jnp.concatenate([dw_head[:, :head_cols], dw_gate], axis=1))], me, "rs_start_in")
    dh = _matmul(dz_head, w_head, tb=True, name="mm_d_h_head", res=dh_gate, out_dtype=F32, after=r_in[3])
    dx, _, dg_mix = _rms_bwd(dh, xs, g_mix, dx1, "rms_mix_bwd")

    results = {}
    last = dx

    def update(started, targets):
        nonlocal last
        n = len(targets)
        recvs = _split_wait(started, _land_block(n), last, "rs_wait_" + targets[0][0])[n:]
        for recv, (key, w_, m_, v_) in zip(recvs, targets):
            results[key] = _adamw(recv, w_, m_, v_, "adamw_" + key)
            last = results[key][0]

    shard = lambda wname: (wname,) + tuple(given[p + wname][0] for p in ("", "m_", "v_"))
    branch = lambda b: ("w_branch_%d" % b,) + tuple(given[p + "w_branch"][0, b] for p in ("", "m_", "v_"))
    update(r_down, [shard("w_ffn_down")])
    update(r_gate_up, [shard("w_ffn_gate"), shard("w_ffn_up")])
    update(r_out, [shard("w_out")])
    update(r_br, [branch(0), branch(1)])
    results["w_branch"] = tuple(jnp.stack([results["w_branch_0"][k], results["w_branch_1"][k]]) for k in range(4))
    update(r_qkv, [shard("w_q_b"), shard("w_kv_b")])

    small = [("g_mix", dg_mix), ("b_gate", db_gate), ("g_q_a", dg_q), ("g_kv_a", dg_kv), ("g_ffn", dg_ffn),
             ("g_final", dg_final), ("conv_w", dcw.reshape(1, 3 * cc))]
    packed = _all_gather(jnp.concatenate([p for _, p in small], axis=1), "ag_small_grads", last)
    off = 0
    for wname, p in small:
        n = p.shape[1]
        parts = packed[:, :, off:off + n]
        off += n
        if wname == "conv_w":
            width = conv_w.shape[2]
            parts = lax.dynamic_slice_in_dim(parts.reshape(N_DEV, 3, cc), me * width, width, axis=2)
            flat = (3, width)
        else:
            flat = (1, n)
        results[wname] = _adamw(parts, given[wname].reshape(flat), given["m_" + wname].reshape(flat),
                                given["v_" + wname].reshape(flat), "adamw_" + wname)
        last = results[wname][0]
    update(r_in, [shard("w_in")])

    loss = lax.psum(loss_part[0, 0], MESH_AXES)
    order = ["g_mix", "w_in", "b_gate", "conv_w", "g_q_a", "w_q_b", "g_kv_a", "w_kv_b", "w_branch", "w_out", "g_ffn",
             "w_ffn_gate", "w_ffn_up", "w_ffn_down", "g_final"]
    out = [loss, dx[None]]
    for k in range(4):
        out += [results[n][k].reshape(given[n].shape) for n in order]
    return tuple(out)
```

```python
import functools
import math

import jax
import jax.numpy as jnp
from jax import lax
from jax.experimental import pallas as pl
from jax.experimental.pallas import tpu as pltpu

F32 = jnp.float32
BF16 = jnp.bfloat16
N_DEV = 8
MESH_AXES = ("x", "y", "c")
MESH = pl.DeviceIdType.MESH

QK_NOPE = 128
QK_ROPE = 64
V_HEAD = 128
HEAD_PAD = 256
Q_LORA = 1024
KV_LORA = 512
ROPE_THETA = 10000.0
RMS_EPS = 1e-6
SOFTMAX_SCALE = 1.0 / math.sqrt(QK_NOPE + QK_ROPE)
ADAM_LR, ADAM_B1, ADAM_B2, ADAM_EPS, ADAM_WD, ADAM_STEP = 0.001, 0.9, 0.999, 1e-08, 0.01, 10

VMEM_LIMIT = 60 * 1024 * 1024
LANE = 128
ADAM_BLOCK_ELEMS = 1 << 18
COPY_BLOCK_BYTES = 4 * 1024 * 1024
MATMUL_K_TILES =(4096, 2816, 2048, 1024, 512, 256, 128)
MATMUL_VMEM_BUDGET = 48 * 1024 * 1024


def _pick(n, cands=(1024, 512, 256, 128)):
    for c in cands:
        if n % c == 0:
            return c
    return n


def _params(*sem):
    return pltpu.CompilerParams(dimension_semantics=sem, vmem_limit_bytes=VMEM_LIMIT)


def _rb(tm, c, cb=0):
    return pl.BlockSpec((tm, c), lambda i: (i, cb))


def _vec(c, cb=0):
    return pl.BlockSpec((1, c), lambda i: (0, cb))


def _all_gather(x, name, after):
    def body(x_ref, after_ref, out_ref, send_sems, recv_sems, local_sem):
        x_, y_, c_ = lax.axis_index("x"), lax.axis_index("y"), lax.axis_index("c")
        me, sibling = (x_, y_, c_), (x_, y_, 1 - c_)
        chips = [(1 - x_, y_), (x_, 1 - y_), (1 - x_, 1 - y_)]

        def slot(px, py, pc):
            return out_ref.at[4 * px + 2 * py + pc]

        def copy(k, block, to, src=None):
            return pltpu.make_async_remote_copy(
                src_ref=slot(*block) if src is None else src, dst_ref=slot(*block),
                send_sem=send_sems.at[k], recv_sem=recv_sems.at[k], device_id=to, device_id_type=MESH)

        mine = pltpu.make_async_copy(x_ref, slot(*me), local_sem)
        mine.start()
        first = [copy(0, me, sibling, src=x_ref)]
        first += [copy(1 + j, me, (*chip, c_), src=x_ref) for j, chip in enumerate(chips)]
        for cp in first:
            cp.start()
        passed = [copy(4 + j, (*chip, c_), sibling) for j, chip in enumerate(chips)]
        for j, chip in enumerate(chips):
            copy(1 + j, (*chip, c_), me).wait_recv()
            passed[j].start()
        copy(0, sibling, me).wait_recv()
        for j, chip in enumerate(chips):
            copy(4 + j, (*chip, 1 - c_), me).wait_recv()
        for cp in first + passed:
            cp.wait_send()
        mine.wait()

    return pl.pallas_call(
        body, name=name,
        out_shape=jax.ShapeDtypeStruct((N_DEV,) + x.shape, x.dtype),
        in_specs=[pl.BlockSpec(memory_space=pl.ANY), pl.BlockSpec(memory_space=pl.ANY)],
        out_specs=pl.BlockSpec(memory_space=pl.ANY),
        scratch_shapes=[pltpu.SemaphoreType.DMA((7,)), pltpu.SemaphoreType.DMA((7,)), pltpu.SemaphoreType.DMA(())],
    )(x, after)


_HBM = pl.BlockSpec(memory_space=pltpu.HBM)
_SEM = pl.BlockSpec(memory_space=pltpu.SEMAPHORE)
_ANY = pl.BlockSpec(memory_space=pl.ANY)
_EFFECT = pltpu.SideEffectType.DATAFLOW_SIDE_EFFECTING


def _me_index():
    return 4 * lax.axis_index("x") + 2 * lax.axis_index("y") + lax.axis_index("c")


def _window(ref, idx, cols):
    if cols is None:
        return ref.at[idx]
    return ref.at[:, pl.ds(idx * cols if isinstance(idx, int) else pl.multiple_of(idx * cols, LANE), cols)]


def _own_block_only(src, me, name, cols=None, src_cols=False):
    k, n = src.shape if src_cols is False else ((src.shape[0], src_cols) if src_cols else src.shape[1:])
    assert cols is None or (n == cols and cols % LANE == 0)
    out_shape = (N_DEV, k, n) if cols is None else (k, N_DEV * cols)
    limit = COPY_BLOCK_BYTES // (n * src.dtype.itemsize)
    tr = k if k <= limit else _pick(k, tuple(s for s in (1024, 512, 256, 128, 64, 32, 16) if s <= limit))

    def spec(layout):
        if layout is False:
            return pl.BlockSpec((tr, n), lambda i, me_ref: (i, 0))
        if layout is None:
            return pl.BlockSpec((None, tr, n), lambda i, me_ref: (me_ref[0], i, 0))
        return pl.BlockSpec((tr, n), lambda i, me_ref: (i, me_ref[0]))

    def body(me_ref, src_ref, dst_ref):
        dst_ref[...] = src_ref[...]

    return pl.pallas_call(
        body, name=name, out_shape=jax.ShapeDtypeStruct(out_shape, src.dtype),
        grid_spec=pltpu.PrefetchScalarGridSpec(num_scalar_prefetch=1, grid=(k // tr,), in_specs=[spec(src_cols)], out_specs=spec(cols)),
        compiler_params=_params("parallel"))(jnp.reshape(me, (1,)).astype(jnp.int32), src)


def _split_start(bufs, sem_shape, issue, name, after=None):
    n = len(bufs)
    rows, per_row = sem_shape
    sem = lambda sems, i, k: sems.at[i * per_row + k]
    first_out = n + (after is not None)

    def body(*refs):
        issue(refs[:n], refs[first_out], refs[first_out + 1], sem)
        refs[-1][...] = jnp.zeros_like(refs[-1])

    sems = pltpu.SemaphoreType.DMA((rows * per_row,))
    operands = [pltpu.with_memory_space_constraint(b, pltpu.HBM) for b in bufs] + ([] if after is None else [after])
    outs = pl.pallas_call(
        body, name=name,
        out_shape=(sems, sems) + tuple(pltpu.HBM(b.shape, b.dtype) for b in bufs) + (jax.ShapeDtypeStruct((8, LANE), F32),),
        in_specs=(_HBM,) * n + (_ANY,) * (after is not None),
        out_specs=(_SEM, _SEM) + (_HBM,) * n + (pl.BlockSpec(memory_space=pltpu.VMEM),),
        input_output_aliases={i: 2 + i for i in range(n)},
        compiler_params=pltpu.CompilerParams(has_side_effects=_EFFECT),
    )(*operands)
    return outs[0], outs[1], list(outs[2:2 + n]), outs[-1], sem_shape


def _split_wait(started, block_of, after, name):
    send_sems, recv_sems, bufs, _, sem_shape = started
    n = len(bufs)

    def body(*refs):
        x_, y_, c_ = lax.axis_index("x"), lax.axis_index("y"), lax.axis_index("c")
        for i in range(sem_shape[0]):
            blk = block_of(refs, i)
            for k in range(sem_shape[1]):
                cp = pltpu.make_async_remote_copy(
                    src_ref=blk, dst_ref=blk, send_sem=refs[n].at[i * sem_shape[1] + k], recv_sem=refs[n + 1].at[i * sem_shape[1] + k],
                    device_id=(x_, y_, c_), device_id_type=MESH)
                cp.wait_send()
                cp.wait_recv()

    outs = pl.pallas_call(
        body, name=name,
        out_shape=tuple(pltpu.HBM(b.shape, b.dtype) for b in bufs),
        in_specs=(_HBM,) * n + (_SEM, _SEM, _ANY), out_specs=(_HBM,) * n,
        input_output_aliases={i: i for i in range(n)},
        compiler_params=pltpu.CompilerParams(has_side_effects=_EFFECT),
    )(*bufs, send_sems, recv_sems, after)
    return list(outs)


def _first_window(cols):
    return lambda refs, i: _window(refs[i], 0, cols[i])


def _gather_start(blocks, me, name, after=None, cols=None):
    cols = cols or [None] * len(blocks)

    def issue(buf_refs, send_sems, recv_sems, sem):
        x_, y_, c_ = lax.axis_index("x"), lax.axis_index("y"), lax.axis_index("c")
        me_idx = 4 * x_ + 2 * y_ + c_
        targets = [(x_, y_, 1 - c_), (1 - x_, y_, c_), (x_, 1 - y_, c_), (1 - x_, 1 - y_, c_)]
        for i, buf in enumerate(buf_refs):
            mine = _window(buf, me_idx, cols[i])
            for k, to in enumerate(targets):
                pltpu.make_async_remote_copy(
                    src_ref=mine, dst_ref=mine, send_sem=sem(send_sems, i, k), recv_sem=sem(recv_sems, i, k),
                    device_id=to, device_id_type=MESH).start()

    bufs = [_own_block_only(b, me, "%s_own%d" % (name, i), c) for i, (b, c) in enumerate(zip(blocks, cols))]
    return _split_start(bufs, (len(blocks), 4), issue, name, after)


def _forward_start(bufs, name, after=None, cols=None):
    cols = cols or [None] * len(bufs)

    def issue(buf_refs, send_sems, recv_sems, sem):
        x_, y_, c_ = lax.axis_index("x"), lax.axis_index("y"), lax.axis_index("c")
        chips = [(1 - x_, y_), (x_, 1 - y_), (1 - x_, 1 - y_)]
        for i, buf in enumerate(buf_refs):
            for k, (px, py) in enumerate(chips):
                landed = _window(buf, 4 * px + 2 * py + c_, cols[i])
                pltpu.make_async_remote_copy(
                    src_ref=landed, dst_ref=landed, send_sem=sem(send_sems, i, k), recv_sem=sem(recv_sems, i, k),
                    device_id=(x_, y_, 1 - c_), device_id_type=MESH).start()

    return _split_start(bufs, (len(bufs), 3), issue, name, after)


def _exchange_start(parts, me, name, cols=None):
    n = len(parts)
    cols = cols or [None] * n

    def issue(refs, send_sems, recv_sems, sem):
        x_, y_, c_ = lax.axis_index("x"), lax.axis_index("y"), lax.axis_index("c")
        me_idx = 4 * x_ + 2 * y_ + c_
        for i in range(n):
            for d in range(1, N_DEV):
                px = 1 - x_ if d & 4 else x_
                py = 1 - y_ if d & 2 else y_
                pc = 1 - c_ if d & 1 else c_
                pltpu.make_async_remote_copy(
                    src_ref=_window(refs[i], 4 * px + 2 * py + pc, cols[i]), dst_ref=refs[n + i].at[me_idx],
                    send_sem=sem(send_sems, i, d - 1), recv_sem=sem(recv_sems, i, d - 1),
                    device_id=(px, py, pc), device_id_type=MESH).start()

    lands = [_own_block_only(p, me, "%s_own%d" % (name, i), src_cols=c) for i, (p, c) in enumerate(zip(parts, cols))]
    return _split_start(list(parts) + lands, (n, N_DEV - 1), issue, name)


def _land_block(refs_offset):
    return lambda refs, i: refs[refs_offset + i].at[0]


def _matmul(a, b, *, name, ta=False, tb=False, res=None, out_dtype=BF16, after=None):
    (kdim, m) = a.shape if ta else a.shape[::-1]
    (n, kdim_b) = b.shape if tb else b.shape[::-1]
    assert kdim == kdim_b, (a.shape, b.shape, ta, tb)
    tm, tn = _pick(m), _pick(n)
    out_bytes = jnp.dtype(out_dtype).itemsize

    def vmem_bytes(tk):
        return 2 * (2 * tk * (tm + tn) + tm * tn * (out_bytes + (4 if res is not None else 0))) + (4 * tm * tn if tk < kdim else 0)

    fitting = [c for c in MATMUL_K_TILES if kdim % c == 0 and vmem_bytes(c) <= MATMUL_VMEM_BUDGET]
    tk = fitting[0] if fitting else kdim
    nk = kdim // tk
    a_spec = pl.BlockSpec((tk, tm), lambda i, j, k: (k, i)) if ta else pl.BlockSpec((tm, tk), lambda i, j, k: (i, k))
    b_spec = pl.BlockSpec((tn, tk), lambda i, j, k: (j, k)) if tb else pl.BlockSpec((tk, tn), lambda i, j, k: (k, j))
    o_spec = pl.BlockSpec((tm, tn), lambda i, j, k: (i, j))
    dims = (((0 if ta else 1,), (1 if tb else 0,)), ((), ()))

    def body(*refs):
        a_ref, b_ref = refs[:2]
        r_ref = None if res is None else refs[2]
        def finish(v):
            if r_ref is not None:
                v = r_ref[...] + v
            o_ref[...] = v.astype(out_dtype)

        part = lax.dot_general(a_ref[...], b_ref[...], dims, preferred_element_type=F32)
        if nk == 1:
            o_ref = refs[-1]
            finish(part)
            return
        o_ref, acc = refs[-2:]
        k = pl.program_id(2)

        @pl.when(k == 0)
        def _():
            acc[...] = part

        @pl.when(k > 0)
        def _():
            acc[...] += part

        @pl.when(k == nk - 1)
        def _():
            finish(acc[...])

    operands = [a, b] + ([] if res is None else [res]) + ([] if after is None else [after])
    in_specs = [a_spec, b_spec] + ([] if res is None else [o_spec]) + ([] if after is None else [_ANY])
    return pl.pallas_call(
        body, name=name, grid=(m // tm, n // tn, nk),
        out_shape=jax.ShapeDtypeStruct((m, n), out_dtype),
        in_specs=in_specs, out_specs=o_spec,
        scratch_shapes=[pltpu.VMEM((tm, tn), F32)] if nk > 1 else [],
        compiler_params=_params("parallel", "parallel", "arbitrary"),
    )(*operands)


def _rms_inv(x):
    return lax.rsqrt(jnp.mean(x * x, axis=-1, keepdims=True) + RMS_EPS)


def _rms_fwd(x, g, name, after=None):
    t, d = x.shape
    tm = _pick(t, (256, 128))

    def body(x_ref, g_ref, *rest):
        xv = x_ref[...]
        rest[-1][...] = (xv * _rms_inv(xv) * g_ref[...]).astype(BF16)

    return pl.pallas_call(
        body, name=name, grid=(t // tm,), out_shape=jax.ShapeDtypeStruct((t, d), BF16),
        in_specs=[_rb(tm, d), _vec(d)] + ([] if after is None else [_ANY]), out_specs=_rb(tm, d),
        compiler_params=_params("parallel"))(x, g, *([] if after is None else [after]))


def _rms_bwd_rows(dy, xv, g):
    inv = _rms_inv(xv)
    xhat = xv * inv
    dxhat = dy * g
    dx = inv * (dxhat - xhat * jnp.mean(dxhat * xhat, axis=-1, keepdims=True))
    return dx, dy * xhat


def _rms_bwd(dy, x, g, res, name):
    t, d = x.shape
    tm = _pick(t, (128,))

    def body(dy_ref, x_ref, g_ref, r_ref, dx_ref, dxb_ref, dg_ref):
        dx, dgrow = _rms_bwd_rows(dy_ref[...].astype(F32), x_ref[...], g_ref[...])
        dx = r_ref[...] + dx
        dx_ref[...] = dx
        dxb_ref[...] = dx.astype(BF16)

        @pl.when(pl.program_id(0) == 0)
        def _():
            dg_ref[...] = jnp.zeros_like(dg_ref)

        dg_ref[...] += jnp.sum(dgrow, axis=0, keepdims=True)

    return pl.pallas_call(
        body, name=name, grid=(t // tm,),
        out_shape=(jax.ShapeDtypeStruct((t, d), F32), jax.ShapeDtypeStruct((t, d), BF16), jax.ShapeDtypeStruct((1, d), F32)),
        in_specs=[_rb(tm, d), _rb(tm, d), _vec(d), _rb(tm, d)],
        out_specs=(_rb(tm, d), _rb(tm, d), _vec(d)), compiler_params=_params("arbitrary"))(dy, x, g, res)


def _latent_norm(z_all, g_q, g_kv, q_off, kv_off, name):
    t = z_all.shape[0]
    tm = _pick(t, (256, 128))

    def body(qa_ref, kva_ref, gq_ref, gkv_ref, qn_ref, kvn_ref):
        qa = qa_ref[...].astype(F32)
        qn_ref[...] = (qa * _rms_inv(qa) * gq_ref[...]).astype(BF16)
        kva = kva_ref[...].astype(F32)
        kvn_ref[...] = (kva * _rms_inv(kva) * gkv_ref[...]).astype(BF16)

    return pl.pallas_call(
        body, name=name, grid=(t // tm,),
        out_shape=(jax.ShapeDtypeStruct((t, Q_LORA), BF16), jax.ShapeDtypeStruct((t, KV_LORA), BF16)),
        in_specs=[_rb(tm, Q_LORA, q_off // Q_LORA), _rb(tm, KV_LORA, kv_off // KV_LORA), _vec(Q_LORA), _vec(KV_LORA)],
        out_specs=(_rb(tm, Q_LORA), _rb(tm, KV_LORA)), compiler_params=_params("parallel"))(z_all, z_all, g_q, g_kv)


def _latent_norm_bwd(dqn, dkvn, z_all, g_q, g_kv, q_off, kv_off, name):
    t = z_all.shape[0]
    tm = _pick(t, (256, 128))

    def body(dqn_ref, dkvn_ref, qa_ref, kva_ref, gq_ref, gkv_ref, dqa_ref, dkva_ref, dgq_ref, dgkv_ref):
        dqa, dgq = _rms_bwd_rows(dqn_ref[...].astype(F32), qa_ref[...].astype(F32), gq_ref[...])
        dkva, dgkv = _rms_bwd_rows(dkvn_ref[...].astype(F32), kva_ref[...].astype(F32), gkv_ref[...])
        dqa_ref[...] = dqa.astype(BF16)
        dkva_ref[...] = dkva.astype(BF16)

        @pl.when(pl.program_id(0) == 0)
        def _():
            dgq_ref[...] = jnp.zeros_like(dgq_ref)
            dgkv_ref[...] = jnp.zeros_like(dgkv_ref)

        dgq_ref[...] += jnp.sum(dgq, axis=0, keepdims=True)
        dgkv_ref[...] += jnp.sum(dgkv, axis=0, keepdims=True)

    return pl.pallas_call(
        body, name=name, grid=(t // tm,),
        out_shape=(jax.ShapeDtypeStruct((t, Q_LORA), BF16), jax.ShapeDtypeStruct((t, KV_LORA), BF16),
                   jax.ShapeDtypeStruct((1, Q_LORA), F32), jax.ShapeDtypeStruct((1, KV_LORA), F32)),
        in_specs=[_rb(tm, Q_LORA), _rb(tm, KV_LORA), _rb(tm, Q_LORA, q_off // Q_LORA), _rb(tm, KV_LORA, kv_off // KV_LORA),
                  _vec(Q_LORA), _vec(KV_LORA)],
        out_specs=(_rb(tm, Q_LORA), _rb(tm, KV_LORA), _vec(Q_LORA), _vec(KV_LORA)),
        compiler_params=_params("arbitrary"))(dqn, dkvn, z_all, z_all, g_q, g_kv)


def _rot(xv, cos_k, sin_a, sin_b, sign):
    return xv * cos_k + sign * (pltpu.roll(xv, LANE - 32, 1) * sin_a + pltpu.roll(xv, 32, 1) * sin_b)


def _rope_q(q_raw, tabs, n_heads, sign, out_dtype, name):
    t, w = q_raw.shape
    tm = _pick(t, (256, 128))

    def body(q_ref, cos_ref, sa_ref, sb_ref, o_ref):
        cos_k, sin_a, sin_b = cos_ref[...], sa_ref[...], sb_ref[...]
        for h in range(n_heads):
            lo = h * HEAD_PAD
            o_ref[:, lo:lo + LANE] = q_ref[:, lo:lo + LANE].astype(out_dtype)
            o_ref[:, lo + LANE:lo + HEAD_PAD] = _rot(
                q_ref[:, lo + LANE:lo + HEAD_PAD].astype(F32), cos_k, sin_a, sin_b, sign).astype(out_dtype)

    return pl.pallas_call(
        body, name=name, grid=(t // tm,), out_shape=jax.ShapeDtypeStruct((t, w), out_dtype),
        in_specs=[_rb(tm, w), _rb(tm, LANE), _rb(tm, LANE), _rb(tm, LANE)],
        out_specs=_rb(tm, w), compiler_params=_params("parallel"))(q_raw, *tabs)


def _rope_k(kv, z_all, tabs, n_heads, kr_off, name):
    t = kv.shape[0]
    tm = _pick(t, (256, 128))
    wk = n_heads * QK_NOPE

    def body(kn_ref, kr_ref, cos_ref, sa_ref, sb_ref, o_ref):
        krot = _rot(kr_ref[...].astype(F32), cos_ref[...], sa_ref[...], sb_ref[...], 1.0).astype(BF16)
        for h in range(n_heads):
            o_ref[:, h * HEAD_PAD:h * HEAD_PAD + LANE] = kn_ref[:, h * QK_NOPE:(h + 1) * QK_NOPE]
            o_ref[:, h * HEAD_PAD + LANE:(h + 1) * HEAD_PAD] = krot

    return pl.pallas_call(
        body, name=name, grid=(t // tm,), out_shape=jax.ShapeDtypeStruct((t, n_heads * HEAD_PAD), BF16),
        in_specs=[_rb(tm, wk), _rb(tm, LANE, kr_off // LANE), _rb(tm, LANE), _rb(tm, LANE), _rb(tm, LANE)],
        out_specs=_rb(tm, n_heads * HEAD_PAD), compiler_params=_params("parallel"))(kv, z_all, *tabs)


def _rope_k_bwd(dk_pad, dv, tabs, n_heads, name):
    t = dk_pad.shape[0]
    tm = _pick(t, (256, 128))
    wk = n_heads * QK_NOPE

    def body(dk_ref, dv_ref, cos_ref, sa_ref, sb_ref, dkv_ref, dkr_ref):
        acc = dk_ref[:, LANE:HEAD_PAD]
        dkv_ref[:, 0:QK_NOPE] = dk_ref[:, 0:LANE].astype(BF16)
        for h in range(1, n_heads):
            acc = acc + dk_ref[:, h * HEAD_PAD + LANE:(h + 1) * HEAD_PAD]
            dkv_ref[:, h * QK_NOPE:(h + 1) * QK_NOPE] = dk_ref[:, h * HEAD_PAD:h * HEAD_PAD + LANE].astype(BF16)
        dkv_ref[:, wk:] = dv_ref[...].astype(BF16)
        dkr_ref[...] = _rot(acc, cos_ref[...], sa_ref[...], sb_ref[...], -1.0).astype(BF16)

    return pl.pallas_call(
        body, name=name, grid=(t // tm,),
        out_shape=(jax.ShapeDtypeStruct((t, 2 * wk), BF16), jax.ShapeDtypeStruct((t, LANE), BF16)),
        in_specs=[_rb(tm, n_heads * HEAD_PAD), _rb(tm, wk), _rb(tm, LANE), _rb(tm, LANE), _rb(tm, LANE)],
        out_specs=(_rb(tm, 2 * wk), _rb(tm, LANE)), compiler_params=_params("parallel"))(dk_pad, dv, *tabs)


NT_DIMS = (((1,), (1,)), ((), ()))
TN_DIMS = (((0,), (0,)), ((), ()))


def _softmax_rows(q, k):
    s = lax.dot_general(q, k, NT_DIMS, preferred_element_type=F32) * SOFTMAX_SCALE
    e = jnp.exp(s - jnp.max(s, axis=-1, keepdims=True))
    return e * (1.0 / jnp.sum(e, axis=-1, keepdims=True))


def _attn_fwd(q_pad, k_pad, kv, n_heads, name):
    t = q_pad.shape[0]
    tq = _pick(t, (256, 128))

    def body(q_ref, k_ref, v_ref, o_ref):
        p = _softmax_rows(q_ref[...], k_ref[...]).astype(BF16)
        o_ref[...] = jnp.dot(p, v_ref[...], preferred_element_type=F32).astype(BF16)

    return pl.pallas_call(
        body, name=name, grid=(n_heads, t // tq),
        out_shape=jax.ShapeDtypeStruct((t, n_heads * V_HEAD), BF16),
        in_specs=[pl.BlockSpec((tq, HEAD_PAD), lambda h, i: (i, h)),
                  pl.BlockSpec((t, HEAD_PAD), lambda h, i: (0, h)),
                  pl.BlockSpec((t, V_HEAD), lambda h, i: (0, n_heads + h))],
        out_specs=pl.BlockSpec((tq, V_HEAD), lambda h, i: (i, h)),
        compiler_params=_params("parallel", "parallel"))(q_pad, k_pad, kv)


def _attn_bwd(q_pad, k_pad, kv, do, n_heads, name):
    t = q_pad.shape[0]
    tq = _pick(t, (256, 128))
    nq = t // tq

    def body(q_ref, k_ref, v_ref, do_ref, dq_ref, dk_ref, dv_ref):
        @pl.when(pl.program_id(1) == 0)
        def _():
            dk_ref[...] = jnp.zeros_like(dk_ref)
            dv_ref[...] = jnp.zeros_like(dv_ref)

        q, k, dout = q_ref[...], k_ref[...], do_ref[...]
        p = _softmax_rows(q, k)
        dp = lax.dot_general(dout, v_ref[...], NT_DIMS, preferred_element_type=F32)
        ds = (p * (dp - jnp.sum(p * dp, axis=-1, keepdims=True)) * SOFTMAX_SCALE).astype(BF16)
        dq_ref[...] = jnp.dot(ds, k, preferred_element_type=F32)
        dk_ref[...] += lax.dot_general(ds, q, TN_DIMS, preferred_element_type=F32)
        dv_ref[...] += lax.dot_general(p.astype(BF16), dout, TN_DIMS, preferred_element_type=F32)

    return pl.pallas_call(
        body, name=name, grid=(n_heads, nq),
        out_shape=(jax.ShapeDtypeStruct((t, n_heads * HEAD_PAD), F32), jax.ShapeDtypeStruct((t, n_heads * HEAD_PAD), F32),
                   jax.ShapeDtypeStruct((t, n_heads * V_HEAD), F32)),
        in_specs=[pl.BlockSpec((tq, HEAD_PAD), lambda h, i: (i, h)),
                  pl.BlockSpec((t, HEAD_PAD), lambda h, i: (0, h)),
                  pl.BlockSpec((t, V_HEAD), lambda h, i: (0, n_heads + h)),
                  pl.BlockSpec((tq, V_HEAD), lambda h, i: (i, h))],
        out_specs=(pl.BlockSpec((tq, HEAD_PAD), lambda h, i: (i, h)),
                   pl.BlockSpec((t, HEAD_PAD), lambda h, i: (0, h)),
                   pl.BlockSpec((t, V_HEAD), lambda h, i: (0, h))),
        compiler_params=_params("parallel", "arbitrary"))(q_pad, k_pad, kv, do)


def _shift_rows(u, t):
    row = lax.broadcasted_iota(jnp.int32, u.shape, 0)
    prev = jnp.where(row == 0, 0.0, pltpu.roll(u, 1, 0))
    nxt = jnp.where(row == t - 1, 0.0, pltpu.roll(u, t - 1, 0))
    return prev, nxt


def _conv_fwd(z_all, conv_w, cc, name):
    t = z_all.shape[0]
    nb = cc // LANE

    def body(cb_ref, cc_ref, ch_ref, w_ref, y_ref):
        u = cc_ref[...].astype(F32) * ch_ref[...].astype(F32)
        prev, nxt = _shift_rows(u, t)
        w = w_ref[...]
        conv = prev * w[0:1, :] + u * w[1:2, :] + nxt * w[2:3, :]
        y_ref[...] = (cb_ref[...].astype(F32) * conv).astype(BF16)

    col = lambda g: pl.BlockSpec((t, LANE), lambda j: (0, g * nb + j))
    return pl.pallas_call(
        body, name=name, grid=(nb,), out_shape=jax.ShapeDtypeStruct((t, cc), BF16),
        in_specs=[col(0), col(1), col(2), pl.BlockSpec((3, LANE), lambda j: (0, j))],
        out_specs=pl.BlockSpec((t, LANE), lambda j: (0, j)),
        compiler_params=_params("parallel"))(z_all, z_all, z_all, conv_w)


def _conv_bwd(dy, z_all, conv_w, cc, name):
    t = z_all.shape[0]
    nb = cc // LANE

    def body(dy_ref, cb_ref, cc_ref, ch_ref, w_ref, dcb_ref, dcc_ref, dch_ref, dw_ref):
        c_c, c_h = cc_ref[...].astype(F32), ch_ref[...].astype(F32)
        u = c_c * c_h
        prev, nxt = _shift_rows(u, t)
        w = w_ref[...]
        dyv = dy_ref[...].astype(F32)
        dcb_ref[...] = (dyv * (prev * w[0:1, :] + u * w[1:2, :] + nxt * w[2:3, :])).astype(BF16)
        dconv = dyv * cb_ref[...].astype(F32)
        dw_ref[0:1, :] = jnp.sum(dconv * prev, axis=0, keepdims=True)
        dw_ref[1:2, :] = jnp.sum(dconv * u, axis=0, keepdims=True)
        dw_ref[2:3, :] = jnp.sum(dconv * nxt, axis=0, keepdims=True)
        dprev, dnxt = _shift_rows(dconv, t)
        du = dnxt * w[0:1, :] + dconv * w[1:2, :] + dprev * w[2:3, :]
        dcc_ref[...] = (du * c_h).astype(BF16)
        dch_ref[...] = (du * c_c).astype(BF16)

    col = lambda g: pl.BlockSpec((t, LANE), lambda j: (0, g * nb + j))
    one = pl.BlockSpec((t, LANE), lambda j: (0, j))
    wsp = pl.BlockSpec((3, LANE), lambda j: (0, j))
    act = jax.ShapeDtypeStruct((t, cc), BF16)
    return pl.pallas_call(
        body, name=name, grid=(nb,),
        out_shape=(act, act, act, jax.ShapeDtypeStruct((3, cc), F32)),
        in_specs=[one, col(0), col(1), col(2), wsp],
        out_specs=(one, one, one, wsp),
        compiler_params=_params("parallel"))(dy, z_all, z_all, z_all, conv_w)


def _sigmoid(v):
    return 1.0 / (1.0 + jnp.exp(-v))


def _merge_fwd(z_all, b_gate, y_a, y_b, gate_off, name):
    t, d = y_a.shape
    tm = _pick(t, (128,))
    gb = gate_off // d

    def body(za_ref, zb_ref, ba_ref, bb_ref, ya_ref, yb_ref, m_ref):
        ga = _sigmoid(za_ref[...].astype(F32) + ba_ref[...])
        gbv = _sigmoid(zb_ref[...].astype(F32) + bb_ref[...])
        m_ref[...] = (ga * ya_ref[...].astype(F32) + gbv * yb_ref[...].astype(F32)).astype(BF16)

    return pl.pallas_call(
        body, name=name, grid=(t // tm,), out_shape=jax.ShapeDtypeStruct((t, d), BF16),
        in_specs=[_rb(tm, d, gb), _rb(tm, d, gb + 1), _vec(d, 0), _vec(d, 1), _rb(tm, d), _rb(tm, d)],
        out_specs=_rb(tm, d), compiler_params=_params("parallel"))(z_all, z_all, b_gate, b_gate, y_a, y_b)


def _merge_bwd(dm, z_all, b_gate, y_a, y_b, gate_off, name):
    t, d = y_a.shape
    tm = _pick(t, (128,))
    gb = gate_off // d

    def body(dm_ref, za_ref, zb_ref, ba_ref, bb_ref, ya_ref, yb_ref, dya_ref, dyb_ref, dzg_ref, db_ref):
        dmv = dm_ref[...].astype(F32)
        ga = _sigmoid(za_ref[...].astype(F32) + ba_ref[...])
        gbv = _sigmoid(zb_ref[...].astype(F32) + bb_ref[...])
        dya_ref[...] = (dmv * ga).astype(BF16)
        dyb_ref[...] = (dmv * gbv).astype(BF16)
        dza = dmv * ya_ref[...].astype(F32) * (ga * (1.0 - ga))
        dzb = dmv * yb_ref[...].astype(F32) * (gbv * (1.0 - gbv))
        dzg_ref[:, 0:d] = dza.astype(BF16)
        dzg_ref[:, d:2 * d] = dzb.astype(BF16)

        @pl.when(pl.program_id(0) == 0)
        def _():
            db_ref[...] = jnp.zeros_like(db_ref)

        db_ref[:, 0:d] += jnp.sum(dza, axis=0, keepdims=True)
        db_ref[:, d:2 * d] += jnp.sum(dzb, axis=0, keepdims=True)

    act = jax.ShapeDtypeStruct((t, d), BF16)
    return pl.pallas_call(
        body, name=name, grid=(t // tm,),
        out_shape=(act, act, jax.ShapeDtypeStruct((t, 2 * d), BF16), jax.ShapeDtypeStruct((1, 2 * d), F32)),
        in_specs=[_rb(tm, d), _rb(tm, d, gb), _rb(tm, d, gb + 1), _vec(d, 0), _vec(d, 1), _rb(tm, d), _rb(tm, d)],
        out_specs=(_rb(tm, d), _rb(tm, d), _rb(tm, 2 * d), _vec(2 * d)),
        compiler_params=_params("arbitrary"))(dm, z_all, z_all, b_gate, b_gate, y_a, y_b)


def _swiglu_fwd(gate, up, name):
    t, f = gate.shape
    tm = _pick(t, (128,))

    def body(g_ref, u_ref, a_ref):
        g = g_ref[...].astype(F32)
        a_ref[...] = (g * _sigmoid(g) * u_ref[...].astype(F32)).astype(BF16)

    return pl.pallas_call(
        body, name=name, grid=(t // tm,), out_shape=jax.ShapeDtypeStruct((t, f), BF16),
        in_specs=[_rb(tm, f), _rb(tm, f)], out_specs=_rb(tm, f), compiler_params=_params("parallel"))(gate, up)


def _swiglu_bwd(dact, gate, up, name):
    t, f = gate.shape
    tm = _pick(t, (128,))

    def body(da_ref, g_ref, u_ref, dg_ref, du_ref):
        g, da = g_ref[...].astype(F32), da_ref[...].astype(F32)
        sg = _sigmoid(g)
        dg_ref[...] = (da * u_ref[...].astype(F32) * (sg * (1.0 + g * (1.0 - sg)))).astype(BF16)
        du_ref[...] = (da * (g * sg)).astype(BF16)

    act = jax.ShapeDtypeStruct((t, f), BF16)
    return pl.pallas_call(
        body, name=name, grid=(t // tm,), out_shape=(act, act),
        in_specs=[_rb(tm, f)] * 3, out_specs=(_rb(tm, f), _rb(tm, f)), compiler_params=_params("parallel"))(dact, gate, up)


def _loss_head(x2, target, g, name):
    t, d = x2.shape
    tm = _pick(t, (128,))

    def body(x_ref, t_ref, g_ref, loss_ref, dx_ref, dxb_ref, dg_ref):
        xv, gv = x_ref[...], g_ref[...]
        err = xv * _rms_inv(xv) * gv - t_ref[...]
        dx, dgrow = _rms_bwd_rows(err * (1.0 / d), xv, gv)
        dx_ref[...] = dx
        dxb_ref[...] = dx.astype(BF16)

        @pl.when(pl.program_id(0) == 0)
        def _():
            loss_ref[...] = jnp.zeros_like(loss_ref)
            dg_ref[...] = jnp.zeros_like(dg_ref)

        loss_ref[...] += (0.5 / d) * jnp.sum(jnp.sum(err * err, axis=1, keepdims=True), axis=0, keepdims=True)
        dg_ref[...] += jnp.sum(dgrow, axis=0, keepdims=True)

    return pl.pallas_call(
        body, name=name, grid=(t // tm,),
        out_shape=(jax.ShapeDtypeStruct((1, 1), F32), jax.ShapeDtypeStruct((t, d), F32),
                   jax.ShapeDtypeStruct((t, d), BF16), jax.ShapeDtypeStruct((1, d), F32)),
        in_specs=[_rb(tm, d), _rb(tm, d), _vec(d)],
        out_specs=(pl.BlockSpec((1, 1), lambda i: (0, 0)), _rb(tm, d), _rb(tm, d), _vec(d)),
        compiler_params=_params("arbitrary"))(x2, target, g)


def _adamw(parts, w, m, v, name):
    r, c = w.shape
    cp = parts.shape[2]
    assert parts.shape[1] >= r and cp >= c
    tr = r if r * c <= ADAM_BLOCK_ELEMS else _pick(r, tuple(s for s in (512, 256, 128, 64, 32, 16, 8) if s * c <= ADAM_BLOCK_ELEMS))
    assert tr == parts.shape[1] or tr % 8 == 0

    def body(p_ref, w_ref, m_ref, v_ref, g_ref, d_ref, nm_ref, nv_ref):
        g = p_ref[0, :, 0:c].astype(F32)
        for s in range(1, N_DEV):
            g = g + p_ref[s, :, 0:c].astype(F32)
        nm = ADAM_B1 * m_ref[...] + (1.0 - ADAM_B1) * g
        nv = ADAM_B2 * v_ref[...] + (1.0 - ADAM_B2) * (g * g)
        m_hat = nm / (1.0 - ADAM_B1 ** ADAM_STEP)
        v_hat = nv / (1.0 - ADAM_B2 ** ADAM_STEP)
        g_ref[...] = g
        d_ref[...] = -ADAM_LR * (m_hat / (jnp.sqrt(v_hat) + ADAM_EPS) + ADAM_WD * w_ref[...])
        nm_ref[...] = nm
        nv_ref[...] = nv

    blk = pl.BlockSpec((tr, c), lambda i: (i, 0))
    out = jax.ShapeDtypeStruct((r, c), F32)
    return pl.pallas_call(
        body, name=name, grid=(r // tr,), out_shape=(out, out, out, out),
        in_specs=[pl.BlockSpec((N_DEV, tr, cp), lambda i: (0, i, 0)), blk, blk, blk],
        out_specs=(blk, blk, blk, blk), compiler_params=_params("parallel"))(parts, w, m, v)


def _cols_of(g):
    return jnp.transpose(g, (1, 0, 2)).reshape(g.shape[1], N_DEV * g.shape[2])


def _col_parts(dw):
    k, n8 = dw.shape
    return jnp.transpose(dw.reshape(k, N_DEV, n8 // N_DEV), (1, 0, 2))


def kernel(x, positions, g_mix, w_in, b_gate, conv_w, g_q_a, w_q_b, g_kv_a, w_kv_b, w_branch, w_out, g_ffn, w_ffn_gate, w_ffn_up, w_ffn_down, g_final, loss_target, m_g_mix, m_w_in, m_b_gate, m_conv_w, m_g_q_a, m_w_q_b, m_g_kv_a, m_w_kv_b, m_w_branch, m_w_out, m_g_ffn, m_w_ffn_gate, m_w_ffn_up, m_w_ffn_down, m_g_final, v_g_mix, v_w_in, v_b_gate, v_conv_w, v_g_q_a, v_w_q_b, v_g_kv_a, v_w_kv_b, v_w_branch, v_w_out, v_g_ffn, v_w_ffn_gate, v_w_ffn_up, v_w_ffn_down, v_g_final):
    given = dict(locals())
    xs = x[0]
    t, d = xs.shape
    cc = d // 2
    n_heads = cc // V_HEAD
    in_cols = N_DEV * w_in.shape[2]
    q_off, kv_off, kr_off = 3 * cc, 3 * cc + Q_LORA, 3 * cc + Q_LORA + KV_LORA
    head_cols = kr_off + QK_ROPE
    head_pad = -(-(kr_off + LANE) // 1024) * 1024
    assert in_cols == head_cols + 2 * d and q_off % Q_LORA == 0 and kv_off % KV_LORA == 0 and kr_off % LANE == 0
    fs = w_ffn_gate.shape[2]
    fsp = -(-fs // LANE) * LANE
    ffp = N_DEV * fsp
    me = _me_index()
    bf = lambda a: a.astype(BF16)
    one_slot = lambda refs, i: refs[i].at[0]

    g_in = _gather_start([bf(w_in[0]), conv_w[0]], me, "ag1_start_in")
    h = _rms_fwd(xs, g_mix, "rms_mix", after=g_in[3])
    f_in = _forward_start(_split_wait(g_in, one_slot, h, "ag1_wait_in"), "ag2_start_in")
    mix_cols = [w_q_b.shape[2], w_kv_b.shape[2], w_branch.shape[3], None]
    zero = g_in[3][0, 0]
    bf_later = lambda a: (a + zero).astype(BF16)
    g_mix_w = _gather_start([bf_later(w_q_b[0]), bf_later(w_kv_b[0]), bf_later(w_branch[0].reshape(2 * cc, -1)), bf_later(w_out[0])],
                            me, "ag1_start_mix", after=f_in[3], cols=mix_cols)
    w_in_g, cw_g = _split_wait(f_in, one_slot, g_mix_w[3], "ag2_wait_in")
    w_in_full = _cols_of(w_in_g)
    w_head = jnp.concatenate([w_in_full[:, :head_cols], jnp.zeros((d, head_pad - head_cols), BF16)], axis=1)
    w_gate = w_in_full[:, head_cols:]
    cw = _cols_of(cw_g)

    inv_freq = ROPE_THETA ** (-jnp.arange(0, QK_ROPE, 2, dtype=F32) / QK_ROPE)
    ang = positions[0].astype(F32)[:, None] * inv_freq[None, :]
    cos, sin = jnp.cos(ang), jnp.sin(ang)
    z32, z64 = jnp.zeros((t, 32), F32), jnp.zeros((t, 64), F32)
    tabs = (jnp.concatenate([cos, cos, jnp.ones((t, 64), F32)], axis=1),
            jnp.concatenate([-sin, z32, z64], axis=1),
            jnp.concatenate([z32, sin, z64], axis=1))

    z_head = _matmul(h, w_head, name="mm_z_head")
    zg = _matmul(h, w_gate, name="mm_z_gate")
    f_mix = _forward_start(_split_wait(g_mix_w, _first_window(mix_cols), zg, "ag1_wait_mix"), "ag2_start_mix", cols=mix_cols)
    ffn_cols = [fsp, fsp, None]
    g_ffn_w = _gather_start([bf_later(jnp.pad(w_ffn_gate[0], ((0, 0), (0, fsp - fs)))), bf_later(jnp.pad(w_ffn_up[0], ((0, 0), (0, fsp - fs)))),
                             bf_later(jnp.pad(w_ffn_down[0], ((0, fsp - fs), (0, 0))))], me, "ag1_start_ffn", after=f_mix[3], cols=ffn_cols)
    y_a = _conv_fwd(z_head, cw, cc, "conv_fwd")
    qn, kvn = _latent_norm(z_head, g_q_a, g_kv_a, q_off, kv_off, "latent_norm")
    wq_full, wkv_full, wbr, wo_g = _split_wait(f_mix, _first_window(mix_cols), g_ffn_w[3], "ag2_wait_mix")
    wq = wq_full.reshape(Q_LORA, n_heads, QK_NOPE + QK_ROPE)
    wq_pad = jnp.pad(wq, ((0, 0), (0, 0), (0, HEAD_PAD - QK_NOPE - QK_ROPE))).reshape(Q_LORA, n_heads * HEAD_PAD)
    wkv = wkv_full.reshape(KV_LORA, n_heads, 2, QK_NOPE)
    wkv_perm = jnp.transpose(wkv, (0, 2, 1, 3)).reshape(KV_LORA, 2 * n_heads * QK_NOPE)
    wb_a, wb_b = wbr[:cc], wbr[cc:]
    wo = wo_g.reshape(d, d)
    q_pad = _rope_q(_matmul(qn, wq_pad, name="mm_q", out_dtype=F32), tabs, n_heads, 1.0, BF16, "rope_q")
    kv = _matmul(kvn, wkv_perm, name="mm_kv")
    k_pad = _rope_k(kv, z_head, tabs, n_heads, kr_off, "rope_k")
    y_b = _attn_fwd(q_pad, k_pad, kv, n_heads, "attn_fwd")
    f_ffn = _forward_start(_split_wait(g_ffn_w, _first_window(ffn_cols), y_b, "ag1_wait_ffn"), "ag2_start_ffn", cols=ffn_cols)
    ybr_a = _matmul(y_a, wb_a, name="mm_br_a", after=f_ffn[3])
    ybr_b = _matmul(y_b, wb_b, name="mm_br_b")
    merged = _merge_fwd(zg, b_gate, ybr_a, ybr_b, 0, "merge_fwd")
    x1 = _matmul(merged, wo, name="mm_out", res=xs, out_dtype=F32)
    h2 = _rms_fwd(x1, g_ffn, "rms_ffn")
    wg, wu, wd_g = _split_wait(f_ffn, _first_window(ffn_cols), h2, "ag2_wait_ffn")
    wd = wd_g.reshape(ffp, d)
    gate = _matmul(h2, wg, name="mm_gate")
    up = _matmul(h2, wu, name="mm_up")
    act = _swiglu_fwd(gate, up, "swiglu_fwd")
    x2 = _matmul(act, wd, name="mm_down", res=x1, out_dtype=F32)
    loss_part, dx2, dx2b, dg_final = _loss_head(x2, loss_target[0], g_final.reshape(1, d), "loss_head")

    dact = _matmul(dx2b, wd, tb=True, name="mm_d_act")
    dwd = _matmul(act, dx2b, ta=True, name="mm_dw_down")
    r_down = _exchange_start([dwd.reshape(N_DEV, fsp, d)], me, "rs_start_down")
    dgate, dup = _swiglu_bwd(dact, gate, up, "swiglu_bwd")
    dwg = _matmul(h2, dgate, ta=True, name="mm_dw_gate", after=r_down[3])
    dwu = _matmul(h2, dup, ta=True, name="mm_dw_up")
    r_gate_up = _exchange_start([dwg, dwu], me, "rs_start_gate_up", cols=[fsp, fsp])
    dh2 = _matmul(dgate, wg, tb=True, name="mm_d_h2_gate", out_dtype=F32, after=r_gate_up[3])
    dh2 = _matmul(dup, wu, tb=True, name="mm_d_h2_up", res=dh2, out_dtype=F32)
    dx1, dx1b, dg_ffn = _rms_bwd(dh2, x1, g_ffn, dx2, "rms_ffn_bwd")
    dwo = _matmul(merged, dx1b, ta=True, name="mm_dw_out")
    r_out = _exchange_start([dwo.reshape(N_DEV, d // N_DEV, d)], me, "rs_start_out")
    dmerged = _matmul(dx1b, wo, tb=True, name="mm_d_merged", after=r_out[3])
    dybr_a, dybr_b, dzg, db_gate = _merge_bwd(dmerged, zg, b_gate, ybr_a, ybr_b, 0, "merge_bwd")
    dwb_a = _matmul(y_a, dybr_a, ta=True, name="mm_dw_br_a")
    dwb_b = _matmul(y_b, dybr_b, ta=True, name="mm_dw_br_b")
    r_br = _exchange_start([dwb_a, dwb_b], me, "rs_start_branch", cols=[d // N_DEV] * 2)
    dy_a = _matmul(dybr_a, wb_a, tb=True, name="mm_d_y_a", after=r_br[3])
    dy_b = _matmul(dybr_b, wb_b, tb=True, name="mm_d_y_b")
    dw_gate = _matmul(h, dzg, ta=True, name="mm_dw_in_gate")
    dh_gate = _matmul(dzg, w_gate, tb=True, name="mm_d_h_gate", out_dtype=F32)
    dq_pad, dk_pad, dv = _attn_bwd(q_pad, k_pad, kv, dy_b, n_heads, "attn_bwd")
    dq_raw = _rope_q(dq_pad, tabs, n_heads, -1.0, BF16, "rope_q_bwd")
    dkv, dkr = _rope_k_bwd(dk_pad, dv, tabs, n_heads, "rope_k_bwd")
    dwq = _matmul(qn, dq_raw, ta=True, name="mm_dw_q")
    dwkv = _matmul(kvn, dkv, ta=True, name="mm_dw_kv")
    dwq_full = dwq.reshape(Q_LORA, n_heads, HEAD_PAD)[:, :, :QK_NOPE + QK_ROPE].reshape(Q_LORA, -1)
    dwkv_full = jnp.transpose(dwkv.reshape(KV_LORA, 2, n_heads, QK_NOPE), (0, 2, 1, 3)).reshape(KV_LORA, -1)
    r_qkv = _exchange_start([_col_parts(dwq_full), _col_parts(dwkv_full)], me, "rs_start_q_kv")
    dqn = _matmul(dq_raw, wq_pad, tb=True, name="mm_d_qn", after=r_qkv[3])
    dkvn = _matmul(dkv, wkv_perm, tb=True, name="mm_d_kvn")
    dqa, dkva, dg_q, dg_kv = _latent_norm_bwd(dqn, dkvn, z_head, g_q_a, g_kv_a, q_off, kv_off, "latent_norm_bwd")
    dcb, dcc, dch, dcw = _conv_bwd(dy_a, z_head, cw, cc, "conv_bwd")
    dz_head = jnp.concatenate([dcb, dcc, dch, dqa, dkva, dkr, jnp.zeros((t, head_pad - kr_off - LANE), BF16)], axis=1)
    dw_head = _matmul(h, dz_head, ta=True, name="mm_dw_in_head")
    r_in = _exchange_start([_col_parts(jnp.concatenate([dw_head[:, :head_cols], dw_gate], axis=1))], me, "rs_start_in")
    dh = _matmul(dz_head, w_head, tb=True, name="mm_d_h_head", res=dh_gate, out_dtype=F32, after=r_in[3])
    dx, _, dg_mix = _rms_bwd(dh, xs, g_mix, dx1, "rms_mix_bwd")

    results = {}
    last = dx

    def update(started, targets):
        nonlocal last
        n = len(targets)
        recvs = _split_wait(started, _land_block(n), last, "rs_wait_" + targets[0][0])[n:]
        for recv, (key, w_, m_, v_) in zip(recvs, targets):
            results[key] = _adamw(recv, w_, m_, v_, "adamw_" + key)
            last = results[key][0]

    shard = lambda wname: (wname,) + tuple(given[p + wname][0] for p in ("", "m_", "v_"))
    branch = lambda b: ("w_branch_%d" % b,) + tuple(given[p + "w_branch"][0, b] for p in ("", "m_", "v_"))
    update(r_down, [shard("w_ffn_down")])
    update(r_gate_up, [shard("w_ffn_gate"), shard("w_ffn_up")])
    update(r_out, [shard("w_out")])
    update(r_br, [branch(0), branch(1)])
    results["w_branch"] = tuple(jnp.stack([results["w_branch_0"][k], results["w_branch_1"][k]]) for k in range(4))
    update(r_qkv, [shard("w_q_b"), shard("w_kv_b")])

    small = [("g_mix", dg_mix), ("b_gate", db_gate), ("g_q_a", dg_q), ("g_kv_a", dg_kv), ("g_ffn", dg_ffn),
             ("g_final", dg_final), ("conv_w", dcw.reshape(1, 3 * cc))]
    packed = _all_gather(jnp.concatenate([p for _, p in small], axis=1), "ag_small_grads", last)
    off = 0
    for wname, p in small:
        n = p.shape[1]
        parts = packed[:, :, off:off + n]
        off += n
        if wname == "conv_w":
            width = conv_w.shape[2]
            parts = lax.dynamic_slice_in_dim(parts.reshape(N_DEV, 3, cc), me * width, width, axis=2)
            flat = (3, width)
        else:
            flat = (1, n)
        results[wname] = _adamw(parts, given[wname].reshape(flat), given["m_" + wname].reshape(flat),
                                given["v_" + wname].reshape(flat), "adamw_" + wname)
        last = results[wname][0]
    update(r_in, [shard("w_in")])

    loss = lax.psum(loss_part[0, 0], MESH_AXES)
    order = ["g_mix", "w_in", "b_gate", "conv_w", "g_q_a", "w_q_b", "g_kv_a", "w_kv_b", "w_branch", "w_out", "g_ffn",
             "w_ffn_gate", "w_ffn_up", "w_ffn_down", "g_final"]
    out = [loss, dx[None]]
    for k in range(4):
        out += [results[n][k].reshape(given[n].shape) for n in order]
    return tuple(out)
```

```python
import functools
import math

import jax
import jax.numpy as jnp
from jax import lax
from jax.experimental import pallas as pl
from jax.experimental.pallas import tpu as pltpu

F32 = jnp.float32
BF16 = jnp.bfloat16
N_DEV = 8
MESH_AXES = ("x", "y", "c")
MESH = pl.DeviceIdType.MESH

QK_NOPE = 128
QK_ROPE = 64
V_HEAD = 128
HEAD_PAD = 256
Q_LORA = 1024
KV_LORA = 512
ROPE_THETA = 10000.0
RMS_EPS = 1e-6
SOFTMAX_SCALE = 1.0 / math.sqrt(QK_NOPE + QK_ROPE)
ADAM_LR, ADAM_B1, ADAM_B2, ADAM_EPS, ADAM_WD, ADAM_STEP = 0.001, 0.9, 0.999, 1e-08, 0.01, 10

VMEM_LIMIT = 60 * 1024 * 1024
LANE = 128
ADAM_BLOCK_ELEMS = 1 << 18
COPY_BLOCK_BYTES = 4 * 1024 * 1024
MATMUL_K_TILES =(4096, 2816, 2048, 1024, 512, 256, 128)
MATMUL_VMEM_BUDGET = 48 * 1024 * 1024


def _pick(n, cands=(1024, 512, 256, 128)):
    for c in cands:
        if n % c == 0:
            return c
    return n


def _params(*sem):
    return pltpu.CompilerParams(dimension_semantics=sem, vmem_limit_bytes=VMEM_LIMIT)


def _rb(tm, c, cb=0):
    return pl.BlockSpec((tm, c), lambda i: (i, cb))


def _vec(c, cb=0):
    return pl.BlockSpec((1, c), lambda i: (0, cb))


def _all_gather(x, name, after):
    def body(x_ref, after_ref, out_ref, send_sems, recv_sems, local_sem):
        x_, y_, c_ = lax.axis_index("x"), lax.axis_index("y"), lax.axis_index("c")
        me, sibling = (x_, y_, c_), (x_, y_, 1 - c_)
        chips = [(1 - x_, y_), (x_, 1 - y_), (1 - x_, 1 - y_)]

        def slot(px, py, pc):
            return out_ref.at[4 * px + 2 * py + pc]

        def copy(k, block, to, src=None):
            return pltpu.make_async_remote_copy(
                src_ref=slot(*block) if src is None else src, dst_ref=slot(*block),
                send_sem=send_sems.at[k], recv_sem=recv_sems.at[k], device_id=to, device_id_type=MESH)

        mine = pltpu.make_async_copy(x_ref, slot(*me), local_sem)
        mine.start()
        first = [copy(0, me, sibling, src=x_ref)]
        first += [copy(1 + j, me, (*chip, c_), src=x_ref) for j, chip in enumerate(chips)]
        for cp in first:
            cp.start()
        passed = [copy(4 + j, (*chip, c_), sibling) for j, chip in enumerate(chips)]
        for j, chip in enumerate(chips):
            copy(1 + j, (*chip, c_), me).wait_recv()
            passed[j].start()
        copy(0, sibling, me).wait_recv()
        for j, chip in enumerate(chips):
            copy(4 + j, (*chip, 1 - c_), me).wait_recv()
        for cp in first + passed:
            cp.wait_send()
        mine.wait()

    return pl.pallas_call(
        body, name=name,
        out_shape=jax.ShapeDtypeStruct((N_DEV,) + x.shape, x.dtype),
        in_specs=[pl.BlockSpec(memory_space=pl.ANY), pl.BlockSpec(memory_space=pl.ANY)],
        out_specs=pl.BlockSpec(memory_space=pl.ANY),
        scratch_shapes=[pltpu.SemaphoreType.DMA((7,)), pltpu.SemaphoreType.DMA((7,)), pltpu.SemaphoreType.DMA(())],
    )(x, after)


_HBM = pl.BlockSpec(memory_space=pltpu.HBM)
_SEM = pl.BlockSpec(memory_space=pltpu.SEMAPHORE)
_ANY = pl.BlockSpec(memory_space=pl.ANY)
_EFFECT = pltpu.SideEffectType.DATAFLOW_SIDE_EFFECTING


def _me_index():
    return 4 * lax.axis_index("x") + 2 * lax.axis_index("y") + lax.axis_index("c")


def _window(ref, idx, cols):
    if cols is None:
        return ref.at[idx]
    return ref.at[:, pl.ds(idx * cols if isinstance(idx, int) else pl.multiple_of(idx * cols, LANE), cols)]


def _own_block_only(src, me, name, cols=None, src_cols=False):
    k, n = src.shape if src_cols is False else ((src.shape[0], src_cols) if src_cols else src.shape[1:])
    assert cols is None or (n == cols and cols % LANE == 0)
    out_shape = (N_DEV, k, n) if cols is None else (k, N_DEV * cols)
    limit = COPY_BLOCK_BYTES // (n * src.dtype.itemsize)
    tr = k if k <= limit else _pick(k, tuple(s for s in (1024, 512, 256, 128, 64, 32, 16) if s <= limit))

    def spec(layout):
        if layout is False:
            return pl.BlockSpec((tr, n), lambda i, me_ref: (i, 0))
        if layout is None:
            return pl.BlockSpec((None, tr, n), lambda i, me_ref: (me_ref[0], i, 0))
        return pl.BlockSpec((tr, n), lambda i, me_ref: (i, me_ref[0]))

    def body(me_ref, src_ref, dst_ref):
        dst_ref[...] = src_ref[...]

    return pl.pallas_call(
        body, name=name, out_shape=jax.ShapeDtypeStruct(out_shape, src.dtype),
        grid_spec=pltpu.PrefetchScalarGridSpec(num_scalar_prefetch=1, grid=(k // tr,), in_specs=[spec(src_cols)], out_specs=spec(cols)),
        compiler_params=_params("parallel"))(jnp.reshape(me, (1,)).astype(jnp.int32), src)


def _split_start(bufs, sem_shape, issue, name, after=None):
    n = len(bufs)
    rows, per_row = sem_shape
    sem = lambda sems, i, k: sems.at[i * per_row + k]
    first_out = n + (after is not None)

    def body(*refs):
        issue(refs[:n], refs[first_out], refs[first_out + 1], sem)
        refs[-1][...] = jnp.zeros_like(refs[-1])

    sems = pltpu.SemaphoreType.DMA((rows * per_row,))
    operands = [pltpu.with_memory_space_constraint(b, pltpu.HBM) for b in bufs] + ([] if after is None else [after])
    outs = pl.pallas_call(
        body, name=name,
        out_shape=(sems, sems) + tuple(pltpu.HBM(b.shape, b.dtype) for b in bufs) + (jax.ShapeDtypeStruct((8, LANE), F32),),
        in_specs=(_HBM,) * n + (_ANY,) * (after is not None),
        out_specs=(_SEM, _SEM) + (_HBM,) * n + (pl.BlockSpec(memory_space=pltpu.VMEM),),
        input_output_aliases={i: 2 + i for i in range(n)},
        compiler_params=pltpu.CompilerParams(has_side_effects=_EFFECT),
    )(*operands)
    return outs[0], outs[1], list(outs[2:2 + n]), outs[-1], sem_shape


def _split_wait(started, block_of, after, name, bufs=None, recvs=None, sends=None):
    send_sems, recv_sems, start_bufs, _, sem_shape = started
    bufs = start_bufs if bufs is None else bufs
    n = len(bufs)
    recvs = range(sem_shape[1]) if recvs is None else recvs
    sends = range(sem_shape[1]) if sends is None else sends
    after = tuple(after) if isinstance(after, (tuple, list)) else (after,)

    def body(*refs):
        x_, y_, c_ = lax.axis_index("x"), lax.axis_index("y"), lax.axis_index("c")
        for i in range(sem_shape[0]):
            blk = block_of(refs, i)
            for k in range(sem_shape[1]):
                cp = pltpu.make_async_remote_copy(
                    src_ref=blk, dst_ref=blk, send_sem=refs[n].at[i * sem_shape[1] + k], recv_sem=refs[n + 1].at[i * sem_shape[1] + k],
                    device_id=(x_, y_, c_), device_id_type=MESH)
                if k in sends:
                    cp.wait_send()
                if k in recvs:
                    cp.wait_recv()

    outs = pl.pallas_call(
        body, name=name,
        out_shape=tuple(pltpu.HBM(b.shape, b.dtype) for b in bufs),
        in_specs=(_HBM,) * n + (_SEM, _SEM) + (_ANY,) * len(after), out_specs=(_HBM,) * n,
        input_output_aliases={i: i for i in range(n)},
        compiler_params=pltpu.CompilerParams(has_side_effects=_EFFECT),
    )(*bufs, send_sems, recv_sems, *after)
    return list(outs)


def _first_window(cols):
    return lambda refs, i: _window(refs[i], 0, cols[i])


def _gather_prepare(blocks, me, name, cols=None):
    cols = cols or [None] * len(blocks)
    return [_own_block_only(b, me, "%s_own%d" % (name, i), c) for i, (b, c) in enumerate(zip(blocks, cols))]


def _gather_start(bufs, name, after=None, cols=None):
    cols = cols or [None] * len(bufs)

    def issue(buf_refs, send_sems, recv_sems, sem):
        x_, y_, c_ = lax.axis_index("x"), lax.axis_index("y"), lax.axis_index("c")
        me_idx = 4 * x_ + 2 * y_ + c_
        targets = [(x_, y_, 1 - c_), (1 - x_, y_, c_), (x_, 1 - y_, c_), (1 - x_, 1 - y_, c_)]
        for i, buf in enumerate(buf_refs):
            mine = _window(buf, me_idx, cols[i])
            for k, to in enumerate(targets):
                pltpu.make_async_remote_copy(
                    src_ref=mine, dst_ref=mine, send_sem=sem(send_sems, i, k), recv_sem=sem(recv_sems, i, k),
                    device_id=to, device_id_type=MESH).start()

    return _split_start(bufs, (len(bufs), 4), issue, name, after)


def _forward_start(bufs, name, after=None, cols=None, chips=(0, 1, 2)):
    cols = cols or [None] * len(bufs)

    def issue(buf_refs, send_sems, recv_sems, sem):
        x_, y_, c_ = lax.axis_index("x"), lax.axis_index("y"), lax.axis_index("c")
        places = [(1 - x_, y_), (x_, 1 - y_), (1 - x_, 1 - y_)]
        for i, buf in enumerate(buf_refs):
            for k, chip in enumerate(chips):
                px, py = places[chip]
                landed = _window(buf, 4 * px + 2 * py + c_, cols[i])
                pltpu.make_async_remote_copy(
                    src_ref=landed, dst_ref=landed, send_sem=sem(send_sems, i, k), recv_sem=sem(recv_sems, i, k),
                    device_id=(x_, y_, 1 - c_), device_id_type=MESH).start()

    return _split_start(bufs, (len(bufs), len(chips)), issue, name, after)


def _exchange_start(parts, me, name, cols=None):
    n = len(parts)
    cols = cols or [None] * n

    def issue(refs, send_sems, recv_sems, sem):
        x_, y_, c_ = lax.axis_index("x"), lax.axis_index("y"), lax.axis_index("c")
        me_idx = 4 * x_ + 2 * y_ + c_
        for i in range(n):
            for d in range(1, N_DEV):
                px = 1 - x_ if d & 4 else x_
                py = 1 - y_ if d & 2 else y_
                pc = 1 - c_ if d & 1 else c_
                pltpu.make_async_remote_copy(
                    src_ref=_window(refs[i], 4 * px + 2 * py + pc, cols[i]), dst_ref=refs[n + i].at[me_idx],
                    send_sem=sem(send_sems, i, d - 1), recv_sem=sem(recv_sems, i, d - 1),
                    device_id=(px, py, pc), device_id_type=MESH).start()

    lands = [_own_block_only(p, me, "%s_own%d" % (name, i), src_cols=c) for i, (p, c) in enumerate(zip(parts, cols))]
    return _split_start(list(parts) + lands, (n, N_DEV - 1), issue, name)


def _land_block(refs_offset):
    return lambda refs, i: refs[refs_offset + i].at[0]


def _matmul(a, b, *, name, ta=False, tb=False, res=None, out_dtype=BF16, after=None):
    (kdim, m) = a.shape if ta else a.shape[::-1]
    (n, kdim_b) = b.shape if tb else b.shape[::-1]
    assert kdim == kdim_b, (a.shape, b.shape, ta, tb)
    tm, tn = _pick(m), _pick(n)
    out_bytes = jnp.dtype(out_dtype).itemsize

    def vmem_bytes(tk):
        return 2 * (2 * tk * (tm + tn) + tm * tn * (out_bytes + (4 if res is not None else 0))) + (4 * tm * tn if tk < kdim else 0)

    fitting = [c for c in MATMUL_K_TILES if kdim % c == 0 and vmem_bytes(c) <= MATMUL_VMEM_BUDGET]
    tk = fitting[0] if fitting else kdim
    nk = kdim // tk
    a_spec = pl.BlockSpec((tk, tm), lambda i, j, k: (k, i)) if ta else pl.BlockSpec((tm, tk), lambda i, j, k: (i, k))
    b_spec = pl.BlockSpec((tn, tk), lambda i, j, k: (j, k)) if tb else pl.BlockSpec((tk, tn), lambda i, j, k: (k, j))
    o_spec = pl.BlockSpec((tm, tn), lambda i, j, k: (i, j))
    dims = (((0 if ta else 1,), (1 if tb else 0,)), ((), ()))

    def body(*refs):
        a_ref, b_ref = refs[:2]
        r_ref = None if res is None else refs[2]
        def finish(v):
            if r_ref is not None:
                v = r_ref[...] + v
            o_ref[...] = v.astype(out_dtype)

        part = lax.dot_general(a_ref[...], b_ref[...], dims, preferred_element_type=F32)
        if nk == 1:
            o_ref = refs[-1]
            finish(part)
            return
        o_ref, acc = refs[-2:]
        k = pl.program_id(2)

        @pl.when(k == 0)
        def _():
            acc[...] = part

        @pl.when(k > 0)
        def _():
            acc[...] += part

        @pl.when(k == nk - 1)
        def _():
            finish(acc[...])

    operands = [a, b] + ([] if res is None else [res]) + ([] if after is None else [after])
    in_specs = [a_spec, b_spec] + ([] if res is None else [o_spec]) + ([] if after is None else [_ANY])
    return pl.pallas_call(
        body, name=name, grid=(m // tm, n // tn, nk),
        out_shape=jax.ShapeDtypeStruct((m, n), out_dtype),
        in_specs=in_specs, out_specs=o_spec,
        scratch_shapes=[pltpu.VMEM((tm, tn), F32)] if nk > 1 else [],
        compiler_params=_params("parallel", "parallel", "arbitrary"),
    )(*operands)


def _accumulate(acc, part, k, nk, finish):
    @pl.when(k == 0)
    def _():
        acc[...] = part

    @pl.when(k > 0)
    def _():
        acc[...] += part

    @pl.when(k == nk - 1)
    def _():
        finish(acc[...])


def _mm_z_shard(h, w_sh, slot, z_sh, name, after=()):
    t, d = h.shape
    n = w_sh.shape[2]
    tm, tk = _pick(t), _pick(d, (2048, 1024, 512, 256, 128))
    nk = d // tk
    passed = list(after) + ([] if z_sh is None else [z_sh])

    def body(slot_ref, h_ref, w_ref, *rest):
        o_ref, acc = rest[-2:]

        def finish(v):
            o_ref[...] = v.astype(BF16)

        _accumulate(acc, jnp.dot(h_ref[...], w_ref[...], preferred_element_type=F32), pl.program_id(1), nk, finish)

    return pl.pallas_call(
        body, name=name, out_shape=jax.ShapeDtypeStruct((N_DEV, t, n), BF16),
        grid_spec=pltpu.PrefetchScalarGridSpec(
            num_scalar_prefetch=1, grid=(t // tm, nk),
            in_specs=[pl.BlockSpec((tm, tk), lambda i, k, s: (i, k)),
                      pl.BlockSpec((None, tk, n), lambda i, k, s: (s[0], k, 0))] + [_ANY] * len(passed),
            out_specs=pl.BlockSpec((None, tm, n), lambda i, k, s: (s[0], i, 0)),
            scratch_shapes=[pltpu.VMEM((tm, n), F32)]),
        input_output_aliases={} if z_sh is None else {2 + len(passed): 0},
        compiler_params=_params("parallel", "arbitrary"),
    )(jnp.reshape(slot, (1,)).astype(jnp.int32), h, w_sh, *passed)


def _mm_dw_shards(h, dz_sh, name):
    t, d = h.shape
    n = dz_sh.shape[2]
    tm, tk = _pick(d), _pick(t, (2048, 1024, 512, 256, 128))
    nk = t // tk

    def body(h_ref, dz_ref, o_ref, acc):
        def finish(v):
            o_ref[...] = v.astype(BF16)

        _accumulate(acc, lax.dot_general(h_ref[...], dz_ref[...], TN_DIMS, preferred_element_type=F32), pl.program_id(2), nk, finish)

    return pl.pallas_call(
        body, name=name, grid=(N_DEV, d // tm, nk), out_shape=jax.ShapeDtypeStruct((N_DEV, d, n), BF16),
        in_specs=[pl.BlockSpec((tk, tm), lambda s, i, k: (k, i)), pl.BlockSpec((None, tk, n), lambda s, i, k: (s, k, 0))],
        out_specs=pl.BlockSpec((None, tm, n), lambda s, i, k: (s, i, 0)),
        scratch_shapes=[pltpu.VMEM((tm, n), F32)],
        compiler_params=_params("parallel", "parallel", "arbitrary"))(h, dz_sh)


def _mm_dh_shards(dz_sh, w_sh, name, after):
    t = dz_sh.shape[1]
    d, n = w_sh.shape[1:]
    tm, tn = _pick(t), _pick(d)

    def body(dz_ref, w_ref, after_ref, o_ref, acc):
        def finish(v):
            o_ref[...] = v

        _accumulate(acc, lax.dot_general(dz_ref[...], w_ref[...], NT_DIMS, preferred_element_type=F32), pl.program_id(2), N_DEV, finish)

    return pl.pallas_call(
        body, name=name, grid=(t // tm, d // tn, N_DEV), out_shape=jax.ShapeDtypeStruct((t, d), F32),
        in_specs=[pl.BlockSpec((None, tm, n), lambda i, j, s: (s, i, 0)), pl.BlockSpec((None, tn, n), lambda i, j, s: (s, j, 0)), _ANY],
        out_specs=pl.BlockSpec((tm, tn), lambda i, j, s: (i, j)),
        scratch_shapes=[pltpu.VMEM((tm, tn), F32)],
        compiler_params=_params("parallel", "parallel", "arbitrary"))(dz_sh, w_sh, after)


def _rms_inv(x):
    return lax.rsqrt(jnp.mean(x * x, axis=-1, keepdims=True) + RMS_EPS)


def _rms_fwd(x, g, name, after=None):
    t, d = x.shape
    tm = _pick(t, (256, 128))

    def body(x_ref, g_ref, *rest):
        xv = x_ref[...]
        rest[-1][...] = (xv * _rms_inv(xv) * g_ref[...]).astype(BF16)

    return pl.pallas_call(
        body, name=name, grid=(t // tm,), out_shape=jax.ShapeDtypeStruct((t, d), BF16),
        in_specs=[_rb(tm, d), _vec(d)] + ([] if after is None else [_ANY]), out_specs=_rb(tm, d),
        compiler_params=_params("parallel"))(x, g, *([] if after is None else [after]))


def _rms_bwd_rows(dy, xv, g):
    inv = _rms_inv(xv)
    xhat = xv * inv
    dxhat = dy * g
    dx = inv * (dxhat - xhat * jnp.mean(dxhat * xhat, axis=-1, keepdims=True))
    return dx, dy * xhat


def _rms_bwd(dy, x, g, res, name):
    t, d = x.shape
    tm = _pick(t, (128,))

    def body(dy_ref, x_ref, g_ref, r_ref, dx_ref, dxb_ref, dg_ref):
        dx, dgrow = _rms_bwd_rows(dy_ref[...].astype(F32), x_ref[...], g_ref[...])
        dx = r_ref[...] + dx
        dx_ref[...] = dx
        dxb_ref[...] = dx.astype(BF16)

        @pl.when(pl.program_id(0) == 0)
        def _():
            dg_ref[...] = jnp.zeros_like(dg_ref)

        dg_ref[...] += jnp.sum(dgrow, axis=0, keepdims=True)

    return pl.pallas_call(
        body, name=name, grid=(t // tm,),
        out_shape=(jax.ShapeDtypeStruct((t, d), F32), jax.ShapeDtypeStruct((t, d), BF16), jax.ShapeDtypeStruct((1, d), F32)),
        in_specs=[_rb(tm, d), _rb(tm, d), _vec(d), _rb(tm, d)],
        out_specs=(_rb(tm, d), _rb(tm, d), _vec(d)), compiler_params=_params("arbitrary"))(dy, x, g, res)


def _latent_norm(z_all, g_q, g_kv, q_off, kv_off, name):
    t = z_all.shape[0]
    tm = _pick(t, (256, 128))

    def body(qa_ref, kva_ref, gq_ref, gkv_ref, qn_ref, kvn_ref):
        qa = qa_ref[...].astype(F32)
        qn_ref[...] = (qa * _rms_inv(qa) * gq_ref[...]).astype(BF16)
        kva = kva_ref[...].astype(F32)
        kvn_ref[...] = (kva * _rms_inv(kva) * gkv_ref[...]).astype(BF16)

    return pl.pallas_call(
        body, name=name, grid=(t // tm,),
        out_shape=(jax.ShapeDtypeStruct((t, Q_LORA), BF16), jax.ShapeDtypeStruct((t, KV_LORA), BF16)),
        in_specs=[_rb(tm, Q_LORA, q_off // Q_LORA), _rb(tm, KV_LORA, kv_off // KV_LORA), _vec(Q_LORA), _vec(KV_LORA)],
        out_specs=(_rb(tm, Q_LORA), _rb(tm, KV_LORA)), compiler_params=_params("parallel"))(z_all, z_all, g_q, g_kv)


def _latent_norm_bwd(dqn, dkvn, z_all, g_q, g_kv, q_off, kv_off, name):
    t = z_all.shape[0]
    tm = _pick(t, (256, 128))

    def body(dqn_ref, dkvn_ref, qa_ref, kva_ref, gq_ref, gkv_ref, dqa_ref, dkva_ref, dgq_ref, dgkv_ref):
        dqa, dgq = _rms_bwd_rows(dqn_ref[...].astype(F32), qa_ref[...].astype(F32), gq_ref[...])
        dkva, dgkv = _rms_bwd_rows(dkvn_ref[...].astype(F32), kva_ref[...].astype(F32), gkv_ref[...])
        dqa_ref[...] = dqa.astype(BF16)
        dkva_ref[...] = dkva.astype(BF16)

        @pl.when(pl.program_id(0) == 0)
        def _():
            dgq_ref[...] = jnp.zeros_like(dgq_ref)
            dgkv_ref[...] = jnp.zeros_like(dgkv_ref)

        dgq_ref[...] += jnp.sum(dgq, axis=0, keepdims=True)
        dgkv_ref[...] += jnp.sum(dgkv, axis=0, keepdims=True)

    return pl.pallas_call(
        body, name=name, grid=(t // tm,),
        out_shape=(jax.ShapeDtypeStruct((t, Q_LORA), BF16), jax.ShapeDtypeStruct((t, KV_LORA), BF16),
                   jax.ShapeDtypeStruct((1, Q_LORA), F32), jax.ShapeDtypeStruct((1, KV_LORA), F32)),
        in_specs=[_rb(tm, Q_LORA), _rb(tm, KV_LORA), _rb(tm, Q_LORA, q_off // Q_LORA), _rb(tm, KV_LORA, kv_off // KV_LORA),
                  _vec(Q_LORA), _vec(KV_LORA)],
        out_specs=(_rb(tm, Q_LORA), _rb(tm, KV_LORA), _vec(Q_LORA), _vec(KV_LORA)),
        compiler_params=_params("arbitrary"))(dqn, dkvn, z_all, z_all, g_q, g_kv)


def _rot(xv, cos_k, sin_a, sin_b, sign):
    return xv * cos_k + sign * (pltpu.roll(xv, LANE - 32, 1) * sin_a + pltpu.roll(xv, 32, 1) * sin_b)


def _rope_q(q_raw, tabs, n_heads, sign, out_dtype, name):
    t, w = q_raw.shape
    tm = _pick(t, (256, 128))

    def body(q_ref, cos_ref, sa_ref, sb_ref, o_ref):
        cos_k, sin_a, sin_b = cos_ref[...], sa_ref[...], sb_ref[...]
        for h in range(n_heads):
            lo = h * HEAD_PAD
            o_ref[:, lo:lo + LANE] = q_ref[:, lo:lo + LANE].astype(out_dtype)
            o_ref[:, lo + LANE:lo + HEAD_PAD] = _rot(
                q_ref[:, lo + LANE:lo + HEAD_PAD].astype(F32), cos_k, sin_a, sin_b, sign).astype(out_dtype)

    return pl.pallas_call(
        body, name=name, grid=(t // tm,), out_shape=jax.ShapeDtypeStruct((t, w), out_dtype),
        in_specs=[_rb(tm, w), _rb(tm, LANE), _rb(tm, LANE), _rb(tm, LANE)],
        out_specs=_rb(tm, w), compiler_params=_params("parallel"))(q_raw, *tabs)


def _rope_k(kv, z_all, tabs, n_heads, kr_off, name):
    t = kv.shape[0]
    tm = _pick(t, (256, 128))
    wk = n_heads * QK_NOPE

    def body(kn_ref, kr_ref, cos_ref, sa_ref, sb_ref, o_ref):
        krot = _rot(kr_ref[...].astype(F32), cos_ref[...], sa_ref[...], sb_ref[...], 1.0).astype(BF16)
        for h in range(n_heads):
            o_ref[:, h * HEAD_PAD:h * HEAD_PAD + LANE] = kn_ref[:, h * QK_NOPE:(h + 1) * QK_NOPE]
            o_ref[:, h * HEAD_PAD + LANE:(h + 1) * HEAD_PAD] = krot

    return pl.pallas_call(
        body, name=name, grid=(t // tm,), out_shape=jax.ShapeDtypeStruct((t, n_heads * HEAD_PAD), BF16),
        in_specs=[_rb(tm, wk), _rb(tm, LANE, kr_off // LANE), _rb(tm, LANE), _rb(tm, LANE), _rb(tm, LANE)],
        out_specs=_rb(tm, n_heads * HEAD_PAD), compiler_params=_params("parallel"))(kv, z_all, *tabs)


def _rope_k_bwd(dk_pad, dv, tabs, n_heads, name):
    t = dk_pad.shape[0]
    tm = _pick(t, (256, 128))
    wk = n_heads * QK_NOPE

    def body(dk_ref, dv_ref, cos_ref, sa_ref, sb_ref, dkv_ref, dkr_ref):
        acc = dk_ref[:, LANE:HEAD_PAD]
        dkv_ref[:, 0:QK_NOPE] = dk_ref[:, 0:LANE].astype(BF16)
        for h in range(1, n_heads):
            acc = acc + dk_ref[:, h * HEAD_PAD + LANE:(h + 1) * HEAD_PAD]
            dkv_ref[:, h * QK_NOPE:(h + 1) * QK_NOPE] = dk_ref[:, h * HEAD_PAD:h * HEAD_PAD + LANE].astype(BF16)
        dkv_ref[:, wk:] = dv_ref[...].astype(BF16)
        dkr_ref[...] = _rot(acc, cos_ref[...], sa_ref[...], sb_ref[...], -1.0).astype(BF16)

    return pl.pallas_call(
        body, name=name, grid=(t // tm,),
        out_shape=(jax.ShapeDtypeStruct((t, 2 * wk), BF16), jax.ShapeDtypeStruct((t, LANE), BF16)),
        in_specs=[_rb(tm, n_heads * HEAD_PAD), _rb(tm, wk), _rb(tm, LANE), _rb(tm, LANE), _rb(tm, LANE)],
        out_specs=(_rb(tm, 2 * wk), _rb(tm, LANE)), compiler_params=_params("parallel"))(dk_pad, dv, *tabs)


NT_DIMS = (((1,), (1,)), ((), ()))
TN_DIMS = (((0,), (0,)), ((), ()))


def _softmax_rows(q, k):
    s = lax.dot_general(q, k, NT_DIMS, preferred_element_type=F32) * SOFTMAX_SCALE
    e = jnp.exp(s - jnp.max(s, axis=-1, keepdims=True))
    return e * (1.0 / jnp.sum(e, axis=-1, keepdims=True))


def _attn_fwd(q_pad, k_pad, kv, n_heads, name):
    t = q_pad.shape[0]
    tq = _pick(t, (256, 128))

    def body(q_ref, k_ref, v_ref, o_ref):
        p = _softmax_rows(q_ref[...], k_ref[...]).astype(BF16)
        o_ref[...] = jnp.dot(p, v_ref[...], preferred_element_type=F32).astype(BF16)

    return pl.pallas_call(
        body, name=name, grid=(n_heads, t // tq),
        out_shape=jax.ShapeDtypeStruct((t, n_heads * V_HEAD), BF16),
        in_specs=[pl.BlockSpec((tq, HEAD_PAD), lambda h, i: (i, h)),
                  pl.BlockSpec((t, HEAD_PAD), lambda h, i: (0, h)),
                  pl.BlockSpec((t, V_HEAD), lambda h, i: (0, n_heads + h))],
        out_specs=pl.BlockSpec((tq, V_HEAD), lambda h, i: (i, h)),
        compiler_params=_params("parallel", "parallel"))(q_pad, k_pad, kv)


def _attn_bwd(q_pad, k_pad, kv, do, n_heads, name):
    t = q_pad.shape[0]
    tq = _pick(t, (256, 128))
    nq = t // tq

    def body(q_ref, k_ref, v_ref, do_ref, dq_ref, dk_ref, dv_ref):
        @pl.when(pl.program_id(1) == 0)
        def _():
            dk_ref[...] = jnp.zeros_like(dk_ref)
            dv_ref[...] = jnp.zeros_like(dv_ref)

        q, k, dout = q_ref[...], k_ref[...], do_ref[...]
        p = _softmax_rows(q, k)
        dp = lax.dot_general(dout, v_ref[...], NT_DIMS, preferred_element_type=F32)
        ds = (p * (dp - jnp.sum(p * dp, axis=-1, keepdims=True)) * SOFTMAX_SCALE).astype(BF16)
        dq_ref[...] = jnp.dot(ds, k, preferred_element_type=F32)
        dk_ref[...] += lax.dot_general(ds, q, TN_DIMS, preferred_element_type=F32)
        dv_ref[...] += lax.dot_general(p.astype(BF16), dout, TN_DIMS, preferred_element_type=F32)

    return pl.pallas_call(
        body, name=name, grid=(n_heads, nq),
        out_shape=(jax.ShapeDtypeStruct((t, n_heads * HEAD_PAD), F32), jax.ShapeDtypeStruct((t, n_heads * HEAD_PAD), F32),
                   jax.ShapeDtypeStruct((t, n_heads * V_HEAD), F32)),
        in_specs=[pl.BlockSpec((tq, HEAD_PAD), lambda h, i: (i, h)),
                  pl.BlockSpec((t, HEAD_PAD), lambda h, i: (0, h)),
                  pl.BlockSpec((t, V_HEAD), lambda h, i: (0, n_heads + h)),
                  pl.BlockSpec((tq, V_HEAD), lambda h, i: (i, h))],
        out_specs=(pl.BlockSpec((tq, HEAD_PAD), lambda h, i: (i, h)),
                   pl.BlockSpec((t, HEAD_PAD), lambda h, i: (0, h)),
                   pl.BlockSpec((t, V_HEAD), lambda h, i: (0, h))),
        compiler_params=_params("parallel", "arbitrary"))(q_pad, k_pad, kv, do)


def _shift_rows(u, t):
    row = lax.broadcasted_iota(jnp.int32, u.shape, 0)
    prev = jnp.where(row == 0, 0.0, pltpu.roll(u, 1, 0))
    nxt = jnp.where(row == t - 1, 0.0, pltpu.roll(u, t - 1, 0))
    return prev, nxt


def _conv_fwd(z_all, conv_w, cc, name):
    t = z_all.shape[0]
    nb = cc // LANE

    def body(cb_ref, cc_ref, ch_ref, w_ref, y_ref):
        u = cc_ref[...].astype(F32) * ch_ref[...].astype(F32)
        prev, nxt = _shift_rows(u, t)
        w = w_ref[...]
        conv = prev * w[0:1, :] + u * w[1:2, :] + nxt * w[2:3, :]
        y_ref[...] = (cb_ref[...].astype(F32) * conv).astype(BF16)

    col = lambda g: pl.BlockSpec((t, LANE), lambda j: (0, g * nb + j))
    return pl.pallas_call(
        body, name=name, grid=(nb,), out_shape=jax.ShapeDtypeStruct((t, cc), BF16),
        in_specs=[col(0), col(1), col(2), pl.BlockSpec((3, LANE), lambda j: (0, j))],
        out_specs=pl.BlockSpec((t, LANE), lambda j: (0, j)),
        compiler_params=_params("parallel"))(z_all, z_all, z_all, conv_w)


def _conv_bwd(dy, z_all, conv_w, cc, name):
    t = z_all.shape[0]
    nb = cc // LANE

    def body(dy_ref, cb_ref, cc_ref, ch_ref, w_ref, dcb_ref, dcc_ref, dch_ref, dw_ref):
        c_c, c_h = cc_ref[...].astype(F32), ch_ref[...].astype(F32)
        u = c_c * c_h
        prev, nxt = _shift_rows(u, t)
        w = w_ref[...]
        dyv = dy_ref[...].astype(F32)
        dcb_ref[...] = (dyv * (prev * w[0:1, :] + u * w[1:2, :] + nxt * w[2:3, :])).astype(BF16)
        dconv = dyv * cb_ref[...].astype(F32)
        dw_ref[0:1, :] = jnp.sum(dconv * prev, axis=0, keepdims=True)
        dw_ref[1:2, :] = jnp.sum(dconv * u, axis=0, keepdims=True)
        dw_ref[2:3, :] = jnp.sum(dconv * nxt, axis=0, keepdims=True)
        dprev, dnxt = _shift_rows(dconv, t)
        du = dnxt * w[0:1, :] + dconv * w[1:2, :] + dprev * w[2:3, :]
        dcc_ref[...] = (du * c_h).astype(BF16)
        dch_ref[...] = (du * c_c).astype(BF16)

    col = lambda g: pl.BlockSpec((t, LANE), lambda j: (0, g * nb + j))
    one = pl.BlockSpec((t, LANE), lambda j: (0, j))
    wsp = pl.BlockSpec((3, LANE), lambda j: (0, j))
    act = jax.ShapeDtypeStruct((t, cc), BF16)
    return pl.pallas_call(
        body, name=name, grid=(nb,),
        out_shape=(act, act, act, jax.ShapeDtypeStruct((3, cc), F32)),
        in_specs=[one, col(0), col(1), col(2), wsp],
        out_specs=(one, one, one, wsp),
        compiler_params=_params("parallel"))(dy, z_all, z_all, z_all, conv_w)


def _sigmoid(v):
    return 1.0 / (1.0 + jnp.exp(-v))


def _merge_fwd(z_all, b_gate, y_a, y_b, gate_off, name):
    t, d = y_a.shape
    tm = _pick(t, (128,))
    gb = gate_off // d

    def body(za_ref, zb_ref, ba_ref, bb_ref, ya_ref, yb_ref, m_ref):
        ga = _sigmoid(za_ref[...].astype(F32) + ba_ref[...])
        gbv = _sigmoid(zb_ref[...].astype(F32) + bb_ref[...])
        m_ref[...] = (ga * ya_ref[...].astype(F32) + gbv * yb_ref[...].astype(F32)).astype(BF16)

    return pl.pallas_call(
        body, name=name, grid=(t // tm,), out_shape=jax.ShapeDtypeStruct((t, d), BF16),
        in_specs=[_rb(tm, d, gb), _rb(tm, d, gb + 1), _vec(d, 0), _vec(d, 1), _rb(tm, d), _rb(tm, d)],
        out_specs=_rb(tm, d), compiler_params=_params("parallel"))(z_all, z_all, b_gate, b_gate, y_a, y_b)


def _merge_bwd(dm, z_all, b_gate, y_a, y_b, gate_off, name):
    t, d = y_a.shape
    tm = _pick(t, (128,))
    gb = gate_off // d

    def body(dm_ref, za_ref, zb_ref, ba_ref, bb_ref, ya_ref, yb_ref, dya_ref, dyb_ref, dzg_ref, db_ref):
        dmv = dm_ref[...].astype(F32)
        ga = _sigmoid(za_ref[...].astype(F32) + ba_ref[...])
        gbv = _sigmoid(zb_ref[...].astype(F32) + bb_ref[...])
        dya_ref[...] = (dmv * ga).astype(BF16)
        dyb_ref[...] = (dmv * gbv).astype(BF16)
        dza = dmv * ya_ref[...].astype(F32) * (ga * (1.0 - ga))
        dzb = dmv * yb_ref[...].astype(F32) * (gbv * (1.0 - gbv))
        dzg_ref[:, 0:d] = dza.astype(BF16)
        dzg_ref[:, d:2 * d] = dzb.astype(BF16)

        @pl.when(pl.program_id(0) == 0)
        def _():
            db_ref[...] = jnp.zeros_like(db_ref)

        db_ref[:, 0:d] += jnp.sum(dza, axis=0, keepdims=True)
        db_ref[:, d:2 * d] += jnp.sum(dzb, axis=0, keepdims=True)

    act = jax.ShapeDtypeStruct((t, d), BF16)
    return pl.pallas_call(
        body, name=name, grid=(t // tm,),
        out_shape=(act, act, jax.ShapeDtypeStruct((t, 2 * d), BF16), jax.ShapeDtypeStruct((1, 2 * d), F32)),
        in_specs=[_rb(tm, d), _rb(tm, d, gb), _rb(tm, d, gb + 1), _vec(d, 0), _vec(d, 1), _rb(tm, d), _rb(tm, d)],
        out_specs=(_rb(tm, d), _rb(tm, d), _rb(tm, 2 * d), _vec(2 * d)),
        compiler_params=_params("arbitrary"))(dm, z_all, z_all, b_gate, b_gate, y_a, y_b)


def _swiglu_fwd(gate, up, name, after):
    t, f = gate.shape
    tm = _pick(t, (128,))

    def body(g_ref, u_ref, after_ref, a_ref):
        g = g_ref[...].astype(F32)
        a_ref[...] = (g * _sigmoid(g) * u_ref[...].astype(F32)).astype(BF16)

    return pl.pallas_call(
        body, name=name, grid=(t // tm,), out_shape=jax.ShapeDtypeStruct((t, f), BF16),
        in_specs=[_rb(tm, f), _rb(tm, f), _ANY], out_specs=_rb(tm, f), compiler_params=_params("parallel"))(gate, up, after)


def _swiglu_bwd(dact, gate, up, name):
    t, f = gate.shape
    tm = _pick(t, (128,))

    def body(da_ref, g_ref, u_ref, dg_ref, du_ref):
        g, da = g_ref[...].astype(F32), da_ref[...].astype(F32)
        sg = _sigmoid(g)
        dg_ref[...] = (da * u_ref[...].astype(F32) * (sg * (1.0 + g * (1.0 - sg)))).astype(BF16)
        du_ref[...] = (da * (g * sg)).astype(BF16)

    act = jax.ShapeDtypeStruct((t, f), BF16)
    return pl.pallas_call(
        body, name=name, grid=(t // tm,), out_shape=(act, act),
        in_specs=[_rb(tm, f)] * 3, out_specs=(_rb(tm, f), _rb(tm, f)), compiler_params=_params("parallel"))(dact, gate, up)


def _loss_head(x2, target, g, name):
    t, d = x2.shape
    tm = _pick(t, (128,))

    def body(x_ref, t_ref, g_ref, loss_ref, dx_ref, dxb_ref, dg_ref):
        xv, gv = x_ref[...], g_ref[...]
        err = xv * _rms_inv(xv) * gv - t_ref[...]
        dx, dgrow = _rms_bwd_rows(err * (1.0 / d), xv, gv)
        dx_ref[...] = dx
        dxb_ref[...] = dx.astype(BF16)

        @pl.when(pl.program_id(0) == 0)
        def _():
            loss_ref[...] = jnp.zeros_like(loss_ref)
            dg_ref[...] = jnp.zeros_like(dg_ref)

        loss_ref[...] += (0.5 / d) * jnp.sum(jnp.sum(err * err, axis=1, keepdims=True), axis=0, keepdims=True)
        dg_ref[...] += jnp.sum(dgrow, axis=0, keepdims=True)

    return pl.pallas_call(
        body, name=name, grid=(t // tm,),
        out_shape=(jax.ShapeDtypeStruct((1, 1), F32), jax.ShapeDtypeStruct((t, d), F32),
                   jax.ShapeDtypeStruct((t, d), BF16), jax.ShapeDtypeStruct((1, d), F32)),
        in_specs=[_rb(tm, d), _rb(tm, d), _vec(d)],
        out_specs=(pl.BlockSpec((1, 1), lambda i: (0, 0)), _rb(tm, d), _rb(tm, d), _vec(d)),
        compiler_params=_params("arbitrary"))(x2, target, g)


def _adamw(parts, w, m, v, name):
    r, c = w.shape
    cp = parts.shape[2]
    assert parts.shape[1] >= r and cp >= c
    tr = r if r * c <= ADAM_BLOCK_ELEMS else _pick(r, tuple(s for s in (512, 256, 128, 64, 32, 16, 8) if s * c <= ADAM_BLOCK_ELEMS))
    assert tr == parts.shape[1] or tr % 8 == 0

    def body(p_ref, w_ref, m_ref, v_ref, g_ref, d_ref, nm_ref, nv_ref):
        g = p_ref[0, :, 0:c].astype(F32)
        for s in range(1, N_DEV):
            g = g + p_ref[s, :, 0:c].astype(F32)
        nm = ADAM_B1 * m_ref[...] + (1.0 - ADAM_B1) * g
        nv = ADAM_B2 * v_ref[...] + (1.0 - ADAM_B2) * (g * g)
        m_hat = nm / (1.0 - ADAM_B1 ** ADAM_STEP)
        v_hat = nv / (1.0 - ADAM_B2 ** ADAM_STEP)
        g_ref[...] = g
        d_ref[...] = -ADAM_LR * (m_hat / (jnp.sqrt(v_hat) + ADAM_EPS) + ADAM_WD * w_ref[...])
        nm_ref[...] = nm
        nv_ref[...] = nv

    blk = pl.BlockSpec((tr, c), lambda i: (i, 0))
    out = jax.ShapeDtypeStruct((r, c), F32)
    return pl.pallas_call(
        body, name=name, grid=(r // tr,), out_shape=(out, out, out, out),
        in_specs=[pl.BlockSpec((N_DEV, tr, cp), lambda i: (0, i, 0)), blk, blk, blk],
        out_specs=(blk, blk, blk, blk), compiler_params=_params("parallel"))(parts, w, m, v)


def _cols_of(g):
    return jnp.transpose(g, (1, 0, 2)).reshape(g.shape[1], N_DEV * g.shape[2])


def _col_parts(dw):
    k, n8 = dw.shape
    return jnp.transpose(dw.reshape(k, N_DEV, n8 // N_DEV), (1, 0, 2))


def kernel(x, positions, g_mix, w_in, b_gate, conv_w, g_q_a, w_q_b, g_kv_a, w_kv_b, w_branch, w_out, g_ffn, w_ffn_gate, w_ffn_up, w_ffn_down, g_final, loss_target, m_g_mix, m_w_in, m_b_gate, m_conv_w, m_g_q_a, m_w_q_b, m_g_kv_a, m_w_kv_b, m_w_branch, m_w_out, m_g_ffn, m_w_ffn_gate, m_w_ffn_up, m_w_ffn_down, m_g_final, v_g_mix, v_w_in, v_b_gate, v_conv_w, v_g_q_a, v_w_q_b, v_g_kv_a, v_w_kv_b, v_w_branch, v_w_out, v_g_ffn, v_w_ffn_gate, v_w_ffn_up, v_w_ffn_down, v_g_final):
    given = dict(locals())
    xs = x[0]
    t, d = xs.shape
    cc = d // 2
    n_heads = cc // V_HEAD
    in_cols = N_DEV * w_in.shape[2]
    q_off, kv_off, kr_off = 3 * cc, 3 * cc + Q_LORA, 3 * cc + Q_LORA + KV_LORA
    head_cols = kr_off + QK_ROPE
    head_pad = -(-(kr_off + LANE) // 1024) * 1024
    assert in_cols == head_cols + 2 * d and q_off % Q_LORA == 0 and kv_off % KV_LORA == 0 and kr_off % LANE == 0
    fs = w_ffn_gate.shape[2]
    fsp = -(-fs // LANE) * LANE
    ffp = N_DEV * fsp
    me = _me_index()
    bf = lambda a: a.astype(BF16)
    one_slot = lambda refs, i: refs[i].at[0]

    x_, y_, c_ = lax.axis_index("x"), lax.axis_index("y"), lax.axis_index("c")
    place = lambda px, py, pc: 4 * px + 2 * py + pc
    chips = [(1 - x_, y_), (x_, 1 - y_), (1 - x_, 1 - y_)]
    mix_cols = [w_q_b.shape[2], w_kv_b.shape[2], w_branch.shape[3], None]
    ffn_cols = [fsp, fsp]
    g_in = _gather_start(_gather_prepare([bf(w_in[0]), conv_w[0]], me, "ag1_start_in"), "ag1_start_in")
    zero = g_in[3][0, 0]
    bf_later = lambda a: (a + zero).astype(BF16)
    h = _rms_fwd(xs, g_mix, "rms_mix", after=g_in[3])
    bufs = g_in[2]
    z_sh = _mm_z_shard(h, bufs[0], me, None, "mm_z_own")
    bufs = _split_wait(g_in, one_slot, z_sh, "ag1_wait_in_sib", bufs=bufs, recvs=(0,), sends=())
    z_sh = _mm_z_shard(h, bufs[0], place(x_, y_, 1 - c_), z_sh, "mm_z_sib")
    mix_bufs = _gather_prepare([bf_later(w_q_b[0]), bf_later(w_kv_b[0]), bf_later(w_branch[0].reshape(2 * cc, -1)), bf_later(w_out[0])],
                               me, "ag1_start_mix", mix_cols)
    gate_up_bufs = _gather_prepare([bf_later(jnp.pad(w_ffn_gate[0], ((0, 0), (0, fsp - fs)))),
                                    bf_later(jnp.pad(w_ffn_up[0], ((0, 0), (0, fsp - fs))))], me, "ag1_start_gate_up", ffn_cols)
    down_bufs = _gather_prepare([bf_later(jnp.pad(w_ffn_down[0], ((0, fsp - fs), (0, 0))))], me, "ag1_start_down")
    prepared = tuple(mix_bufs + gate_up_bufs + down_bufs)
    passed_on = []
    for j, (px, py) in enumerate(chips):
        last = j == len(chips) - 1
        bufs = _split_wait(g_in, one_slot, (z_sh,) + (prepared if j == 0 else ()), "ag1_wait_in_chip%d" % j, bufs=bufs,
                           recvs=(1 + j,), sends=range(4) if last else ())
        f = _forward_start(bufs, "ag2_start_in_chip%d" % j, chips=(j,))
        passed_on.append(f)
        bufs, tokens = f[2], (f[3],)
        if last:
            g_mix_w = _gather_start(mix_bufs, "ag1_start_mix", after=f[3], cols=mix_cols)
            tokens = (g_mix_w[3],)
        z_sh = _mm_z_shard(h, bufs[0], place(px, py, c_), z_sh, "mm_z_chip%d" % j, after=tokens)
        for jj in ((0, 1) if j == 1 else (2,) if last else ()):
            bufs = _split_wait(passed_on[jj], one_slot, z_sh, "ag2_wait_in_chip%d" % jj, bufs=bufs)
            z_sh = _mm_z_shard(h, bufs[0], place(*chips[jj], 1 - c_), z_sh, "mm_z_chip%d_sib" % jj)
    w_in_g, cw_g = bufs
    cw = _cols_of(cw_g)
    z_full = jnp.transpose(z_sh, (1, 0, 2)).reshape(t, in_cols)
    z_head = jnp.concatenate([z_full[:, :head_cols], jnp.zeros((t, head_pad - head_cols), BF16)], axis=1)
    zg = z_full[:, head_cols:]

    inv_freq = ROPE_THETA ** (-jnp.arange(0, QK_ROPE, 2, dtype=F32) / QK_ROPE)
    ang = positions[0].astype(F32)[:, None] * inv_freq[None, :]
    cos, sin = jnp.cos(ang), jnp.sin(ang)
    z32, z64 = jnp.zeros((t, 32), F32), jnp.zeros((t, 64), F32)
    tabs = (jnp.concatenate([cos, cos, jnp.ones((t, 64), F32)], axis=1),
            jnp.concatenate([-sin, z32, z64], axis=1),
            jnp.concatenate([z32, sin, z64], axis=1))

    f_mix = _forward_start(_split_wait(g_mix_w, _first_window(mix_cols), zg, "ag1_wait_mix"), "ag2_start_mix", cols=mix_cols)
    g_gate_up = _gather_start(gate_up_bufs, "ag1_start_gate_up", after=f_mix[3], cols=ffn_cols)
    y_a = _conv_fwd(z_head, cw, cc, "conv_fwd")
    qn, kvn = _latent_norm(z_head, g_q_a, g_kv_a, q_off, kv_off, "latent_norm")
    wq_full, wkv_full, wbr, wo_g = _split_wait(f_mix, _first_window(mix_cols), g_gate_up[3], "ag2_wait_mix")
    wq = wq_full.reshape(Q_LORA, n_heads, QK_NOPE + QK_ROPE)
    wq_pad = jnp.pad(wq, ((0, 0), (0, 0), (0, HEAD_PAD - QK_NOPE - QK_ROPE))).reshape(Q_LORA, n_heads * HEAD_PAD)
    wkv = wkv_full.reshape(KV_LORA, n_heads, 2, QK_NOPE)
    wkv_perm = jnp.transpose(wkv, (0, 2, 1, 3)).reshape(KV_LORA, 2 * n_heads * QK_NOPE)
    wb_a, wb_b = wbr[:cc], wbr[cc:]
    wo = wo_g.reshape(d, d)
    q_pad = _rope_q(_matmul(qn, wq_pad, name="mm_q", out_dtype=F32), tabs, n_heads, 1.0, BF16, "rope_q")
    kv = _matmul(kvn, wkv_perm, name="mm_kv")
    k_pad = _rope_k(kv, z_head, tabs, n_heads, kr_off, "rope_k")
    y_b = _attn_fwd(q_pad, k_pad, kv, n_heads, "attn_fwd")
    f_gate_up = _forward_start(_split_wait(g_gate_up, _first_window(ffn_cols), y_b, "ag1_wait_gate_up"), "ag2_start_gate_up", cols=ffn_cols)
    g_down = _gather_start(down_bufs, "ag1_start_down", after=f_gate_up[3])
    ybr_a = _matmul(y_a, wb_a, name="mm_br_a", after=g_down[3])
    ybr_b = _matmul(y_b, wb_b, name="mm_br_b")
    merged = _merge_fwd(zg, b_gate, ybr_a, ybr_b, 0, "merge_fwd")
    x1 = _matmul(merged, wo, name="mm_out", res=xs, out_dtype=F32)
    h2 = _rms_fwd(x1, g_ffn, "rms_ffn")
    wg, wu = _split_wait(f_gate_up, _first_window(ffn_cols), h2, "ag2_wait_gate_up")
    gate = _matmul(h2, wg, name="mm_gate")
    up = _matmul(h2, wu, name="mm_up")
    f_down = _forward_start(_split_wait(g_down, one_slot, up, "ag1_wait_down"), "ag2_start_down")
    act = _swiglu_fwd(gate, up, "swiglu_fwd", after=f_down[3])
    wd = _split_wait(f_down, one_slot, act, "ag2_wait_down")[0].reshape(ffp, d)
    x2 = _matmul(act, wd, name="mm_down", res=x1, out_dtype=F32)
    loss_part, dx2, dx2b, dg_final = _loss_head(x2, loss_target[0], g_final.reshape(1, d), "loss_head")

    dact = _matmul(dx2b, wd, tb=True, name="mm_d_act")
    dwd = _matmul(act, dx2b, ta=True, name="mm_dw_down")
    r_down = _exchange_start([dwd.reshape(N_DEV, fsp, d)], me, "rs_start_down")
    dgate, dup = _swiglu_bwd(dact, gate, up, "swiglu_bwd")
    dwg = _matmul(h2, dgate, ta=True, name="mm_dw_gate", after=r_down[3])
    dwu = _matmul(h2, dup, ta=True, name="mm_dw_up")
    r_gate_up = _exchange_start([dwg, dwu], me, "rs_start_gate_up", cols=[fsp, fsp])
    dh2 = _matmul(dgate, wg, tb=True, name="mm_d_h2_gate", out_dtype=F32, after=r_gate_up[3])
    dh2 = _matmul(dup, wu, tb=True, name="mm_d_h2_up", res=dh2, out_dtype=F32)
    dx1, dx1b, dg_ffn = _rms_bwd(dh2, x1, g_ffn, dx2, "rms_ffn_bwd")
    dwo = _matmul(merged, dx1b, ta=True, name="mm_dw_out")
    r_out = _exchange_start([dwo.reshape(N_DEV, d // N_DEV, d)], me, "rs_start_out")
    dmerged = _matmul(dx1b, wo, tb=True, name="mm_d_merged", after=r_out[3])
    dybr_a, dybr_b, dzg, db_gate = _merge_bwd(dmerged, zg, b_gate, ybr_a, ybr_b, 0, "merge_bwd")
    dwb_a = _matmul(y_a, dybr_a, ta=True, name="mm_dw_br_a")
    dwb_b = _matmul(y_b, dybr_b, ta=True, name="mm_dw_br_b")
    r_br = _exchange_start([dwb_a, dwb_b], me, "rs_start_branch", cols=[d // N_DEV] * 2)
    dy_a = _matmul(dybr_a, wb_a, tb=True, name="mm_d_y_a", after=r_br[3])
    dy_b = _matmul(dybr_b, wb_b, tb=True, name="mm_d_y_b")
    dq_pad, dk_pad, dv = _attn_bwd(q_pad, k_pad, kv, dy_b, n_heads, "attn_bwd")
    dq_raw = _rope_q(dq_pad, tabs, n_heads, -1.0, BF16, "rope_q_bwd")
    dkv, dkr = _rope_k_bwd(dk_pad, dv, tabs, n_heads, "rope_k_bwd")
    dwq = _matmul(qn, dq_raw, ta=True, name="mm_dw_q")
    dwkv = _matmul(kvn, dkv, ta=True, name="mm_dw_kv")
    dwq_full = dwq.reshape(Q_LORA, n_heads, HEAD_PAD)[:, :, :QK_NOPE + QK_ROPE].reshape(Q_LORA, -1)
    dwkv_full = jnp.transpose(dwkv.reshape(KV_LORA, 2, n_heads, QK_NOPE), (0, 2, 1, 3)).reshape(KV_LORA, -1)
    r_qkv = _exchange_start([_col_parts(dwq_full), _col_parts(dwkv_full)], me, "rs_start_q_kv")
    dqn = _matmul(dq_raw, wq_pad, tb=True, name="mm_d_qn", after=r_qkv[3])
    dkvn = _matmul(dkv, wkv_perm, tb=True, name="mm_d_kvn")
    dqa, dkva, dg_q, dg_kv = _latent_norm_bwd(dqn, dkvn, z_head, g_q_a, g_kv_a, q_off, kv_off, "latent_norm_bwd")
    dcb, dcc, dch, dcw = _conv_bwd(dy_a, z_head, cw, cc, "conv_bwd")
    dz_full = jnp.concatenate([dcb, dcc, dch, dqa, dkva, dkr[:, :QK_ROPE], dzg], axis=1)
    dz_sh = jnp.transpose(dz_full.reshape(t, N_DEV, in_cols // N_DEV), (1, 0, 2))
    r_in = _exchange_start([_mm_dw_shards(h, dz_sh, "mm_dw_in")], me, "rs_start_in")
    dh = _mm_dh_shards(dz_sh, w_in_g, "mm_d_h", r_in[3])
    dx, _, dg_mix = _rms_bwd(dh, xs, g_mix, dx1, "rms_mix_bwd")

    results = {}
    last = dx

    def update(started, targets):
        nonlocal last
        n = len(targets)
        recvs = _split_wait(started, _land_block(n), last, "rs_wait_" + targets[0][0])[n:]
        for recv, (key, w_, m_, v_) in zip(recvs, targets):
            results[key] = _adamw(recv, w_, m_, v_, "adamw_" + key)
            last = results[key][0]

    shard = lambda wname: (wname,) + tuple(given[p + wname][0] for p in ("", "m_", "v_"))
    branch = lambda b: ("w_branch_%d" % b,) + tuple(given[p + "w_branch"][0, b] for p in ("", "m_", "v_"))
    update(r_down, [shard("w_ffn_down")])
    update(r_gate_up, [shard("w_ffn_gate"), shard("w_ffn_up")])
    update(r_out, [shard("w_out")])
    update(r_br, [branch(0), branch(1)])
    results["w_branch"] = tuple(jnp.stack([results["w_branch_0"][k], results["w_branch_1"][k]]) for k in range(4))
    update(r_qkv, [shard("w_q_b"), shard("w_kv_b")])

    small = [("g_mix", dg_mix), ("b_gate", db_gate), ("g_q_a", dg_q), ("g_kv_a", dg_kv), ("g_ffn", dg_ffn),
             ("g_final", dg_final), ("conv_w", dcw.reshape(1, 3 * cc))]
    packed = _all_gather(jnp.concatenate([p for _, p in small], axis=1), "ag_small_grads", last)
    off = 0
    for wname, p in small:
        n = p.shape[1]
        parts = packed[:, :, off:off + n]
        off += n
        if wname == "conv_w":
            width = conv_w.shape[2]
            parts = lax.dynamic_slice_in_dim(parts.reshape(N_DEV, 3, cc), me * width, width, axis=2)
            flat = (3, width)
        else:
            flat = (1, n)
        results[wname] = _adamw(parts, given[wname].reshape(flat), given["m_" + wname].reshape(flat),
                                given["v_" + wname].reshape(flat), "adamw_" + wname)
        last = results[wname][0]
    update(r_in, [shard("w_in")])

    loss = lax.psum(loss_part[0, 0], MESH_AXES)
    order = ["g_mix", "w_in", "b_gate", "conv_w", "g_q_a", "w_q_b", "g_kv_a", "w_kv_b", "w_branch", "w_out", "g_ffn",
             "w_ffn_gate", "w_ffn_up", "w_ffn_down", "g_final"]
    out = [loss, dx[None]]
    for k in range(4):
        out += [results[n][k].reshape(given[n].shape) for n in order]
    return tuple(out)
```

```python
import functools
import math

import jax
import jax.numpy as jnp
from jax import lax
from jax.experimental import pallas as pl
from jax.experimental.pallas import tpu as pltpu

F32 = jnp.float32
BF16 = jnp.bfloat16
N_DEV = 8
MESH_AXES = ("x", "y", "c")
MESH = pl.DeviceIdType.MESH

QK_NOPE = 128
QK_ROPE = 64
V_HEAD = 128
HEAD_PAD = 256
Q_LORA = 1024
KV_LORA = 512
ROPE_THETA = 10000.0
RMS_EPS = 1e-6
SOFTMAX_SCALE = 1.0 / math.sqrt(QK_NOPE + QK_ROPE)
ADAM_LR, ADAM_B1, ADAM_B2, ADAM_EPS, ADAM_WD, ADAM_STEP = 0.001, 0.9, 0.999, 1e-08, 0.01, 10

VMEM_LIMIT = 60 * 1024 * 1024
LANE = 128
ADAM_BLOCK_ELEMS = 1 << 18
COPY_BLOCK_BYTES = 4 * 1024 * 1024
MATMUL_K_TILES =(4096, 2816, 2048, 1024, 512, 256, 128)
MATMUL_VMEM_BUDGET = 48 * 1024 * 1024


def _pick(n, cands=(1024, 512, 256, 128)):
    for c in cands:
        if n % c == 0:
            return c
    return n


def _params(*sem):
    return pltpu.CompilerParams(dimension_semantics=sem, vmem_limit_bytes=VMEM_LIMIT)


def _rb(tm, c, cb=0):
    return pl.BlockSpec((tm, c), lambda i: (i, cb))


def _vec(c, cb=0):
    return pl.BlockSpec((1, c), lambda i: (0, cb))


def _all_gather(x, name, after):
    def body(x_ref, after_ref, out_ref, send_sems, recv_sems, local_sem):
        x_, y_, c_ = lax.axis_index("x"), lax.axis_index("y"), lax.axis_index("c")
        me, sibling = (x_, y_, c_), (x_, y_, 1 - c_)
        chips = [(1 - x_, y_), (x_, 1 - y_), (1 - x_, 1 - y_)]

        def slot(px, py, pc):
            return out_ref.at[4 * px + 2 * py + pc]

        def copy(k, block, to, src=None):
            return pltpu.make_async_remote_copy(
                src_ref=slot(*block) if src is None else src, dst_ref=slot(*block),
                send_sem=send_sems.at[k], recv_sem=recv_sems.at[k], device_id=to, device_id_type=MESH)

        mine = pltpu.make_async_copy(x_ref, slot(*me), local_sem)
        mine.start()
        first = [copy(0, me, sibling, src=x_ref)]
        first += [copy(1 + j, me, (*chip, c_), src=x_ref) for j, chip in enumerate(chips)]
        for cp in first:
            cp.start()
        passed = [copy(4 + j, (*chip, c_), sibling) for j, chip in enumerate(chips)]
        for j, chip in enumerate(chips):
            copy(1 + j, (*chip, c_), me).wait_recv()
            passed[j].start()
        copy(0, sibling, me).wait_recv()
        for j, chip in enumerate(chips):
            copy(4 + j, (*chip, 1 - c_), me).wait_recv()
        for cp in first + passed:
            cp.wait_send()
        mine.wait()

    return pl.pallas_call(
        body, name=name,
        out_shape=jax.ShapeDtypeStruct((N_DEV,) + x.shape, x.dtype),
        in_specs=[pl.BlockSpec(memory_space=pl.ANY), pl.BlockSpec(memory_space=pl.ANY)],
        out_specs=pl.BlockSpec(memory_space=pl.ANY),
        scratch_shapes=[pltpu.SemaphoreType.DMA((7,)), pltpu.SemaphoreType.DMA((7,)), pltpu.SemaphoreType.DMA(())],
    )(x, after)


_HBM = pl.BlockSpec(memory_space=pltpu.HBM)
_SEM = pl.BlockSpec(memory_space=pltpu.SEMAPHORE)
_ANY = pl.BlockSpec(memory_space=pl.ANY)
_EFFECT = pltpu.SideEffectType.DATAFLOW_SIDE_EFFECTING


def _me_index():
    return 4 * lax.axis_index("x") + 2 * lax.axis_index("y") + lax.axis_index("c")


def _window(ref, idx, cols):
    if cols is None:
        return ref.at[idx]
    return ref.at[:, pl.ds(idx * cols if isinstance(idx, int) else pl.multiple_of(idx * cols, LANE), cols)]


def _own_block_only(src, me, name, cols=None, src_cols=False):
    k, n = src.shape if src_cols is False else ((src.shape[0], src_cols) if src_cols else src.shape[1:])
    assert cols is None or (n == cols and cols % LANE == 0)
    out_shape = (N_DEV, k, n) if cols is None else (k, N_DEV * cols)
    limit = COPY_BLOCK_BYTES // (n * src.dtype.itemsize)
    tr = k if k <= limit else _pick(k, tuple(s for s in (1024, 512, 256, 128, 64, 32, 16) if s <= limit))

    def spec(layout):
        if layout is False:
            return pl.BlockSpec((tr, n), lambda i, me_ref: (i, 0))
        if layout is None:
            return pl.BlockSpec((None, tr, n), lambda i, me_ref: (me_ref[0], i, 0))
        return pl.BlockSpec((tr, n), lambda i, me_ref: (i, me_ref[0]))

    def body(me_ref, src_ref, dst_ref):
        dst_ref[...] = src_ref[...]

    return pl.pallas_call(
        body, name=name, out_shape=jax.ShapeDtypeStruct(out_shape, src.dtype),
        grid_spec=pltpu.PrefetchScalarGridSpec(num_scalar_prefetch=1, grid=(k // tr,), in_specs=[spec(src_cols)], out_specs=spec(cols)),
        compiler_params=_params("parallel"))(jnp.reshape(me, (1,)).astype(jnp.int32), src)


def _split_start(bufs, sem_shape, issue, name, after=None):
    n = len(bufs)
    rows, per_row = sem_shape
    sem = lambda sems, i, k: sems.at[i * per_row + k]
    first_out = n + (after is not None)

    def body(*refs):
        issue(refs[:n], refs[first_out], refs[first_out + 1], sem)
        refs[-1][...] = jnp.zeros_like(refs[-1])

    sems = pltpu.SemaphoreType.DMA((rows * per_row,))
    operands = [pltpu.with_memory_space_constraint(b, pltpu.HBM) for b in bufs] + ([] if after is None else [after])
    outs = pl.pallas_call(
        body, name=name,
        out_shape=(sems, sems) + tuple(pltpu.HBM(b.shape, b.dtype) for b in bufs) + (jax.ShapeDtypeStruct((8, LANE), F32),),
        in_specs=(_HBM,) * n + (_ANY,) * (after is not None),
        out_specs=(_SEM, _SEM) + (_HBM,) * n + (pl.BlockSpec(memory_space=pltpu.VMEM),),
        input_output_aliases={i: 2 + i for i in range(n)},
        compiler_params=pltpu.CompilerParams(has_side_effects=_EFFECT),
    )(*operands)
    return outs[0], outs[1], list(outs[2:2 + n]), outs[-1], sem_shape


def _split_wait(started, block_of, after, name, bufs=None, recvs=None, sends=None):
    send_sems, recv_sems, start_bufs, _, sem_shape = started
    bufs = start_bufs if bufs is None else bufs
    n = len(bufs)
    recvs = range(sem_shape[1]) if recvs is None else recvs
    sends = range(sem_shape[1]) if sends is None else sends
    after = tuple(after) if isinstance(after, (tuple, list)) else (after,)

    def body(*refs):
        x_, y_, c_ = lax.axis_index("x"), lax.axis_index("y"), lax.axis_index("c")
        for i in range(sem_shape[0]):
            blk = block_of(refs, i)
            for k in range(sem_shape[1]):
                cp = pltpu.make_async_remote_copy(
                    src_ref=blk, dst_ref=blk, send_sem=refs[n].at[i * sem_shape[1] + k], recv_sem=refs[n + 1].at[i * sem_shape[1] + k],
                    device_id=(x_, y_, c_), device_id_type=MESH)
                if k in sends:
                    cp.wait_send()
                if k in recvs:
                    cp.wait_recv()

    outs = pl.pallas_call(
        body, name=name,
        out_shape=tuple(pltpu.HBM(b.shape, b.dtype) for b in bufs),
        in_specs=(_HBM,) * n + (_SEM, _SEM) + (_ANY,) * len(after), out_specs=(_HBM,) * n,
        input_output_aliases={i: i for i in range(n)},
        compiler_params=pltpu.CompilerParams(has_side_effects=_EFFECT),
    )(*bufs, send_sems, recv_sems, *after)
    return list(outs)


def _first_window(cols):
    return lambda refs, i: _window(refs[i], 0, cols[i])


def _gather_prepare(blocks, me, name, cols=None):
    cols = cols or [None] * len(blocks)
    return [_own_block_only(b, me, "%s_own%d" % (name, i), c) for i, (b, c) in enumerate(zip(blocks, cols))]


def _gather_start(bufs, name, after=None, cols=None):
    cols = cols or [None] * len(bufs)

    def issue(buf_refs, send_sems, recv_sems, sem):
        x_, y_, c_ = lax.axis_index("x"), lax.axis_index("y"), lax.axis_index("c")
        me_idx = 4 * x_ + 2 * y_ + c_
        targets = [(x_, y_, 1 - c_), (1 - x_, y_, c_), (x_, 1 - y_, c_), (1 - x_, 1 - y_, c_)]
        for i, buf in enumerate(buf_refs):
            mine = _window(buf, me_idx, cols[i])
            for k, to in enumerate(targets):
                pltpu.make_async_remote_copy(
                    src_ref=mine, dst_ref=mine, send_sem=sem(send_sems, i, k), recv_sem=sem(recv_sems, i, k),
                    device_id=to, device_id_type=MESH).start()

    return _split_start(bufs, (len(bufs), 4), issue, name, after)


def _forward_start(bufs, name, after=None, cols=None, chips=(0, 1, 2)):
    cols = cols or [None] * len(bufs)

    def issue(buf_refs, send_sems, recv_sems, sem):
        x_, y_, c_ = lax.axis_index("x"), lax.axis_index("y"), lax.axis_index("c")
        places = [(1 - x_, y_), (x_, 1 - y_), (1 - x_, 1 - y_)]
        for i, buf in enumerate(buf_refs):
            for k, chip in enumerate(chips):
                px, py = places[chip]
                landed = _window(buf, 4 * px + 2 * py + c_, cols[i])
                pltpu.make_async_remote_copy(
                    src_ref=landed, dst_ref=landed, send_sem=sem(send_sems, i, k), recv_sem=sem(recv_sems, i, k),
                    device_id=(x_, y_, 1 - c_), device_id_type=MESH).start()

    return _split_start(bufs, (len(bufs), len(chips)), issue, name, after)


def _exchange_start(parts, me, name, cols=None):
    n = len(parts)
    cols = cols or [None] * n

    def issue(refs, send_sems, recv_sems, sem):
        x_, y_, c_ = lax.axis_index("x"), lax.axis_index("y"), lax.axis_index("c")
        me_idx = 4 * x_ + 2 * y_ + c_
        for i in range(n):
            for d in range(1, N_DEV):
                px = 1 - x_ if d & 4 else x_
                py = 1 - y_ if d & 2 else y_
                pc = 1 - c_ if d & 1 else c_
                pltpu.make_async_remote_copy(
                    src_ref=_window(refs[i], 4 * px + 2 * py + pc, cols[i]), dst_ref=refs[n + i].at[me_idx],
                    send_sem=sem(send_sems, i, d - 1), recv_sem=sem(recv_sems, i, d - 1),
                    device_id=(px, py, pc), device_id_type=MESH).start()

    lands = [_own_block_only(p, me, "%s_own%d" % (name, i), src_cols=c) for i, (p, c) in enumerate(zip(parts, cols))]
    return _split_start(list(parts) + lands, (n, N_DEV - 1), issue, name)


def _land_block(refs_offset):
    return lambda refs, i: refs[refs_offset + i].at[0]


def _matmul(a, b, *, name, ta=False, tb=False, res=None, out_dtype=BF16, after=None):
    (kdim, m) = a.shape if ta else a.shape[::-1]
    (n, kdim_b) = b.shape if tb else b.shape[::-1]
    assert kdim == kdim_b, (a.shape, b.shape, ta, tb)
    tm, tn = _pick(m), _pick(n)
    out_bytes = jnp.dtype(out_dtype).itemsize

    def vmem_bytes(tk):
        return 2 * (2 * tk * (tm + tn) + tm * tn * (out_bytes + (4 if res is not None else 0))) + (4 * tm * tn if tk < kdim else 0)

    fitting = [c for c in MATMUL_K_TILES if kdim % c == 0 and vmem_bytes(c) <= MATMUL_VMEM_BUDGET]
    tk = fitting[0] if fitting else kdim
    nk = kdim // tk
    a_spec = pl.BlockSpec((tk, tm), lambda i, j, k: (k, i)) if ta else pl.BlockSpec((tm, tk), lambda i, j, k: (i, k))
    b_spec = pl.BlockSpec((tn, tk), lambda i, j, k: (j, k)) if tb else pl.BlockSpec((tk, tn), lambda i, j, k: (k, j))
    o_spec = pl.BlockSpec((tm, tn), lambda i, j, k: (i, j))
    dims = (((0 if ta else 1,), (1 if tb else 0,)), ((), ()))

    def body(*refs):
        a_ref, b_ref = refs[:2]
        r_ref = None if res is None else refs[2]
        def finish(v):
            if r_ref is not None:
                v = r_ref[...] + v
            o_ref[...] = v.astype(out_dtype)

        part = lax.dot_general(a_ref[...], b_ref[...], dims, preferred_element_type=F32)
        if nk == 1:
            o_ref = refs[-1]
            finish(part)
            return
        o_ref, acc = refs[-2:]
        k = pl.program_id(2)

        @pl.when(k == 0)
        def _():
            acc[...] = part

        @pl.when(k > 0)
        def _():
            acc[...] += part

        @pl.when(k == nk - 1)
        def _():
            finish(acc[...])

    operands = [a, b] + ([] if res is None else [res]) + ([] if after is None else [after])
    in_specs = [a_spec, b_spec] + ([] if res is None else [o_spec]) + ([] if after is None else [_ANY])
    return pl.pallas_call(
        body, name=name, grid=(m // tm, n // tn, nk),
        out_shape=jax.ShapeDtypeStruct((m, n), out_dtype),
        in_specs=in_specs, out_specs=o_spec,
        scratch_shapes=[pltpu.VMEM((tm, tn), F32)] if nk > 1 else [],
        compiler_params=_params("parallel", "parallel", "arbitrary"),
    )(*operands)


def _accumulate(acc, part, k, nk, finish):
    @pl.when(k == 0)
    def _():
        acc[...] = part

    @pl.when(k > 0)
    def _():
        acc[...] += part

    @pl.when(k == nk - 1)
    def _():
        finish(acc[...])


def _mm_z_shard(h, w_sh, slot, z_sh, name, after=()):
    t, d = h.shape
    n = w_sh.shape[2]
    tm, tk = _pick(t), _pick(d, (2048, 1024, 512, 256, 128))
    nk = d // tk
    passed = list(after) + ([] if z_sh is None else [z_sh])

    def body(slot_ref, h_ref, w_ref, *rest):
        o_ref, acc = rest[-2:]

        def finish(v):
            o_ref[...] = v.astype(BF16)

        _accumulate(acc, jnp.dot(h_ref[...], w_ref[...], preferred_element_type=F32), pl.program_id(1), nk, finish)

    return pl.pallas_call(
        body, name=name, out_shape=jax.ShapeDtypeStruct((N_DEV, t, n), BF16),
        grid_spec=pltpu.PrefetchScalarGridSpec(
            num_scalar_prefetch=1, grid=(t // tm, nk),
            in_specs=[pl.BlockSpec((tm, tk), lambda i, k, s: (i, k)),
                      pl.BlockSpec((None, tk, n), lambda i, k, s: (s[0], k, 0))] + [_ANY] * len(passed),
            out_specs=pl.BlockSpec((None, tm, n), lambda i, k, s: (s[0], i, 0)),
            scratch_shapes=[pltpu.VMEM((tm, n), F32)]),
        input_output_aliases={} if z_sh is None else {2 + len(passed): 0},
        compiler_params=_params("parallel", "arbitrary"),
    )(jnp.reshape(slot, (1,)).astype(jnp.int32), h, w_sh, *passed)


def _mm_dw_shards(h, dz_sh, name):
    t, d = h.shape
    n = dz_sh.shape[2]
    tm, tk = _pick(d), _pick(t, (2048, 1024, 512, 256, 128))
    nk = t // tk

    def body(h_ref, dz_ref, o_ref, acc):
        def finish(v):
            o_ref[...] = v.astype(BF16)

        _accumulate(acc, lax.dot_general(h_ref[...], dz_ref[...], TN_DIMS, preferred_element_type=F32), pl.program_id(2), nk, finish)

    return pl.pallas_call(
        body, name=name, grid=(N_DEV, d // tm, nk), out_shape=jax.ShapeDtypeStruct((N_DEV, d, n), BF16),
        in_specs=[pl.BlockSpec((tk, tm), lambda s, i, k: (k, i)), pl.BlockSpec((None, tk, n), lambda s, i, k: (s, k, 0))],
        out_specs=pl.BlockSpec((None, tm, n), lambda s, i, k: (s, i, 0)),
        scratch_shapes=[pltpu.VMEM((tm, n), F32)],
        compiler_params=_params("parallel", "parallel", "arbitrary"))(h, dz_sh)


def _mm_dh_shards(dz_sh, w_sh, name, after):
    t = dz_sh.shape[1]
    d, n = w_sh.shape[1:]
    tm, tn = _pick(t), _pick(d)

    def body(dz_ref, w_ref, after_ref, o_ref, acc):
        def finish(v):
            o_ref[...] = v

        _accumulate(acc, lax.dot_general(dz_ref[...], w_ref[...], NT_DIMS, preferred_element_type=F32), pl.program_id(2), N_DEV, finish)

    return pl.pallas_call(
        body, name=name, grid=(t // tm, d // tn, N_DEV), out_shape=jax.ShapeDtypeStruct((t, d), F32),
        in_specs=[pl.BlockSpec((None, tm, n), lambda i, j, s: (s, i, 0)), pl.BlockSpec((None, tn, n), lambda i, j, s: (s, j, 0)), _ANY],
        out_specs=pl.BlockSpec((tm, tn), lambda i, j, s: (i, j)),
        scratch_shapes=[pltpu.VMEM((tm, tn), F32)],
        compiler_params=_params("parallel", "parallel", "arbitrary"))(dz_sh, w_sh, after)


def _rms_inv(x):
    return lax.rsqrt(jnp.mean(x * x, axis=-1, keepdims=True) + RMS_EPS)


def _rms_fwd(x, g, name, after=None):
    t, d = x.shape
    tm = _pick(t, (256, 128))

    def body(x_ref, g_ref, *rest):
        xv = x_ref[...]
        rest[-1][...] = (xv * _rms_inv(xv) * g_ref[...]).astype(BF16)

    return pl.pallas_call(
        body, name=name, grid=(t // tm,), out_shape=jax.ShapeDtypeStruct((t, d), BF16),
        in_specs=[_rb(tm, d), _vec(d)] + ([] if after is None else [_ANY]), out_specs=_rb(tm, d),
        compiler_params=_params("parallel"))(x, g, *([] if after is None else [after]))


def _rms_bwd_rows(dy, xv, g):
    inv = _rms_inv(xv)
    xhat = xv * inv
    dxhat = dy * g
    dx = inv * (dxhat - xhat * jnp.mean(dxhat * xhat, axis=-1, keepdims=True))
    return dx, dy * xhat


def _rms_bwd(dy, x, g, res, name):
    t, d = x.shape
    tm = _pick(t, (128,))

    def body(dy_ref, x_ref, g_ref, r_ref, dx_ref, dxb_ref, dg_ref):
        dx, dgrow = _rms_bwd_rows(dy_ref[...].astype(F32), x_ref[...], g_ref[...])
        dx = r_ref[...] + dx
        dx_ref[...] = dx
        dxb_ref[...] = dx.astype(BF16)

        @pl.when(pl.program_id(0) == 0)
        def _():
            dg_ref[...] = jnp.zeros_like(dg_ref)

        dg_ref[...] += jnp.sum(dgrow, axis=0, keepdims=True)

    return pl.pallas_call(
        body, name=name, grid=(t // tm,),
        out_shape=(jax.ShapeDtypeStruct((t, d), F32), jax.ShapeDtypeStruct((t, d), BF16), jax.ShapeDtypeStruct((1, d), F32)),
        in_specs=[_rb(tm, d), _rb(tm, d), _vec(d), _rb(tm, d)],
        out_specs=(_rb(tm, d), _rb(tm, d), _vec(d)), compiler_params=_params("arbitrary"))(dy, x, g, res)


def _latent_norm(z_all, g_q, g_kv, q_off, kv_off, name):
    t = z_all.shape[0]
    tm = _pick(t, (256, 128))

    def body(qa_ref, kva_ref, gq_ref, gkv_ref, qn_ref, kvn_ref):
        qa = qa_ref[...].astype(F32)
        qn_ref[...] = (qa * _rms_inv(qa) * gq_ref[...]).astype(BF16)
        kva = kva_ref[...].astype(F32)
        kvn_ref[...] = (kva * _rms_inv(kva) * gkv_ref[...]).astype(BF16)

    return pl.pallas_call(
        body, name=name, grid=(t // tm,),
        out_shape=(jax.ShapeDtypeStruct((t, Q_LORA), BF16), jax.ShapeDtypeStruct((t, KV_LORA), BF16)),
        in_specs=[_rb(tm, Q_LORA, q_off // Q_LORA), _rb(tm, KV_LORA, kv_off // KV_LORA), _vec(Q_LORA), _vec(KV_LORA)],
        out_specs=(_rb(tm, Q_LORA), _rb(tm, KV_LORA)), compiler_params=_params("parallel"))(z_all, z_all, g_q, g_kv)


def _latent_norm_bwd(dqn, dkvn, z_all, g_q, g_kv, q_off, kv_off, name):
    t = z_all.shape[0]
    tm = _pick(t, (256, 128))

    def body(dqn_ref, dkvn_ref, qa_ref, kva_ref, gq_ref, gkv_ref, dqa_ref, dkva_ref, dgq_ref, dgkv_ref):
        dqa, dgq = _rms_bwd_rows(dqn_ref[...].astype(F32), qa_ref[...].astype(F32), gq_ref[...])
        dkva, dgkv = _rms_bwd_rows(dkvn_ref[...].astype(F32), kva_ref[...].astype(F32), gkv_ref[...])
        dqa_ref[...] = dqa.astype(BF16)
        dkva_ref[...] = dkva.astype(BF16)

        @pl.when(pl.program_id(0) == 0)
        def _():
            dgq_ref[...] = jnp.zeros_like(dgq_ref)
            dgkv_ref[...] = jnp.zeros_like(dgkv_ref)

        dgq_ref[...] += jnp.sum(dgq, axis=0, keepdims=True)
        dgkv_ref[...] += jnp.sum(dgkv, axis=0, keepdims=True)

    return pl.pallas_call(
        body, name=name, grid=(t // tm,),
        out_shape=(jax.ShapeDtypeStruct((t, Q_LORA), BF16), jax.ShapeDtypeStruct((t, KV_LORA), BF16),
                   jax.ShapeDtypeStruct((1, Q_LORA), F32), jax.ShapeDtypeStruct((1, KV_LORA), F32)),
        in_specs=[_rb(tm, Q_LORA), _rb(tm, KV_LORA), _rb(tm, Q_LORA, q_off // Q_LORA), _rb(tm, KV_LORA, kv_off // KV_LORA),
                  _vec(Q_LORA), _vec(KV_LORA)],
        out_specs=(_rb(tm, Q_LORA), _rb(tm, KV_LORA), _vec(Q_LORA), _vec(KV_LORA)),
        compiler_params=_params("arbitrary"))(dqn, dkvn, z_all, z_all, g_q, g_kv)


def _rot(xv, cos_k, sin_a, sin_b, sign):
    return xv * cos_k + sign * (pltpu.roll(xv, LANE - 32, 1) * sin_a + pltpu.roll(xv, 32, 1) * sin_b)


def _rope_q(q_raw, tabs, n_heads, sign, out_dtype, name):
    t, w = q_raw.shape
    tm = _pick(t, (256, 128))

    def body(q_ref, cos_ref, sa_ref, sb_ref, o_ref):
        cos_k, sin_a, sin_b = cos_ref[...], sa_ref[...], sb_ref[...]
        for h in range(n_heads):
            lo = h * HEAD_PAD
            o_ref[:, lo:lo + LANE] = q_ref[:, lo:lo + LANE].astype(out_dtype)
            o_ref[:, lo + LANE:lo + HEAD_PAD] = _rot(
                q_ref[:, lo + LANE:lo + HEAD_PAD].astype(F32), cos_k, sin_a, sin_b, sign).astype(out_dtype)

    return pl.pallas_call(
        body, name=name, grid=(t // tm,), out_shape=jax.ShapeDtypeStruct((t, w), out_dtype),
        in_specs=[_rb(tm, w), _rb(tm, LANE), _rb(tm, LANE), _rb(tm, LANE)],
        out_specs=_rb(tm, w), compiler_params=_params("parallel"))(q_raw, *tabs)


def _rope_k(kv, z_all, tabs, n_heads, kr_off, name):
    t = kv.shape[0]
    tm = _pick(t, (256, 128))
    wk = n_heads * QK_NOPE

    def body(kn_ref, kr_ref, cos_ref, sa_ref, sb_ref, o_ref):
        krot = _rot(kr_ref[...].astype(F32), cos_ref[...], sa_ref[...], sb_ref[...], 1.0).astype(BF16)
        for h in range(n_heads):
            o_ref[:, h * HEAD_PAD:h * HEAD_PAD + LANE] = kn_ref[:, h * QK_NOPE:(h + 1) * QK_NOPE]
            o_ref[:, h * HEAD_PAD + LANE:(h + 1) * HEAD_PAD] = krot

    return pl.pallas_call(
        body, name=name, grid=(t // tm,), out_shape=jax.ShapeDtypeStruct((t, n_heads * HEAD_PAD), BF16),
        in_specs=[_rb(tm, wk), _rb(tm, LANE, kr_off // LANE), _rb(tm, LANE), _rb(tm, LANE), _rb(tm, LANE)],
        out_specs=_rb(tm, n_heads * HEAD_PAD), compiler_params=_params("parallel"))(kv, z_all, *tabs)


def _rope_k_bwd(dk_pad, dv, tabs, n_heads, name):
    t = dk_pad.shape[0]
    tm = _pick(t, (256, 128))
    wk = n_heads * QK_NOPE

    def body(dk_ref, dv_ref, cos_ref, sa_ref, sb_ref, dkv_ref, dkr_ref):
        acc = dk_ref[:, LANE:HEAD_PAD]
        dkv_ref[:, 0:QK_NOPE] = dk_ref[:, 0:LANE].astype(BF16)
        for h in range(1, n_heads):
            acc = acc + dk_ref[:, h * HEAD_PAD + LANE:(h + 1) * HEAD_PAD]
            dkv_ref[:, h * QK_NOPE:(h + 1) * QK_NOPE] = dk_ref[:, h * HEAD_PAD:h * HEAD_PAD + LANE].astype(BF16)
        dkv_ref[:, wk:] = dv_ref[...].astype(BF16)
        dkr_ref[...] = _rot(acc, cos_ref[...], sa_ref[...], sb_ref[...], -1.0).astype(BF16)

    return pl.pallas_call(
        body, name=name, grid=(t // tm,),
        out_shape=(jax.ShapeDtypeStruct((t, 2 * wk), BF16), jax.ShapeDtypeStruct((t, LANE), BF16)),
        in_specs=[_rb(tm, n_heads * HEAD_PAD), _rb(tm, wk), _rb(tm, LANE), _rb(tm, LANE), _rb(tm, LANE)],
        out_specs=(_rb(tm, 2 * wk), _rb(tm, LANE)), compiler_params=_params("parallel"))(dk_pad, dv, *tabs)


NT_DIMS = (((1,), (1,)), ((), ()))
TN_DIMS = (((0,), (0,)), ((), ()))


def _softmax_parts(q, k):
    s = lax.dot_general(q, k, NT_DIMS, preferred_element_type=F32)
    e = jnp.exp2((s - jnp.max(s, axis=-1, keepdims=True)) * (SOFTMAX_SCALE * math.log2(math.e)))
    return e, 1.0 / jnp.sum(e, axis=-1, keepdims=True)


def _attn_fwd(q_pad, k_pad, kv, n_heads, name):
    t = q_pad.shape[0]
    tq = _pick(t, (256, 128))

    def body(q_ref, k_ref, v_ref, o_ref):
        e, inv_l = _softmax_parts(q_ref[...], k_ref[...])
        o_ref[...] = (jnp.dot(e.astype(BF16), v_ref[...], preferred_element_type=F32) * inv_l).astype(BF16)

    return pl.pallas_call(
        body, name=name, grid=(n_heads, t // tq),
        out_shape=jax.ShapeDtypeStruct((t, n_heads * V_HEAD), BF16),
        in_specs=[pl.BlockSpec((tq, HEAD_PAD), lambda h, i: (i, h)),
                  pl.BlockSpec((t, HEAD_PAD), lambda h, i: (0, h)),
                  pl.BlockSpec((t, V_HEAD), lambda h, i: (0, n_heads + h))],
        out_specs=pl.BlockSpec((tq, V_HEAD), lambda h, i: (i, h)),
        compiler_params=_params("parallel", "parallel"))(q_pad, k_pad, kv)


def _attn_bwd(q_pad, k_pad, kv, do, n_heads, name):
    t = q_pad.shape[0]
    tq = _pick(t, (256, 128))
    nq = t // tq

    def body(q_ref, k_ref, v_ref, do_ref, dq_ref, dk_ref, dv_ref):
        @pl.when(pl.program_id(1) == 0)
        def _():
            dk_ref[...] = jnp.zeros_like(dk_ref)
            dv_ref[...] = jnp.zeros_like(dv_ref)

        q, k, dout = q_ref[...], k_ref[...], do_ref[...]
        e, inv_l = _softmax_parts(q, k)
        dp = lax.dot_general(dout, v_ref[...], NT_DIMS, preferred_element_type=F32)
        p_dot_dp = inv_l * jnp.sum(e * dp, axis=-1, keepdims=True)
        ds = (e * ((dp - p_dot_dp) * (SOFTMAX_SCALE * inv_l))).astype(BF16)
        dq_ref[...] = jnp.dot(ds, k, preferred_element_type=F32)
        dk_ref[...] += lax.dot_general(ds, q, TN_DIMS, preferred_element_type=F32)
        dv_ref[...] += lax.dot_general(e.astype(BF16), (dout.astype(F32) * inv_l).astype(BF16), TN_DIMS, preferred_element_type=F32)

    return pl.pallas_call(
        body, name=name, grid=(n_heads, nq),
        out_shape=(jax.ShapeDtypeStruct((t, n_heads * HEAD_PAD), F32), jax.ShapeDtypeStruct((t, n_heads * HEAD_PAD), F32),
                   jax.ShapeDtypeStruct((t, n_heads * V_HEAD), F32)),
        in_specs=[pl.BlockSpec((tq, HEAD_PAD), lambda h, i: (i, h)),
                  pl.BlockSpec((t, HEAD_PAD), lambda h, i: (0, h)),
                  pl.BlockSpec((t, V_HEAD), lambda h, i: (0, n_heads + h)),
                  pl.BlockSpec((tq, V_HEAD), lambda h, i: (i, h))],
        out_specs=(pl.BlockSpec((tq, HEAD_PAD), lambda h, i: (i, h)),
                   pl.BlockSpec((t, HEAD_PAD), lambda h, i: (0, h)),
                   pl.BlockSpec((t, V_HEAD), lambda h, i: (0, h))),
        compiler_params=_params("parallel", "arbitrary"))(q_pad, k_pad, kv, do)


def _shift_rows(u, t):
    row = lax.broadcasted_iota(jnp.int32, u.shape, 0)
    prev = jnp.where(row == 0, 0.0, pltpu.roll(u, 1, 0))
    nxt = jnp.where(row == t - 1, 0.0, pltpu.roll(u, t - 1, 0))
    return prev, nxt


def _conv_fwd(z_all, conv_w, cc, name):
    t = z_all.shape[0]
    nb = cc // LANE

    def body(cb_ref, cc_ref, ch_ref, w_ref, y_ref):
        u = cc_ref[...].astype(F32) * ch_ref[...].astype(F32)
        prev, nxt = _shift_rows(u, t)
        w = w_ref[...]
        conv = prev * w[0:1, :] + u * w[1:2, :] + nxt * w[2:3, :]
        y_ref[...] = (cb_ref[...].astype(F32) * conv).astype(BF16)

    col = lambda g: pl.BlockSpec((t, LANE), lambda j: (0, g * nb + j))
    return pl.pallas_call(
        body, name=name, grid=(nb,), out_shape=jax.ShapeDtypeStruct((t, cc), BF16),
        in_specs=[col(0), col(1), col(2), pl.BlockSpec((3, LANE), lambda j: (0, j))],
        out_specs=pl.BlockSpec((t, LANE), lambda j: (0, j)),
        compiler_params=_params("parallel"))(z_all, z_all, z_all, conv_w)


def _conv_bwd(dy, z_all, conv_w, cc, name):
    t = z_all.shape[0]
    nb = cc // LANE

    def body(dy_ref, cb_ref, cc_ref, ch_ref, w_ref, dcb_ref, dcc_ref, dch_ref, dw_ref):
        c_c, c_h = cc_ref[...].astype(F32), ch_ref[...].astype(F32)
        u = c_c * c_h
        prev, nxt = _shift_rows(u, t)
        w = w_ref[...]
        dyv = dy_ref[...].astype(F32)
        dcb_ref[...] = (dyv * (prev * w[0:1, :] + u * w[1:2, :] + nxt * w[2:3, :])).astype(BF16)
        dconv = dyv * cb_ref[...].astype(F32)
        dw_ref[0:1, :] = jnp.sum(dconv * prev, axis=0, keepdims=True)
        dw_ref[1:2, :] = jnp.sum(dconv * u, axis=0, keepdims=True)
        dw_ref[2:3, :] = jnp.sum(dconv * nxt, axis=0, keepdims=True)
        dprev, dnxt = _shift_rows(dconv, t)
        du = dnxt * w[0:1, :] + dconv * w[1:2, :] + dprev * w[2:3, :]
        dcc_ref[...] = (du * c_h).astype(BF16)
        dch_ref[...] = (du * c_c).astype(BF16)

    col = lambda g: pl.BlockSpec((t, LANE), lambda j: (0, g * nb + j))
    one = pl.BlockSpec((t, LANE), lambda j: (0, j))
    wsp = pl.BlockSpec((3, LANE), lambda j: (0, j))
    act = jax.ShapeDtypeStruct((t, cc), BF16)
    return pl.pallas_call(
        body, name=name, grid=(nb,),
        out_shape=(act, act, act, jax.ShapeDtypeStruct((3, cc), F32)),
        in_specs=[one, col(0), col(1), col(2), wsp],
        out_specs=(one, one, one, wsp),
        compiler_params=_params("parallel"))(dy, z_all, z_all, z_all, conv_w)


def _sigmoid(v):
    return 1.0 / (1.0 + jnp.exp(-v))


def _merge_fwd(z_all, b_gate, y_a, y_b, gate_off, name):
    t, d = y_a.shape
    tm = _pick(t, (128,))
    gb = gate_off // d

    def body(za_ref, zb_ref, ba_ref, bb_ref, ya_ref, yb_ref, m_ref):
        ga = _sigmoid(za_ref[...].astype(F32) + ba_ref[...])
        gbv = _sigmoid(zb_ref[...].astype(F32) + bb_ref[...])
        m_ref[...] = (ga * ya_ref[...].astype(F32) + gbv * yb_ref[...].astype(F32)).astype(BF16)

    return pl.pallas_call(
        body, name=name, grid=(t // tm,), out_shape=jax.ShapeDtypeStruct((t, d), BF16),
        in_specs=[_rb(tm, d, gb), _rb(tm, d, gb + 1), _vec(d, 0), _vec(d, 1), _rb(tm, d), _rb(tm, d)],
        out_specs=_rb(tm, d), compiler_params=_params("parallel"))(z_all, z_all, b_gate, b_gate, y_a, y_b)


def _merge_bwd(dm, z_all, b_gate, y_a, y_b, gate_off, name):
    t, d = y_a.shape
    tm = _pick(t, (128,))
    gb = gate_off // d

    def body(dm_ref, za_ref, zb_ref, ba_ref, bb_ref, ya_ref, yb_ref, dya_ref, dyb_ref, dzg_ref, db_ref):
        dmv = dm_ref[...].astype(F32)
        ga = _sigmoid(za_ref[...].astype(F32) + ba_ref[...])
        gbv = _sigmoid(zb_ref[...].astype(F32) + bb_ref[...])
        dya_ref[...] = (dmv * ga).astype(BF16)
        dyb_ref[...] = (dmv * gbv).astype(BF16)
        dza = dmv * ya_ref[...].astype(F32) * (ga * (1.0 - ga))
        dzb = dmv * yb_ref[...].astype(F32) * (gbv * (1.0 - gbv))
        dzg_ref[:, 0:d] = dza.astype(BF16)
        dzg_ref[:, d:2 * d] = dzb.astype(BF16)

        @pl.when(pl.program_id(0) == 0)
        def _():
            db_ref[...] = jnp.zeros_like(db_ref)

        db_ref[:, 0:d] += jnp.sum(dza, axis=0, keepdims=True)
        db_ref[:, d:2 * d] += jnp.sum(dzb, axis=0, keepdims=True)

    act = jax.ShapeDtypeStruct((t, d), BF16)
    return pl.pallas_call(
        body, name=name, grid=(t // tm,),
        out_shape=(act, act, jax.ShapeDtypeStruct((t, 2 * d), BF16), jax.ShapeDtypeStruct((1, 2 * d), F32)),
        in_specs=[_rb(tm, d), _rb(tm, d, gb), _rb(tm, d, gb + 1), _vec(d, 0), _vec(d, 1), _rb(tm, d), _rb(tm, d)],
        out_specs=(_rb(tm, d), _rb(tm, d), _rb(tm, 2 * d), _vec(2 * d)),
        compiler_params=_params("arbitrary"))(dm, z_all, z_all, b_gate, b_gate, y_a, y_b)


def _swiglu_fwd(gate, up, name, after):
    t, f = gate.shape
    tm = _pick(t, (128,))

    def body(g_ref, u_ref, after_ref, a_ref):
        g = g_ref[...].astype(F32)
        a_ref[...] = (g * _sigmoid(g) * u_ref[...].astype(F32)).astype(BF16)

    return pl.pallas_call(
        body, name=name, grid=(t // tm,), out_shape=jax.ShapeDtypeStruct((t, f), BF16),
        in_specs=[_rb(tm, f), _rb(tm, f), _ANY], out_specs=_rb(tm, f), compiler_params=_params("parallel"))(gate, up, after)


def _swiglu_bwd(dact, gate, up, name):
    t, f = gate.shape
    tm = _pick(t, (128,))

    def body(da_ref, g_ref, u_ref, dg_ref, du_ref):
        g, da = g_ref[...].astype(F32), da_ref[...].astype(F32)
        sg = _sigmoid(g)
        dg_ref[...] = (da * u_ref[...].astype(F32) * (sg * (1.0 + g * (1.0 - sg)))).astype(BF16)
        du_ref[...] = (da * (g * sg)).astype(BF16)

    act = jax.ShapeDtypeStruct((t, f), BF16)
    return pl.pallas_call(
        body, name=name, grid=(t // tm,), out_shape=(act, act),
        in_specs=[_rb(tm, f)] * 3, out_specs=(_rb(tm, f), _rb(tm, f)), compiler_params=_params("parallel"))(dact, gate, up)


def _loss_head(x2, target, g, name):
    t, d = x2.shape
    tm = _pick(t, (128,))

    def body(x_ref, t_ref, g_ref, loss_ref, dx_ref, dxb_ref, dg_ref):
        xv, gv = x_ref[...], g_ref[...]
        err = xv * _rms_inv(xv) * gv - t_ref[...]
        dx, dgrow = _rms_bwd_rows(err * (1.0 / d), xv, gv)
        dx_ref[...] = dx
        dxb_ref[...] = dx.astype(BF16)

        @pl.when(pl.program_id(0) == 0)
        def _():
            loss_ref[...] = jnp.zeros_like(loss_ref)
            dg_ref[...] = jnp.zeros_like(dg_ref)

        loss_ref[...] += (0.5 / d) * jnp.sum(jnp.sum(err * err, axis=1, keepdims=True), axis=0, keepdims=True)
        dg_ref[...] += jnp.sum(dgrow, axis=0, keepdims=True)

    return pl.pallas_call(
        body, name=name, grid=(t // tm,),
        out_shape=(jax.ShapeDtypeStruct((1, 1), F32), jax.ShapeDtypeStruct((t, d), F32),
                   jax.ShapeDtypeStruct((t, d), BF16), jax.ShapeDtypeStruct((1, d), F32)),
        in_specs=[_rb(tm, d), _rb(tm, d), _vec(d)],
        out_specs=(pl.BlockSpec((1, 1), lambda i: (0, 0)), _rb(tm, d), _rb(tm, d), _vec(d)),
        compiler_params=_params("arbitrary"))(x2, target, g)


def _adamw(parts, w, m, v, name):
    r, c = w.shape
    cp = parts.shape[2]
    assert parts.shape[1] >= r and cp >= c
    tr = r if r * c <= ADAM_BLOCK_ELEMS else _pick(r, tuple(s for s in (512, 256, 128, 64, 32, 16, 8) if s * c <= ADAM_BLOCK_ELEMS))
    assert tr == parts.shape[1] or tr % 8 == 0

    def body(p_ref, w_ref, m_ref, v_ref, g_ref, d_ref, nm_ref, nv_ref):
        g = p_ref[0, :, 0:c].astype(F32)
        for s in range(1, N_DEV):
            g = g + p_ref[s, :, 0:c].astype(F32)
        nm = ADAM_B1 * m_ref[...] + (1.0 - ADAM_B1) * g
        nv = ADAM_B2 * v_ref[...] + (1.0 - ADAM_B2) * (g * g)
        m_hat = nm / (1.0 - ADAM_B1 ** ADAM_STEP)
        v_hat = nv / (1.0 - ADAM_B2 ** ADAM_STEP)
        g_ref[...] = g
        d_ref[...] = -ADAM_LR * (m_hat / (jnp.sqrt(v_hat) + ADAM_EPS) + ADAM_WD * w_ref[...])
        nm_ref[...] = nm
        nv_ref[...] = nv

    blk = pl.BlockSpec((tr, c), lambda i: (i, 0))
    out = jax.ShapeDtypeStruct((r, c), F32)
    return pl.pallas_call(
        body, name=name, grid=(r // tr,), out_shape=(out, out, out, out),
        in_specs=[pl.BlockSpec((N_DEV, tr, cp), lambda i: (0, i, 0)), blk, blk, blk],
        out_specs=(blk, blk, blk, blk), compiler_params=_params("parallel"))(parts, w, m, v)


def _cols_of(g):
    return jnp.transpose(g, (1, 0, 2)).reshape(g.shape[1], N_DEV * g.shape[2])


def _cols_slice(pieces, lo, hi):
    out, off = [], 0
    for p in pieces:
        a, b = max(lo, off), min(hi, off + p.shape[1])
        if a < b:
            out.append(p[:, a - off:b - off])
        off += p.shape[1]
    return out[0] if len(out) == 1 else jnp.concatenate(out, axis=1)


def _col_parts(dw):
    k, n8 = dw.shape
    return jnp.transpose(dw.reshape(k, N_DEV, n8 // N_DEV), (1, 0, 2))


def kernel(x, positions, g_mix, w_in, b_gate, conv_w, g_q_a, w_q_b, g_kv_a, w_kv_b, w_branch, w_out, g_ffn, w_ffn_gate, w_ffn_up, w_ffn_down, g_final, loss_target, m_g_mix, m_w_in, m_b_gate, m_conv_w, m_g_q_a, m_w_q_b, m_g_kv_a, m_w_kv_b, m_w_branch, m_w_out, m_g_ffn, m_w_ffn_gate, m_w_ffn_up, m_w_ffn_down, m_g_final, v_g_mix, v_w_in, v_b_gate, v_conv_w, v_g_q_a, v_w_q_b, v_g_kv_a, v_w_kv_b, v_w_branch, v_w_out, v_g_ffn, v_w_ffn_gate, v_w_ffn_up, v_w_ffn_down, v_g_final):
    given = dict(locals())
    xs = x[0]
    t, d = xs.shape
    cc = d // 2
    n_heads = cc // V_HEAD
    in_cols = N_DEV * w_in.shape[2]
    q_off, kv_off, kr_off = 3 * cc, 3 * cc + Q_LORA, 3 * cc + Q_LORA + KV_LORA
    head_cols = kr_off + QK_ROPE
    head_pad = -(-(kr_off + LANE) // 1024) * 1024
    assert in_cols == head_cols + 2 * d and q_off % Q_LORA == 0 and kv_off % KV_LORA == 0 and kr_off % LANE == 0
    fs = w_ffn_gate.shape[2]
    fsp = -(-fs // LANE) * LANE
    ffp = N_DEV * fsp
    me = _me_index()
    bf = lambda a: a.astype(BF16)
    one_slot = lambda refs, i: refs[i].at[0]

    x_, y_, c_ = lax.axis_index("x"), lax.axis_index("y"), lax.axis_index("c")
    place = lambda px, py, pc: 4 * px + 2 * py + pc
    chips = [(1 - x_, y_), (x_, 1 - y_), (1 - x_, 1 - y_)]
    mix_cols = [w_q_b.shape[2], w_kv_b.shape[2], w_branch.shape[3], None]
    ffn_cols = [fsp, fsp]
    g_in = _gather_start(_gather_prepare([bf(w_in[0]), conv_w[0]], me, "ag1_start_in"), "ag1_start_in")
    zero = g_in[3][0, 0]
    bf_later = lambda a: (a + zero).astype(BF16)
    h = _rms_fwd(xs, g_mix, "rms_mix", after=g_in[3])
    bufs = g_in[2]
    z_sh = _mm_z_shard(h, bufs[0], me, None, "mm_z_own")
    bufs = _split_wait(g_in, one_slot, z_sh, "ag1_wait_in_sib", bufs=bufs, recvs=(0,), sends=())
    z_sh = _mm_z_shard(h, bufs[0], place(x_, y_, 1 - c_), z_sh, "mm_z_sib")
    mix_bufs = _gather_prepare([bf_later(w_q_b[0]), bf_later(w_kv_b[0]), bf_later(w_branch[0].reshape(2 * cc, -1)), bf_later(w_out[0])],
                               me, "ag1_start_mix", mix_cols)
    gate_up_bufs = _gather_prepare([bf_later(jnp.pad(w_ffn_gate[0], ((0, 0), (0, fsp - fs)))),
                                    bf_later(jnp.pad(w_ffn_up[0], ((0, 0), (0, fsp - fs))))], me, "ag1_start_gate_up", ffn_cols)
    down_bufs = _gather_prepare([bf_later(jnp.pad(w_ffn_down[0], ((0, fsp - fs), (0, 0))))], me, "ag1_start_down")
    prepared = tuple(mix_bufs + gate_up_bufs + down_bufs)
    passed_on = []
    for j, (px, py) in enumerate(chips):
        last = j == len(chips) - 1
        bufs = _split_wait(g_in, one_slot, (z_sh,) + (prepared if j == 0 else ()), "ag1_wait_in_chip%d" % j, bufs=bufs,
                           recvs=(1 + j,), sends=range(4) if last else ())
        f = _forward_start(bufs, "ag2_start_in_chip%d" % j, chips=(j,))
        passed_on.append(f)
        bufs, tokens = f[2], (f[3],)
        if last:
            g_mix_w = _gather_start(mix_bufs, "ag1_start_mix", after=f[3], cols=mix_cols)
            tokens = (g_mix_w[3],)
        z_sh = _mm_z_shard(h, bufs[0], place(px, py, c_), z_sh, "mm_z_chip%d" % j, after=tokens)
        for jj in ((0, 1) if j == 1 else (2,) if last else ()):
            bufs = _split_wait(passed_on[jj], one_slot, z_sh, "ag2_wait_in_chip%d" % jj, bufs=bufs)
            z_sh = _mm_z_shard(h, bufs[0], place(*chips[jj], 1 - c_), z_sh, "mm_z_chip%d_sib" % jj)
    w_in_g, cw_g = bufs
    cw = _cols_of(cw_g)
    z_full = jnp.transpose(z_sh, (1, 0, 2)).reshape(t, in_cols)
    z_head = jnp.concatenate([z_full[:, :head_cols], jnp.zeros((t, head_pad - head_cols), BF16)], axis=1)
    zg = z_full[:, head_cols:]

    inv_freq = ROPE_THETA ** (-jnp.arange(0, QK_ROPE, 2, dtype=F32) / QK_ROPE)
    ang = positions[0].astype(F32)[:, None] * inv_freq[None, :]
    cos, sin = jnp.cos(ang), jnp.sin(ang)
    z32, z64 = jnp.zeros((t, 32), F32), jnp.zeros((t, 64), F32)
    tabs = (jnp.concatenate([cos, cos, jnp.ones((t, 64), F32)], axis=1),
            jnp.concatenate([-sin, z32, z64], axis=1),
            jnp.concatenate([z32, sin, z64], axis=1))

    f_mix = _forward_start(_split_wait(g_mix_w, _first_window(mix_cols), zg, "ag1_wait_mix"), "ag2_start_mix", cols=mix_cols)
    g_gate_up = _gather_start(gate_up_bufs, "ag1_start_gate_up", after=f_mix[3], cols=ffn_cols)
    y_a = _conv_fwd(z_head, cw, cc, "conv_fwd")
    qn, kvn = _latent_norm(z_head, g_q_a, g_kv_a, q_off, kv_off, "latent_norm")
    wq_full, wkv_full, wbr, wo_g = _split_wait(f_mix, _first_window(mix_cols), g_gate_up[3], "ag2_wait_mix")
    wq = wq_full.reshape(Q_LORA, n_heads, QK_NOPE + QK_ROPE)
    wq_pad = jnp.pad(wq, ((0, 0), (0, 0), (0, HEAD_PAD - QK_NOPE - QK_ROPE))).reshape(Q_LORA, n_heads * HEAD_PAD)
    wkv = wkv_full.reshape(KV_LORA, n_heads, 2, QK_NOPE)
    wkv_perm = jnp.transpose(wkv, (0, 2, 1, 3)).reshape(KV_LORA, 2 * n_heads * QK_NOPE)
    wb_a, wb_b = wbr[:cc], wbr[cc:]
    wo = wo_g.reshape(d, d)
    q_pad = _rope_q(_matmul(qn, wq_pad, name="mm_q", out_dtype=F32), tabs, n_heads, 1.0, BF16, "rope_q")
    kv = _matmul(kvn, wkv_perm, name="mm_kv")
    k_pad = _rope_k(kv, z_head, tabs, n_heads, kr_off, "rope_k")
    y_b = _attn_fwd(q_pad, k_pad, kv, n_heads, "attn_fwd")
    f_gate_up = _forward_start(_split_wait(g_gate_up, _first_window(ffn_cols), y_b, "ag1_wait_gate_up"), "ag2_start_gate_up", cols=ffn_cols)
    g_down = _gather_start(down_bufs, "ag1_start_down", after=f_gate_up[3])
    ybr_a = _matmul(y_a, wb_a, name="mm_br_a", after=g_down[3])
    ybr_b = _matmul(y_b, wb_b, name="mm_br_b")
    merged = _merge_fwd(zg, b_gate, ybr_a, ybr_b, 0, "merge_fwd")
    x1 = _matmul(merged, wo, name="mm_out", res=xs, out_dtype=F32)
    h2 = _rms_fwd(x1, g_ffn, "rms_ffn")
    wg, wu = _split_wait(f_gate_up, _first_window(ffn_cols), h2, "ag2_wait_gate_up")
    gate = _matmul(h2, wg, name="mm_gate")
    up = _matmul(h2, wu, name="mm_up")
    f_down = _forward_start(_split_wait(g_down, one_slot, up, "ag1_wait_down"), "ag2_start_down")
    act = _swiglu_fwd(gate, up, "swiglu_fwd", after=f_down[3])
    wd = _split_wait(f_down, one_slot, act, "ag2_wait_down")[0].reshape(ffp, d)
    x2 = _matmul(act, wd, name="mm_down", res=x1, out_dtype=F32)
    loss_part, dx2, dx2b, dg_final = _loss_head(x2, loss_target[0], g_final.reshape(1, d), "loss_head")

    dact = _matmul(dx2b, wd, tb=True, name="mm_d_act")
    dwd = _matmul(act, dx2b, ta=True, name="mm_dw_down")
    r_down = _exchange_start([dwd.reshape(N_DEV, fsp, d)], me, "rs_start_down")
    dgate, dup = _swiglu_bwd(dact, gate, up, "swiglu_bwd")
    dwg = _matmul(h2, dgate, ta=True, name="mm_dw_gate", after=r_down[3])
    dwu = _matmul(h2, dup, ta=True, name="mm_dw_up")
    r_gate_up = _exchange_start([dwg, dwu], me, "rs_start_gate_up", cols=[fsp, fsp])
    dh2 = _matmul(dgate, wg, tb=True, name="mm_d_h2_gate", out_dtype=F32, after=r_gate_up[3])
    dh2 = _matmul(dup, wu, tb=True, name="mm_d_h2_up", res=dh2, out_dtype=F32)
    dx1, dx1b, dg_ffn = _rms_bwd(dh2, x1, g_ffn, dx2, "rms_ffn_bwd")
    dwo = _matmul(merged, dx1b, ta=True, name="mm_dw_out")
    r_out = _exchange_start([dwo.reshape(N_DEV, d // N_DEV, d)], me, "rs_start_out")
    dmerged = _matmul(dx1b, wo, tb=True, name="mm_d_merged", after=r_out[3])
    dybr_a, dybr_b, dzg, db_gate = _merge_bwd(dmerged, zg, b_gate, ybr_a, ybr_b, 0, "merge_bwd")
    dwb_a = _matmul(y_a, dybr_a, ta=True, name="mm_dw_br_a")
    dwb_b = _matmul(y_b, dybr_b, ta=True, name="mm_dw_br_b")
    r_br = _exchange_start([dwb_a, dwb_b], me, "rs_start_branch", cols=[d // N_DEV] * 2)
    dy_a = _matmul(dybr_a, wb_a, tb=True, name="mm_d_y_a", after=r_br[3])
    dy_b = _matmul(dybr_b, wb_b, tb=True, name="mm_d_y_b")
    dq_pad, dk_pad, dv = _attn_bwd(q_pad, k_pad, kv, dy_b, n_heads, "attn_bwd")
    dq_raw = _rope_q(dq_pad, tabs, n_heads, -1.0, BF16, "rope_q_bwd")
    dkv, dkr = _rope_k_bwd(dk_pad, dv, tabs, n_heads, "rope_k_bwd")
    dwq = _matmul(qn, dq_raw, ta=True, name="mm_dw_q")
    dwkv = _matmul(kvn, dkv, ta=True, name="mm_dw_kv")
    dwq_full = dwq.reshape(Q_LORA, n_heads, HEAD_PAD)[:, :, :QK_NOPE + QK_ROPE].reshape(Q_LORA, -1)
    dwkv_full = jnp.transpose(dwkv.reshape(KV_LORA, 2, n_heads, QK_NOPE), (0, 2, 1, 3)).reshape(KV_LORA, -1)
    r_qkv = _exchange_start([_col_parts(dwq_full), _col_parts(dwkv_full)], me, "rs_start_q_kv")
    dqn = _matmul(dq_raw, wq_pad, tb=True, name="mm_d_qn", after=r_qkv[3])
    dkvn = _matmul(dkv, wkv_perm, tb=True, name="mm_d_kvn")
    dqa, dkva, dg_q, dg_kv = _latent_norm_bwd(dqn, dkvn, z_head, g_q_a, g_kv_a, q_off, kv_off, "latent_norm_bwd")
    dcb, dcc, dch, dcw = _conv_bwd(dy_a, z_head, cw, cc, "conv_bwd")
    dz_pieces = [dcb, dcc, dch, dqa, dkva, dkr[:, :QK_ROPE], dzg]
    n_sh = in_cols // N_DEV
    dz_sh = jnp.stack([_cols_slice(dz_pieces, s * n_sh, (s + 1) * n_sh) for s in range(N_DEV)], axis=0)
    r_in = _exchange_start([_mm_dw_shards(h, dz_sh, "mm_dw_in")], me, "rs_start_in")
    dh = _mm_dh_shards(dz_sh, w_in_g, "mm_d_h", r_in[3])
    dx, _, dg_mix = _rms_bwd(dh, xs, g_mix, dx1, "rms_mix_bwd")

    results = {}
    last = dx

    def update(started, targets):
        nonlocal last
        n = len(targets)
        recvs = _split_wait(started, _land_block(n), last, "rs_wait_" + targets[0][0])[n:]
        for recv, (key, w_, m_, v_) in zip(recvs, targets):
            results[key] = _adamw(recv, w_, m_, v_, "adamw_" + key)
            last = results[key][0]

    shard = lambda wname: (wname,) + tuple(given[p + wname][0] for p in ("", "m_", "v_"))
    branch = lambda b: ("w_branch_%d" % b,) + tuple(given[p + "w_branch"][0, b] for p in ("", "m_", "v_"))
    update(r_down, [shard("w_ffn_down")])
    update(r_gate_up, [shard("w_ffn_gate"), shard("w_ffn_up")])
    update(r_out, [shard("w_out")])
    update(r_br, [branch(0), branch(1)])
    results["w_branch"] = tuple(jnp.stack([results["w_branch_0"][k], results["w_branch_1"][k]]) for k in range(4))
    update(r_qkv, [shard("w_q_b"), shard("w_kv_b")])

    small = [("g_mix", dg_mix), ("b_gate", db_gate), ("g_q_a", dg_q), ("g_kv_a", dg_kv), ("g_ffn", dg_ffn),
             ("g_final", dg_final), ("conv_w", dcw.reshape(1, 3 * cc))]
    packed = _all_gather(jnp.concatenate([p for _, p in small], axis=1), "ag_small_grads", last)
    off = 0
    for wname, p in small:
        n = p.shape[1]
        parts = packed[:, :, off:off + n]
        off += n
        if wname == "conv_w":
            width = conv_w.shape[2]
            parts = lax.dynamic_slice_in_dim(parts.reshape(N_DEV, 3, cc), me * width, width, axis=2)
            flat = (3, width)
        else:
            flat = (1, n)
        results[wname] = _adamw(parts, given[wname].reshape(flat), given["m_" + wname].reshape(flat),
                                given["v_" + wname].reshape(flat), "adamw_" + wname)
        last = results[wname][0]
    update(r_in, [shard("w_in")])

    loss = lax.psum(loss_part[0, 0], MESH_AXES)
    order = ["g_mix", "w_in", "b_gate", "conv_w", "g_q_a", "w_q_b", "g_kv_a", "w_kv_b", "w_branch", "w_out", "g_ffn",
             "w_ffn_gate", "w_ffn_up", "w_ffn_down", "g_final"]
    out = [loss, dx[None]]
    for k in range(4):
        out += [results[n][k].reshape(given[n].shape) for n in order]
    return tuple(out)
```

```python
import functools
import math

import jax
import jax.numpy as jnp
from jax import lax
from jax.experimental import pallas as pl
from jax.experimental.pallas import tpu as pltpu

F32 = jnp.float32
BF16 = jnp.bfloat16
N_DEV = 8
MESH_AXES = ("x", "y", "c")
MESH = pl.DeviceIdType.MESH

QK_NOPE = 128
QK_ROPE = 64
V_HEAD = 128
HEAD_PAD = 256
Q_LORA = 1024
KV_LORA = 512
ROPE_THETA = 10000.0
RMS_EPS = 1e-6
SOFTMAX_SCALE = 1.0 / math.sqrt(QK_NOPE + QK_ROPE)
ADAM_LR, ADAM_B1, ADAM_B2, ADAM_EPS, ADAM_WD, ADAM_STEP = 0.001, 0.9, 0.999, 1e-08, 0.01, 10

VMEM_LIMIT = 60 * 1024 * 1024
LANE = 128
ADAM_BLOCK_ELEMS = 1 << 18
COPY_BLOCK_BYTES = 4 * 1024 * 1024
MATMUL_K_TILES =(4096, 2816, 2048, 1024, 512, 256, 128)
MATMUL_VMEM_BUDGET = 48 * 1024 * 1024


def _pick(n, cands=(1024, 512, 256, 128)):
    for c in cands:
        if n % c == 0:
            return c
    return n


def _params(*sem):
    return pltpu.CompilerParams(dimension_semantics=sem, vmem_limit_bytes=VMEM_LIMIT)


def _rb(tm, c, cb=0):
    return pl.BlockSpec((tm, c), lambda i: (i, cb))


def _vec(c, cb=0):
    return pl.BlockSpec((1, c), lambda i: (0, cb))


def _all_gather(x, name, after):
    def body(x_ref, after_ref, out_ref, send_sems, recv_sems, local_sem):
        x_, y_, c_ = lax.axis_index("x"), lax.axis_index("y"), lax.axis_index("c")
        me, sibling = (x_, y_, c_), (x_, y_, 1 - c_)
        chips = [(1 - x_, y_), (x_, 1 - y_), (1 - x_, 1 - y_)]

        def slot(px, py, pc):
            return out_ref.at[4 * px + 2 * py + pc]

        def copy(k, block, to, src=None):
            return pltpu.make_async_remote_copy(
                src_ref=slot(*block) if src is None else src, dst_ref=slot(*block),
                send_sem=send_sems.at[k], recv_sem=recv_sems.at[k], device_id=to, device_id_type=MESH)

        mine = pltpu.make_async_copy(x_ref, slot(*me), local_sem)
        mine.start()
        first = [copy(0, me, sibling, src=x_ref)]
        first += [copy(1 + j, me, (*chip, c_), src=x_ref) for j, chip in enumerate(chips)]
        for cp in first:
            cp.start()
        passed = [copy(4 + j, (*chip, c_), sibling) for j, chip in enumerate(chips)]
        for j, chip in enumerate(chips):
            copy(1 + j, (*chip, c_), me).wait_recv()
            passed[j].start()
        copy(0, sibling, me).wait_recv()
        for j, chip in enumerate(chips):
            copy(4 + j, (*chip, 1 - c_), me).wait_recv()
        for cp in first + passed:
            cp.wait_send()
        mine.wait()

    return pl.pallas_call(
        body, name=name,
        out_shape=jax.ShapeDtypeStruct((N_DEV,) + x.shape, x.dtype),
        in_specs=[pl.BlockSpec(memory_space=pl.ANY), pl.BlockSpec(memory_space=pl.ANY)],
        out_specs=pl.BlockSpec(memory_space=pl.ANY),
        scratch_shapes=[pltpu.SemaphoreType.DMA((7,)), pltpu.SemaphoreType.DMA((7,)), pltpu.SemaphoreType.DMA(())],
    )(x, after)


_HBM = pl.BlockSpec(memory_space=pltpu.HBM)
_SEM = pl.BlockSpec(memory_space=pltpu.SEMAPHORE)
_ANY = pl.BlockSpec(memory_space=pl.ANY)
_EFFECT = pltpu.SideEffectType.DATAFLOW_SIDE_EFFECTING


def _me_index():
    return 4 * lax.axis_index("x") + 2 * lax.axis_index("y") + lax.axis_index("c")


def _window(ref, idx, cols):
    if cols is None:
        return ref.at[idx]
    return ref.at[:, pl.ds(idx * cols if isinstance(idx, int) else pl.multiple_of(idx * cols, LANE), cols)]


def _own_block_only(src, me, name, cols=None, src_cols=False):
    k, n = src.shape if src_cols is False else ((src.shape[0], src_cols) if src_cols else src.shape[1:])
    assert cols is None or (n == cols and cols % LANE == 0)
    out_shape = (N_DEV, k, n) if cols is None else (k, N_DEV * cols)
    limit = COPY_BLOCK_BYTES // (n * src.dtype.itemsize)
    tr = k if k <= limit else _pick(k, tuple(s for s in (1024, 512, 256, 128, 64, 32, 16) if s <= limit))

    def spec(layout):
        if layout is False:
            return pl.BlockSpec((tr, n), lambda i, me_ref: (i, 0))
        if layout is None:
            return pl.BlockSpec((None, tr, n), lambda i, me_ref: (me_ref[0], i, 0))
        return pl.BlockSpec((tr, n), lambda i, me_ref: (i, me_ref[0]))

    def body(me_ref, src_ref, dst_ref):
        dst_ref[...] = src_ref[...]

    return pl.pallas_call(
        body, name=name, out_shape=jax.ShapeDtypeStruct(out_shape, src.dtype),
        grid_spec=pltpu.PrefetchScalarGridSpec(num_scalar_prefetch=1, grid=(k // tr,), in_specs=[spec(src_cols)], out_specs=spec(cols)),
        compiler_params=_params("parallel"))(jnp.reshape(me, (1,)).astype(jnp.int32), src)


def _split_start(bufs, sem_shape, issue, name, after=None):
    n = len(bufs)
    rows, per_row = sem_shape
    sem = lambda sems, i, k: sems.at[i * per_row + k]
    first_out = n + (after is not None)

    def body(*refs):
        issue(refs[:n], refs[first_out], refs[first_out + 1], sem)
        refs[-1][...] = jnp.zeros_like(refs[-1])

    sems = pltpu.SemaphoreType.DMA((rows * per_row,))
    operands = [pltpu.with_memory_space_constraint(b, pltpu.HBM) for b in bufs] + ([] if after is None else [after])
    outs = pl.pallas_call(
        body, name=name,
        out_shape=(sems, sems) + tuple(pltpu.HBM(b.shape, b.dtype) for b in bufs) + (jax.ShapeDtypeStruct((8, LANE), F32),),
        in_specs=(_HBM,) * n + (_ANY,) * (after is not None),
        out_specs=(_SEM, _SEM) + (_HBM,) * n + (pl.BlockSpec(memory_space=pltpu.VMEM),),
        input_output_aliases={i: 2 + i for i in range(n)},
        compiler_params=pltpu.CompilerParams(has_side_effects=_EFFECT),
    )(*operands)
    return outs[0], outs[1], list(outs[2:2 + n]), outs[-1], sem_shape


def _split_wait(started, block_of, after, name, bufs=None, recvs=None, sends=None):
    send_sems, recv_sems, start_bufs, _, sem_shape = started
    bufs = start_bufs if bufs is None else bufs
    n = len(bufs)
    recvs = range(sem_shape[1]) if recvs is None else recvs
    sends = range(sem_shape[1]) if sends is None else sends
    after = tuple(after) if isinstance(after, (tuple, list)) else (after,)

    def body(*refs):
        x_, y_, c_ = lax.axis_index("x"), lax.axis_index("y"), lax.axis_index("c")
        for i in range(sem_shape[0]):
            blk = block_of(refs, i)
            for k in range(sem_shape[1]):
                cp = pltpu.make_async_remote_copy(
                    src_ref=blk, dst_ref=blk, send_sem=refs[n].at[i * sem_shape[1] + k], recv_sem=refs[n + 1].at[i * sem_shape[1] + k],
                    device_id=(x_, y_, c_), device_id_type=MESH)
                if k in sends:
                    cp.wait_send()
                if k in recvs:
                    cp.wait_recv()

    outs = pl.pallas_call(
        body, name=name,
        out_shape=tuple(pltpu.HBM(b.shape, b.dtype) for b in bufs),
        in_specs=(_HBM,) * n + (_SEM, _SEM) + (_ANY,) * len(after), out_specs=(_HBM,) * n,
        input_output_aliases={i: i for i in range(n)},
        compiler_params=pltpu.CompilerParams(has_side_effects=_EFFECT),
    )(*bufs, send_sems, recv_sems, *after)
    return list(outs)


def _first_window(cols):
    return lambda refs, i: _window(refs[i], 0, cols[i])


def _gather_prepare(blocks, me, name, cols=None):
    cols = cols or [None] * len(blocks)
    return [_own_block_only(b, me, "%s_own%d" % (name, i), c) for i, (b, c) in enumerate(zip(blocks, cols))]


def _gather_start(bufs, name, after=None, cols=None):
    cols = cols or [None] * len(bufs)

    def issue(buf_refs, send_sems, recv_sems, sem):
        x_, y_, c_ = lax.axis_index("x"), lax.axis_index("y"), lax.axis_index("c")
        me_idx = 4 * x_ + 2 * y_ + c_
        targets = [(x_, y_, 1 - c_), (1 - x_, y_, c_), (x_, 1 - y_, c_), (1 - x_, 1 - y_, c_)]
        for i, buf in enumerate(buf_refs):
            mine = _window(buf, me_idx, cols[i])
            for k, to in enumerate(targets):
                pltpu.make_async_remote_copy(
                    src_ref=mine, dst_ref=mine, send_sem=sem(send_sems, i, k), recv_sem=sem(recv_sems, i, k),
                    device_id=to, device_id_type=MESH).start()

    return _split_start(bufs, (len(bufs), 4), issue, name, after)


def _forward_start(bufs, name, after=None, cols=None, chips=(0, 1, 2)):
    cols = cols or [None] * len(bufs)

    def issue(buf_refs, send_sems, recv_sems, sem):
        x_, y_, c_ = lax.axis_index("x"), lax.axis_index("y"), lax.axis_index("c")
        places = [(1 - x_, y_), (x_, 1 - y_), (1 - x_, 1 - y_)]
        for i, buf in enumerate(buf_refs):
            for k, chip in enumerate(chips):
                px, py = places[chip]
                landed = _window(buf, 4 * px + 2 * py + c_, cols[i])
                pltpu.make_async_remote_copy(
                    src_ref=landed, dst_ref=landed, send_sem=sem(send_sems, i, k), recv_sem=sem(recv_sems, i, k),
                    device_id=(x_, y_, 1 - c_), device_id_type=MESH).start()

    return _split_start(bufs, (len(bufs), len(chips)), issue, name, after)


def _exchange_start(parts, me, name, cols=None):
    n = len(parts)
    cols = cols or [None] * n

    def issue(refs, send_sems, recv_sems, sem):
        x_, y_, c_ = lax.axis_index("x"), lax.axis_index("y"), lax.axis_index("c")
        me_idx = 4 * x_ + 2 * y_ + c_
        for i in range(n):
            for d in range(1, N_DEV):
                px = 1 - x_ if d & 4 else x_
                py = 1 - y_ if d & 2 else y_
                pc = 1 - c_ if d & 1 else c_
                pltpu.make_async_remote_copy(
                    src_ref=_window(refs[i], 4 * px + 2 * py + pc, cols[i]), dst_ref=refs[n + i].at[me_idx],
                    send_sem=sem(send_sems, i, d - 1), recv_sem=sem(recv_sems, i, d - 1),
                    device_id=(px, py, pc), device_id_type=MESH).start()

    lands = [_own_block_only(p, me, "%s_own%d" % (name, i), src_cols=c) for i, (p, c) in enumerate(zip(parts, cols))]
    return _split_start(list(parts) + lands, (n, N_DEV - 1), issue, name)


def _land_block(refs_offset):
    return lambda refs, i: refs[refs_offset + i].at[0]


def _matmul(a, b, *, name, ta=False, tb=False, res=None, out_dtype=BF16, after=None):
    (kdim, m) = a.shape if ta else a.shape[::-1]
    (n, kdim_b) = b.shape if tb else b.shape[::-1]
    assert kdim == kdim_b, (a.shape, b.shape, ta, tb)
    tm, tn = _pick(m), _pick(n)
    out_bytes = jnp.dtype(out_dtype).itemsize

    def vmem_bytes(tk):
        return 2 * (2 * tk * (tm + tn) + tm * tn * (out_bytes + (4 if res is not None else 0))) + (4 * tm * tn if tk < kdim else 0)

    fitting = [c for c in MATMUL_K_TILES if kdim % c == 0 and vmem_bytes(c) <= MATMUL_VMEM_BUDGET]
    tk = fitting[0] if fitting else kdim
    nk = kdim // tk
    a_spec = pl.BlockSpec((tk, tm), lambda i, j, k: (k, i)) if ta else pl.BlockSpec((tm, tk), lambda i, j, k: (i, k))
    b_spec = pl.BlockSpec((tn, tk), lambda i, j, k: (j, k)) if tb else pl.BlockSpec((tk, tn), lambda i, j, k: (k, j))
    o_spec = pl.BlockSpec((tm, tn), lambda i, j, k: (i, j))
    dims = (((0 if ta else 1,), (1 if tb else 0,)), ((), ()))

    def body(*refs):
        a_ref, b_ref = refs[:2]
        r_ref = None if res is None else refs[2]
        def finish(v):
            if r_ref is not None:
                v = r_ref[...] + v
            o_ref[...] = v.astype(out_dtype)

        part = lax.dot_general(a_ref[...], b_ref[...], dims, preferred_element_type=F32)
        if nk == 1:
            o_ref = refs[-1]
            finish(part)
            return
        o_ref, acc = refs[-2:]
        k = pl.program_id(2)

        @pl.when(k == 0)
        def _():
            acc[...] = part

        @pl.when(k > 0)
        def _():
            acc[...] += part

        @pl.when(k == nk - 1)
        def _():
            finish(acc[...])

    operands = [a, b] + ([] if res is None else [res]) + ([] if after is None else [after])
    in_specs = [a_spec, b_spec] + ([] if res is None else [o_spec]) + ([] if after is None else [_ANY])
    return pl.pallas_call(
        body, name=name, grid=(m // tm, n // tn, nk),
        out_shape=jax.ShapeDtypeStruct((m, n), out_dtype),
        in_specs=in_specs, out_specs=o_spec,
        scratch_shapes=[pltpu.VMEM((tm, tn), F32)] if nk > 1 else [],
        compiler_params=_params("parallel", "parallel", "arbitrary"),
    )(*operands)


def _accumulate(acc, part, k, nk, finish):
    @pl.when(k == 0)
    def _():
        acc[...] = part

    @pl.when(k > 0)
    def _():
        acc[...] += part

    @pl.when(k == nk - 1)
    def _():
        finish(acc[...])


def _mm_z_shard(h, w_sh, slot, z_sh, name, after=()):
    t, d = h.shape
    n = w_sh.shape[2]
    tm, tk = _pick(t), _pick(d, (2048, 1024, 512, 256, 128))
    nk = d // tk
    passed = list(after) + ([] if z_sh is None else [z_sh])

    def body(slot_ref, h_ref, w_ref, *rest):
        o_ref, acc = rest[-2:]

        def finish(v):
            o_ref[...] = v.astype(BF16)

        _accumulate(acc, jnp.dot(h_ref[...], w_ref[...], preferred_element_type=F32), pl.program_id(1), nk, finish)

    return pl.pallas_call(
        body, name=name, out_shape=jax.ShapeDtypeStruct((N_DEV, t, n), BF16),
        grid_spec=pltpu.PrefetchScalarGridSpec(
            num_scalar_prefetch=1, grid=(t // tm, nk),
            in_specs=[pl.BlockSpec((tm, tk), lambda i, k, s: (i, k)),
                      pl.BlockSpec((None, tk, n), lambda i, k, s: (s[0], k, 0))] + [_ANY] * len(passed),
            out_specs=pl.BlockSpec((None, tm, n), lambda i, k, s: (s[0], i, 0)),
            scratch_shapes=[pltpu.VMEM((tm, n), F32)]),
        input_output_aliases={} if z_sh is None else {2 + len(passed): 0},
        compiler_params=_params("parallel", "arbitrary"),
    )(jnp.reshape(slot, (1,)).astype(jnp.int32), h, w_sh, *passed)


def _mm_dw_shards(h, dz_sh, name):
    t, d = h.shape
    n = dz_sh.shape[2]
    tm, tk = _pick(d), _pick(t, (2048, 1024, 512, 256, 128))
    nk = t // tk

    def body(h_ref, dz_ref, o_ref, acc):
        def finish(v):
            o_ref[...] = v.astype(BF16)

        _accumulate(acc, lax.dot_general(h_ref[...], dz_ref[...], TN_DIMS, preferred_element_type=F32), pl.program_id(2), nk, finish)

    return pl.pallas_call(
        body, name=name, grid=(N_DEV, d // tm, nk), out_shape=jax.ShapeDtypeStruct((N_DEV, d, n), BF16),
        in_specs=[pl.BlockSpec((tk, tm), lambda s, i, k: (k, i)), pl.BlockSpec((None, tk, n), lambda s, i, k: (s, k, 0))],
        out_specs=pl.BlockSpec((None, tm, n), lambda s, i, k: (s, i, 0)),
        scratch_shapes=[pltpu.VMEM((tm, n), F32)],
        compiler_params=_params("parallel", "parallel", "arbitrary"))(h, dz_sh)


def _mm_dh_shards(dz_sh, w_sh, name, after):
    t = dz_sh.shape[1]
    d, n = w_sh.shape[1:]
    tm, tn = _pick(t), _pick(d)

    def body(dz_ref, w_ref, after_ref, o_ref, acc):
        def finish(v):
            o_ref[...] = v

        _accumulate(acc, lax.dot_general(dz_ref[...], w_ref[...], NT_DIMS, preferred_element_type=F32), pl.program_id(2), N_DEV, finish)

    return pl.pallas_call(
        body, name=name, grid=(t // tm, d // tn, N_DEV), out_shape=jax.ShapeDtypeStruct((t, d), F32),
        in_specs=[pl.BlockSpec((None, tm, n), lambda i, j, s: (s, i, 0)), pl.BlockSpec((None, tn, n), lambda i, j, s: (s, j, 0)), _ANY],
        out_specs=pl.BlockSpec((tm, tn), lambda i, j, s: (i, j)),
        scratch_shapes=[pltpu.VMEM((tm, tn), F32)],
        compiler_params=_params("parallel", "parallel", "arbitrary"))(dz_sh, w_sh, after)


def _rms_inv(x):
    return lax.rsqrt(jnp.mean(x * x, axis=-1, keepdims=True) + RMS_EPS)


def _rms_fwd(x, g, name, after=None):
    t, d = x.shape
    tm = _pick(t, (256, 128))

    def body(x_ref, g_ref, *rest):
        xv = x_ref[...]
        rest[-1][...] = (xv * _rms_inv(xv) * g_ref[...]).astype(BF16)

    return pl.pallas_call(
        body, name=name, grid=(t // tm,), out_shape=jax.ShapeDtypeStruct((t, d), BF16),
        in_specs=[_rb(tm, d), _vec(d)] + ([] if after is None else [_ANY]), out_specs=_rb(tm, d),
        compiler_params=_params("parallel"))(x, g, *([] if after is None else [after]))


def _rms_bwd_rows(dy, xv, g):
    inv = _rms_inv(xv)
    xhat = xv * inv
    dxhat = dy * g
    dx = inv * (dxhat - xhat * jnp.mean(dxhat * xhat, axis=-1, keepdims=True))
    return dx, dy * xhat


def _rms_bwd(dy, x, g, res, name):
    t, d = x.shape
    tm = _pick(t, (128,))

    def body(dy_ref, x_ref, g_ref, r_ref, dx_ref, dxb_ref, dg_ref):
        dx, dgrow = _rms_bwd_rows(dy_ref[...].astype(F32), x_ref[...], g_ref[...])
        dx = r_ref[...] + dx
        dx_ref[...] = dx
        dxb_ref[...] = dx.astype(BF16)

        @pl.when(pl.program_id(0) == 0)
        def _():
            dg_ref[...] = jnp.zeros_like(dg_ref)

        dg_ref[...] += jnp.sum(dgrow, axis=0, keepdims=True)

    return pl.pallas_call(
        body, name=name, grid=(t // tm,),
        out_shape=(jax.ShapeDtypeStruct((t, d), F32), jax.ShapeDtypeStruct((t, d), BF16), jax.ShapeDtypeStruct((1, d), F32)),
        in_specs=[_rb(tm, d), _rb(tm, d), _vec(d), _rb(tm, d)],
        out_specs=(_rb(tm, d), _rb(tm, d), _vec(d)), compiler_params=_params("arbitrary"))(dy, x, g, res)


def _latent_norm(z_all, g_q, g_kv, q_off, kv_off, name):
    t = z_all.shape[0]
    tm = _pick(t, (256, 128))

    def body(qa_ref, kva_ref, gq_ref, gkv_ref, qn_ref, kvn_ref):
        qa = qa_ref[...].astype(F32)
        qn_ref[...] = (qa * _rms_inv(qa) * gq_ref[...]).astype(BF16)
        kva = kva_ref[...].astype(F32)
        kvn_ref[...] = (kva * _rms_inv(kva) * gkv_ref[...]).astype(BF16)

    return pl.pallas_call(
        body, name=name, grid=(t // tm,),
        out_shape=(jax.ShapeDtypeStruct((t, Q_LORA), BF16), jax.ShapeDtypeStruct((t, KV_LORA), BF16)),
        in_specs=[_rb(tm, Q_LORA, q_off // Q_LORA), _rb(tm, KV_LORA, kv_off // KV_LORA), _vec(Q_LORA), _vec(KV_LORA)],
        out_specs=(_rb(tm, Q_LORA), _rb(tm, KV_LORA)), compiler_params=_params("parallel"))(z_all, z_all, g_q, g_kv)


def _latent_norm_bwd(dqn, dkvn, z_all, g_q, g_kv, q_off, kv_off, name):
    t = z_all.shape[0]
    tm = _pick(t, (256, 128))

    def body(dqn_ref, dkvn_ref, qa_ref, kva_ref, gq_ref, gkv_ref, dqa_ref, dkva_ref, dgq_ref, dgkv_ref):
        dqa, dgq = _rms_bwd_rows(dqn_ref[...].astype(F32), qa_ref[...].astype(F32), gq_ref[...])
        dkva, dgkv = _rms_bwd_rows(dkvn_ref[...].astype(F32), kva_ref[...].astype(F32), gkv_ref[...])
        dqa_ref[...] = dqa.astype(BF16)
        dkva_ref[...] = dkva.astype(BF16)

        @pl.when(pl.program_id(0) == 0)
        def _():
            dgq_ref[...] = jnp.zeros_like(dgq_ref)
            dgkv_ref[...] = jnp.zeros_like(dgkv_ref)

        dgq_ref[...] += jnp.sum(dgq, axis=0, keepdims=True)
        dgkv_ref[...] += jnp.sum(dgkv, axis=0, keepdims=True)

    return pl.pallas_call(
        body, name=name, grid=(t // tm,),
        out_shape=(jax.ShapeDtypeStruct((t, Q_LORA), BF16), jax.ShapeDtypeStruct((t, KV_LORA), BF16),
                   jax.ShapeDtypeStruct((1, Q_LORA), F32), jax.ShapeDtypeStruct((1, KV_LORA), F32)),
        in_specs=[_rb(tm, Q_LORA), _rb(tm, KV_LORA), _rb(tm, Q_LORA, q_off // Q_LORA), _rb(tm, KV_LORA, kv_off // KV_LORA),
                  _vec(Q_LORA), _vec(KV_LORA)],
        out_specs=(_rb(tm, Q_LORA), _rb(tm, KV_LORA), _vec(Q_LORA), _vec(KV_LORA)),
        compiler_params=_params("arbitrary"))(dqn, dkvn, z_all, z_all, g_q, g_kv)


def _rot(xv, cos_k, sin_a, sin_b, sign):
    return xv * cos_k + sign * (pltpu.roll(xv, LANE - 32, 1) * sin_a + pltpu.roll(xv, 32, 1) * sin_b)


def _rope_q(q_raw, tabs, n_heads, sign, out_dtype, name):
    t, w = q_raw.shape
    tm = _pick(t, (256, 128))

    def body(q_ref, cos_ref, sa_ref, sb_ref, o_ref):
        cos_k, sin_a, sin_b = cos_ref[...], sa_ref[...], sb_ref[...]
        for h in range(n_heads):
            lo = h * HEAD_PAD
            o_ref[:, lo:lo + LANE] = q_ref[:, lo:lo + LANE].astype(out_dtype)
            o_ref[:, lo + LANE:lo + HEAD_PAD] = _rot(
                q_ref[:, lo + LANE:lo + HEAD_PAD].astype(F32), cos_k, sin_a, sin_b, sign).astype(out_dtype)

    return pl.pallas_call(
        body, name=name, grid=(t // tm,), out_shape=jax.ShapeDtypeStruct((t, w), out_dtype),
        in_specs=[_rb(tm, w), _rb(tm, LANE), _rb(tm, LANE), _rb(tm, LANE)],
        out_specs=_rb(tm, w), compiler_params=_params("parallel"))(q_raw, *tabs)


def _rope_k(kv, z_all, tabs, n_heads, kr_off, name):
    t = kv.shape[0]
    tm = _pick(t, (256, 128))
    wk = n_heads * QK_NOPE

    def body(kn_ref, kr_ref, cos_ref, sa_ref, sb_ref, o_ref):
        krot = _rot(kr_ref[...].astype(F32), cos_ref[...], sa_ref[...], sb_ref[...], 1.0).astype(BF16)
        for h in range(n_heads):
            o_ref[:, h * HEAD_PAD:h * HEAD_PAD + LANE] = kn_ref[:, h * QK_NOPE:(h + 1) * QK_NOPE]
            o_ref[:, h * HEAD_PAD + LANE:(h + 1) * HEAD_PAD] = krot

    return pl.pallas_call(
        body, name=name, grid=(t // tm,), out_shape=jax.ShapeDtypeStruct((t, n_heads * HEAD_PAD), BF16),
        in_specs=[_rb(tm, wk), _rb(tm, LANE, kr_off // LANE), _rb(tm, LANE), _rb(tm, LANE), _rb(tm, LANE)],
        out_specs=_rb(tm, n_heads * HEAD_PAD), compiler_params=_params("parallel"))(kv, z_all, *tabs)


def _rope_k_bwd(dk_pad, dv, tabs, n_heads, name):
    t = dk_pad.shape[0]
    tm = _pick(t, (256, 128))
    wk = n_heads * QK_NOPE

    def body(dk_ref, dv_ref, cos_ref, sa_ref, sb_ref, dkv_ref, dkr_ref):
        acc = dk_ref[:, LANE:HEAD_PAD]
        dkv_ref[:, 0:QK_NOPE] = dk_ref[:, 0:LANE].astype(BF16)
        for h in range(1, n_heads):
            acc = acc + dk_ref[:, h * HEAD_PAD + LANE:(h + 1) * HEAD_PAD]
            dkv_ref[:, h * QK_NOPE:(h + 1) * QK_NOPE] = dk_ref[:, h * HEAD_PAD:h * HEAD_PAD + LANE].astype(BF16)
        dkv_ref[:, wk:] = dv_ref[...].astype(BF16)
        dkr_ref[...] = _rot(acc, cos_ref[...], sa_ref[...], sb_ref[...], -1.0).astype(BF16)

    return pl.pallas_call(
        body, name=name, grid=(t // tm,),
        out_shape=(jax.ShapeDtypeStruct((t, 2 * wk), BF16), jax.ShapeDtypeStruct((t, LANE), BF16)),
        in_specs=[_rb(tm, n_heads * HEAD_PAD), _rb(tm, wk), _rb(tm, LANE), _rb(tm, LANE), _rb(tm, LANE)],
        out_specs=(_rb(tm, 2 * wk), _rb(tm, LANE)), compiler_params=_params("parallel"))(dk_pad, dv, *tabs)


NT_DIMS = (((1,), (1,)), ((), ()))
TN_DIMS = (((0,), (0,)), ((), ()))


def _softmax_parts(q, k):
    s = lax.dot_general(q, k, NT_DIMS, preferred_element_type=F32)
    e = jnp.exp2((s - jnp.max(s, axis=-1, keepdims=True)) * (SOFTMAX_SCALE * math.log2(math.e)))
    return e, 1.0 / jnp.sum(e, axis=-1, keepdims=True)


def _attn_fwd(q_pad, k_pad, kv, n_heads, name):
    t = q_pad.shape[0]
    tq = _pick(t, (256, 128))

    def body(q_ref, k_ref, v_ref, o_ref):
        e, inv_l = _softmax_parts(q_ref[...], k_ref[...])
        o_ref[...] = (jnp.dot(e.astype(BF16), v_ref[...], preferred_element_type=F32) * inv_l).astype(BF16)

    return pl.pallas_call(
        body, name=name, grid=(n_heads, t // tq),
        out_shape=jax.ShapeDtypeStruct((t, n_heads * V_HEAD), BF16),
        in_specs=[pl.BlockSpec((tq, HEAD_PAD), lambda h, i: (i, h)),
                  pl.BlockSpec((t, HEAD_PAD), lambda h, i: (0, h)),
                  pl.BlockSpec((t, V_HEAD), lambda h, i: (0, n_heads + h))],
        out_specs=pl.BlockSpec((tq, V_HEAD), lambda h, i: (i, h)),
        compiler_params=_params("parallel", "parallel"))(q_pad, k_pad, kv)


def _attn_bwd(q_pad, k_pad, kv, do, n_heads, name):
    t = q_pad.shape[0]
    tq = _pick(t, (256, 128))
    nq = t // tq

    def body(q_ref, k_ref, v_ref, do_ref, dq_ref, dk_ref, dv_ref):
        @pl.when(pl.program_id(1) == 0)
        def _():
            dk_ref[...] = jnp.zeros_like(dk_ref)
            dv_ref[...] = jnp.zeros_like(dv_ref)

        q, k, dout = q_ref[...], k_ref[...], do_ref[...]
        e, inv_l = _softmax_parts(q, k)
        dp = lax.dot_general(dout, v_ref[...], NT_DIMS, preferred_element_type=F32)
        p_dot_dp = inv_l * jnp.sum(e * dp, axis=-1, keepdims=True)
        ds = (e * ((dp - p_dot_dp) * (SOFTMAX_SCALE * inv_l))).astype(BF16)
        dq_ref[...] = jnp.dot(ds, k, preferred_element_type=F32)
        dk_ref[...] += lax.dot_general(ds, q, TN_DIMS, preferred_element_type=F32)
        dv_ref[...] += lax.dot_general(e.astype(BF16), (dout.astype(F32) * inv_l).astype(BF16), TN_DIMS, preferred_element_type=F32)

    return pl.pallas_call(
        body, name=name, grid=(n_heads, nq),
        out_shape=(jax.ShapeDtypeStruct((t, n_heads * HEAD_PAD), F32), jax.ShapeDtypeStruct((t, n_heads * HEAD_PAD), F32),
                   jax.ShapeDtypeStruct((t, n_heads * V_HEAD), F32)),
        in_specs=[pl.BlockSpec((tq, HEAD_PAD), lambda h, i: (i, h)),
                  pl.BlockSpec((t, HEAD_PAD), lambda h, i: (0, h)),
                  pl.BlockSpec((t, V_HEAD), lambda h, i: (0, n_heads + h)),
                  pl.BlockSpec((tq, V_HEAD), lambda h, i: (i, h))],
        out_specs=(pl.BlockSpec((tq, HEAD_PAD), lambda h, i: (i, h)),
                   pl.BlockSpec((t, HEAD_PAD), lambda h, i: (0, h)),
                   pl.BlockSpec((t, V_HEAD), lambda h, i: (0, h))),
        compiler_params=_params("parallel", "arbitrary"))(q_pad, k_pad, kv, do)


def _shift_rows(u, t):
    row = lax.broadcasted_iota(jnp.int32, u.shape, 0)
    prev = jnp.where(row == 0, 0.0, pltpu.roll(u, 1, 0))
    nxt = jnp.where(row == t - 1, 0.0, pltpu.roll(u, t - 1, 0))
    return prev, nxt


def _conv_fwd(z_all, conv_w, cc, name):
    t = z_all.shape[0]
    nb = cc // LANE

    def body(cb_ref, cc_ref, ch_ref, w_ref, y_ref):
        u = cc_ref[...].astype(F32) * ch_ref[...].astype(F32)
        prev, nxt = _shift_rows(u, t)
        w = w_ref[...]
        conv = prev * w[0:1, :] + u * w[1:2, :] + nxt * w[2:3, :]
        y_ref[...] = (cb_ref[...].astype(F32) * conv).astype(BF16)

    col = lambda g: pl.BlockSpec((t, LANE), lambda j: (0, g * nb + j))
    return pl.pallas_call(
        body, name=name, grid=(nb,), out_shape=jax.ShapeDtypeStruct((t, cc), BF16),
        in_specs=[col(0), col(1), col(2), pl.BlockSpec((3, LANE), lambda j: (0, j))],
        out_specs=pl.BlockSpec((t, LANE), lambda j: (0, j)),
        compiler_params=_params("parallel"))(z_all, z_all, z_all, conv_w)


def _conv_bwd(dy, z_all, conv_w, cc, name):
    t = z_all.shape[0]
    nb = cc // LANE

    def body(dy_ref, cb_ref, cc_ref, ch_ref, w_ref, dcb_ref, dcc_ref, dch_ref, dw_ref):
        c_c, c_h = cc_ref[...].astype(F32), ch_ref[...].astype(F32)
        u = c_c * c_h
        prev, nxt = _shift_rows(u, t)
        w = w_ref[...]
        dyv = dy_ref[...].astype(F32)
        dcb_ref[...] = (dyv * (prev * w[0:1, :] + u * w[1:2, :] + nxt * w[2:3, :])).astype(BF16)
        dconv = dyv * cb_ref[...].astype(F32)
        dw_ref[0:1, :] = jnp.sum(dconv * prev, axis=0, keepdims=True)
        dw_ref[1:2, :] = jnp.sum(dconv * u, axis=0, keepdims=True)
        dw_ref[2:3, :] = jnp.sum(dconv * nxt, axis=0, keepdims=True)
        dprev, dnxt = _shift_rows(dconv, t)
        du = dnxt * w[0:1, :] + dconv * w[1:2, :] + dprev * w[2:3, :]
        dcc_ref[...] = (du * c_h).astype(BF16)
        dch_ref[...] = (du * c_c).astype(BF16)

    col = lambda g: pl.BlockSpec((t, LANE), lambda j: (0, g * nb + j))
    one = pl.BlockSpec((t, LANE), lambda j: (0, j))
    wsp = pl.BlockSpec((3, LANE), lambda j: (0, j))
    act = jax.ShapeDtypeStruct((t, cc), BF16)
    return pl.pallas_call(
        body, name=name, grid=(nb,),
        out_shape=(act, act, act, jax.ShapeDtypeStruct((3, cc), F32)),
        in_specs=[one, col(0), col(1), col(2), wsp],
        out_specs=(one, one, one, wsp),
        compiler_params=_params("parallel"))(dy, z_all, z_all, z_all, conv_w)


def _sigmoid(v):
    return 1.0 / (1.0 + jnp.exp(-v))


def _merge_fwd(z_all, b_gate, y_a, y_b, gate_off, name):
    t, d = y_a.shape
    tm = _pick(t, (128,))
    gb = gate_off // d

    def body(za_ref, zb_ref, ba_ref, bb_ref, ya_ref, yb_ref, m_ref):
        ga = _sigmoid(za_ref[...].astype(F32) + ba_ref[...])
        gbv = _sigmoid(zb_ref[...].astype(F32) + bb_ref[...])
        m_ref[...] = (ga * ya_ref[...].astype(F32) + gbv * yb_ref[...].astype(F32)).astype(BF16)

    return pl.pallas_call(
        body, name=name, grid=(t // tm,), out_shape=jax.ShapeDtypeStruct((t, d), BF16),
        in_specs=[_rb(tm, d, gb), _rb(tm, d, gb + 1), _vec(d, 0), _vec(d, 1), _rb(tm, d), _rb(tm, d)],
        out_specs=_rb(tm, d), compiler_params=_params("parallel"))(z_all, z_all, b_gate, b_gate, y_a, y_b)


def _merge_bwd(dm, z_all, b_gate, y_a, y_b, gate_off, name):
    t, d = y_a.shape
    tm = _pick(t, (128,))
    gb = gate_off // d

    def body(dm_ref, za_ref, zb_ref, ba_ref, bb_ref, ya_ref, yb_ref, dya_ref, dyb_ref, dzg_ref, db_ref):
        dmv = dm_ref[...].astype(F32)
        ga = _sigmoid(za_ref[...].astype(F32) + ba_ref[...])
        gbv = _sigmoid(zb_ref[...].astype(F32) + bb_ref[...])
        dya_ref[...] = (dmv * ga).astype(BF16)
        dyb_ref[...] = (dmv * gbv).astype(BF16)
        dza = dmv * ya_ref[...].astype(F32) * (ga * (1.0 - ga))
        dzb = dmv * yb_ref[...].astype(F32) * (gbv * (1.0 - gbv))
        dzg_ref[:, 0:d] = dza.astype(BF16)
        dzg_ref[:, d:2 * d] = dzb.astype(BF16)

        @pl.when(pl.program_id(0) == 0)
        def _():
            db_ref[...] = jnp.zeros_like(db_ref)

        db_ref[:, 0:d] += jnp.sum(dza, axis=0, keepdims=True)
        db_ref[:, d:2 * d] += jnp.sum(dzb, axis=0, keepdims=True)

    act = jax.ShapeDtypeStruct((t, d), BF16)
    return pl.pallas_call(
        body, name=name, grid=(t // tm,),
        out_shape=(act, act, jax.ShapeDtypeStruct((t, 2 * d), BF16), jax.ShapeDtypeStruct((1, 2 * d), F32)),
        in_specs=[_rb(tm, d), _rb(tm, d, gb), _rb(tm, d, gb + 1), _vec(d, 0), _vec(d, 1), _rb(tm, d), _rb(tm, d)],
        out_specs=(_rb(tm, d), _rb(tm, d), _rb(tm, 2 * d), _vec(2 * d)),
        compiler_params=_params("arbitrary"))(dm, z_all, z_all, b_gate, b_gate, y_a, y_b)


def _swiglu_fwd(gate, up, name, after):
    t, f = gate.shape
    tm = _pick(t, (128,))

    def body(g_ref, u_ref, after_ref, a_ref):
        g = g_ref[...].astype(F32)
        a_ref[...] = (g * _sigmoid(g) * u_ref[...].astype(F32)).astype(BF16)

    return pl.pallas_call(
        body, name=name, grid=(t // tm,), out_shape=jax.ShapeDtypeStruct((t, f), BF16),
        in_specs=[_rb(tm, f), _rb(tm, f), _ANY], out_specs=_rb(tm, f), compiler_params=_params("parallel"))(gate, up, after)


def _swiglu_bwd(dact, gate, up, name):
    t, f = gate.shape
    tm = _pick(t, (128,))

    def body(da_ref, g_ref, u_ref, dg_ref, du_ref):
        g, da = g_ref[...].astype(F32), da_ref[...].astype(F32)
        sg = _sigmoid(g)
        dg_ref[...] = (da * u_ref[...].astype(F32) * (sg * (1.0 + g * (1.0 - sg)))).astype(BF16)
        du_ref[...] = (da * (g * sg)).astype(BF16)

    act = jax.ShapeDtypeStruct((t, f), BF16)
    return pl.pallas_call(
        body, name=name, grid=(t // tm,), out_shape=(act, act),
        in_specs=[_rb(tm, f)] * 3, out_specs=(_rb(tm, f), _rb(tm, f)), compiler_params=_params("parallel"))(dact, gate, up)


def _loss_head(x2, target, g, name):
    t, d = x2.shape
    tm = _pick(t, (128,))

    def body(x_ref, t_ref, g_ref, loss_ref, dx_ref, dxb_ref, dg_ref):
        xv, gv = x_ref[...], g_ref[...]
        err = xv * _rms_inv(xv) * gv - t_ref[...]
        dx, dgrow = _rms_bwd_rows(err * (1.0 / d), xv, gv)
        dx_ref[...] = dx
        dxb_ref[...] = dx.astype(BF16)

        @pl.when(pl.program_id(0) == 0)
        def _():
            loss_ref[...] = jnp.zeros_like(loss_ref)
            dg_ref[...] = jnp.zeros_like(dg_ref)

        loss_ref[...] += (0.5 / d) * jnp.sum(jnp.sum(err * err, axis=1, keepdims=True), axis=0, keepdims=True)
        dg_ref[...] += jnp.sum(dgrow, axis=0, keepdims=True)

    return pl.pallas_call(
        body, name=name, grid=(t // tm,),
        out_shape=(jax.ShapeDtypeStruct((1, 1), F32), jax.ShapeDtypeStruct((t, d), F32),
                   jax.ShapeDtypeStruct((t, d), BF16), jax.ShapeDtypeStruct((1, d), F32)),
        in_specs=[_rb(tm, d), _rb(tm, d), _vec(d)],
        out_specs=(pl.BlockSpec((1, 1), lambda i: (0, 0)), _rb(tm, d), _rb(tm, d), _vec(d)),
        compiler_params=_params("arbitrary"))(x2, target, g)


def _adamw(parts, w, m, v, name):
    r, c = w.shape
    cp = parts.shape[2]
    assert parts.shape[1] >= r and cp >= c
    tr = r if r * c <= ADAM_BLOCK_ELEMS else _pick(r, tuple(s for s in (512, 256, 128, 64, 32, 16, 8) if s * c <= ADAM_BLOCK_ELEMS))
    assert tr == parts.shape[1] or tr % 8 == 0

    def body(p_ref, w_ref, m_ref, v_ref, g_ref, d_ref, nm_ref, nv_ref):
        g = p_ref[0, :, 0:c].astype(F32)
        for s in range(1, N_DEV):
            g = g + p_ref[s, :, 0:c].astype(F32)
        nm = ADAM_B1 * m_ref[...] + (1.0 - ADAM_B1) * g
        nv = ADAM_B2 * v_ref[...] + (1.0 - ADAM_B2) * (g * g)
        m_hat = nm / (1.0 - ADAM_B1 ** ADAM_STEP)
        v_hat = nv / (1.0 - ADAM_B2 ** ADAM_STEP)
        g_ref[...] = g
        d_ref[...] = -ADAM_LR * (m_hat / (jnp.sqrt(v_hat) + ADAM_EPS) + ADAM_WD * w_ref[...])
        nm_ref[...] = nm
        nv_ref[...] = nv

    blk = pl.BlockSpec((tr, c), lambda i: (i, 0))
    out = jax.ShapeDtypeStruct((r, c), F32)
    return pl.pallas_call(
        body, name=name, grid=(r // tr,), out_shape=(out, out, out, out),
        in_specs=[pl.BlockSpec((N_DEV, tr, cp), lambda i: (0, i, 0)), blk, blk, blk],
        out_specs=(blk, blk, blk, blk), compiler_params=_params("parallel"))(parts, w, m, v)


def _cols_of(g):
    return jnp.transpose(g, (1, 0, 2)).reshape(g.shape[1], N_DEV * g.shape[2])


def _cols_slice(pieces, lo, hi):
    out, off = [], 0
    for p in pieces:
        a, b = max(lo, off), min(hi, off + p.shape[1])
        if a < b:
            out.append(p[:, a - off:b - off])
        off += p.shape[1]
    return out[0] if len(out) == 1 else jnp.concatenate(out, axis=1)


def _col_parts(dw):
    k, n8 = dw.shape
    return jnp.transpose(dw.reshape(k, N_DEV, n8 // N_DEV), (1, 0, 2))


def kernel(x, positions, g_mix, w_in, b_gate, conv_w, g_q_a, w_q_b, g_kv_a, w_kv_b, w_branch, w_out, g_ffn, w_ffn_gate, w_ffn_up, w_ffn_down, g_final, loss_target, m_g_mix, m_w_in, m_b_gate, m_conv_w, m_g_q_a, m_w_q_b, m_g_kv_a, m_w_kv_b, m_w_branch, m_w_out, m_g_ffn, m_w_ffn_gate, m_w_ffn_up, m_w_ffn_down, m_g_final, v_g_mix, v_w_in, v_b_gate, v_conv_w, v_g_q_a, v_w_q_b, v_g_kv_a, v_w_kv_b, v_w_branch, v_w_out, v_g_ffn, v_w_ffn_gate, v_w_ffn_up, v_w_ffn_down, v_g_final):
    given = dict(locals())
    xs = x[0]
    t, d = xs.shape
    cc = d // 2
    n_heads = cc // V_HEAD
    in_cols = N_DEV * w_in.shape[2]
    q_off, kv_off, kr_off = 3 * cc, 3 * cc + Q_LORA, 3 * cc + Q_LORA + KV_LORA
    head_cols = kr_off + QK_ROPE
    head_pad = -(-(kr_off + LANE) // 1024) * 1024
    assert in_cols == head_cols + 2 * d and q_off % Q_LORA == 0 and kv_off % KV_LORA == 0 and kr_off % LANE == 0
    fs = w_ffn_gate.shape[2]
    fsp = -(-fs // LANE) * LANE
    ffp = N_DEV * fsp
    me = _me_index()
    bf = lambda a: a.astype(BF16)
    one_slot = lambda refs, i: refs[i].at[0]

    x_, y_, c_ = lax.axis_index("x"), lax.axis_index("y"), lax.axis_index("c")
    place = lambda px, py, pc: 4 * px + 2 * py + pc
    chips = [(1 - x_, y_), (x_, 1 - y_), (1 - x_, 1 - y_)]
    mix_cols = [w_q_b.shape[2], w_kv_b.shape[2], w_branch.shape[3], None]
    ffn_cols = [fsp, fsp]
    g_in = _gather_start(_gather_prepare([bf(w_in[0]), conv_w[0]], me, "ag1_start_in"), "ag1_start_in")
    zero = g_in[3][0, 0]
    bf_later = lambda a: (a + zero).astype(BF16)
    h = _rms_fwd(xs, g_mix, "rms_mix", after=g_in[3])
    bufs = g_in[2]
    z_sh = _mm_z_shard(h, bufs[0], me, None, "mm_z_own")
    bufs = _split_wait(g_in, one_slot, z_sh, "ag1_wait_in_sib", bufs=bufs, recvs=(0,), sends=())
    z_sh = _mm_z_shard(h, bufs[0], place(x_, y_, 1 - c_), z_sh, "mm_z_sib")
    mix_bufs = _gather_prepare([bf_later(w_q_b[0]), bf_later(w_kv_b[0]), bf_later(w_branch[0].reshape(2 * cc, -1)), bf_later(w_out[0])],
                               me, "ag1_start_mix", mix_cols)
    gate_up_bufs = _gather_prepare([bf_later(jnp.pad(w_ffn_gate[0], ((0, 0), (0, fsp - fs)))),
                                    bf_later(jnp.pad(w_ffn_up[0], ((0, 0), (0, fsp - fs))))], me, "ag1_start_gate_up", ffn_cols)
    down_bufs = _gather_prepare([bf_later(jnp.pad(w_ffn_down[0], ((0, fsp - fs), (0, 0))))], me, "ag1_start_down")
    prepared = tuple(mix_bufs + gate_up_bufs + down_bufs)
    passed_on = []
    for j, (px, py) in enumerate(chips):
        last = j == len(chips) - 1
        bufs = _split_wait(g_in, one_slot, (z_sh,) + (prepared if j == 0 else ()), "ag1_wait_in_chip%d" % j, bufs=bufs,
                           recvs=(1 + j,), sends=range(4) if last else ())
        f = _forward_start(bufs, "ag2_start_in_chip%d" % j, chips=(j,))
        passed_on.append(f)
        bufs, tokens = f[2], (f[3],)
        if last:
            g_mix_w = _gather_start(mix_bufs, "ag1_start_mix", after=f[3], cols=mix_cols)
            tokens = (g_mix_w[3],)
        z_sh = _mm_z_shard(h, bufs[0], place(px, py, c_), z_sh, "mm_z_chip%d" % j, after=tokens)
        for jj in ((0, 1) if j == 1 else (2,) if last else ()):
            bufs = _split_wait(passed_on[jj], one_slot, z_sh, "ag2_wait_in_chip%d" % jj, bufs=bufs)
            z_sh = _mm_z_shard(h, bufs[0], place(*chips[jj], 1 - c_), z_sh, "mm_z_chip%d_sib" % jj)
    w_in_g, cw_g = bufs
    cw = _cols_of(cw_g)
    z_pieces = [z_sh[s] for s in range(N_DEV)]
    z_head = jnp.concatenate([_cols_slice(z_pieces, 0, head_cols), jnp.zeros((t, head_pad - head_cols), BF16)], axis=1)
    zg = _cols_slice(z_pieces, head_cols, in_cols)

    inv_freq = ROPE_THETA ** (-jnp.arange(0, QK_ROPE, 2, dtype=F32) / QK_ROPE)
    ang = positions[0].astype(F32)[:, None] * inv_freq[None, :]
    cos, sin = jnp.cos(ang), jnp.sin(ang)
    z32, z64 = jnp.zeros((t, 32), F32), jnp.zeros((t, 64), F32)
    tabs = (jnp.concatenate([cos, cos, jnp.ones((t, 64), F32)], axis=1),
            jnp.concatenate([-sin, z32, z64], axis=1),
            jnp.concatenate([z32, sin, z64], axis=1))

    f_mix = _forward_start(_split_wait(g_mix_w, _first_window(mix_cols), zg, "ag1_wait_mix"), "ag2_start_mix", cols=mix_cols)
    g_gate_up = _gather_start(gate_up_bufs, "ag1_start_gate_up", after=f_mix[3], cols=ffn_cols)
    y_a = _conv_fwd(z_head, cw, cc, "conv_fwd")
    qn, kvn = _latent_norm(z_head, g_q_a, g_kv_a, q_off, kv_off, "latent_norm")
    wq_full, wkv_full, wbr, wo_g = _split_wait(f_mix, _first_window(mix_cols), g_gate_up[3], "ag2_wait_mix")
    wq = wq_full.reshape(Q_LORA, n_heads, QK_NOPE + QK_ROPE)
    wq_pad = jnp.pad(wq, ((0, 0), (0, 0), (0, HEAD_PAD - QK_NOPE - QK_ROPE))).reshape(Q_LORA, n_heads * HEAD_PAD)
    wkv = wkv_full.reshape(KV_LORA, n_heads, 2, QK_NOPE)
    wkv_perm = jnp.transpose(wkv, (0, 2, 1, 3)).reshape(KV_LORA, 2 * n_heads * QK_NOPE)
    wb_a, wb_b = wbr[:cc], wbr[cc:]
    wo = wo_g.reshape(d, d)
    q_pad = _rope_q(_matmul(qn, wq_pad, name="mm_q", out_dtype=F32), tabs, n_heads, 1.0, BF16, "rope_q")
    kv = _matmul(kvn, wkv_perm, name="mm_kv")
    k_pad = _rope_k(kv, z_head, tabs, n_heads, kr_off, "rope_k")
    y_b = _attn_fwd(q_pad, k_pad, kv, n_heads, "attn_fwd")
    f_gate_up = _forward_start(_split_wait(g_gate_up, _first_window(ffn_cols), y_b, "ag1_wait_gate_up"), "ag2_start_gate_up", cols=ffn_cols)
    g_down = _gather_start(down_bufs, "ag1_start_down", after=f_gate_up[3])
    ybr_a = _matmul(y_a, wb_a, name="mm_br_a", after=g_down[3])
    ybr_b = _matmul(y_b, wb_b, name="mm_br_b")
    merged = _merge_fwd(zg, b_gate, ybr_a, ybr_b, 0, "merge_fwd")
    x1 = _matmul(merged, wo, name="mm_out", res=xs, out_dtype=F32)
    h2 = _rms_fwd(x1, g_ffn, "rms_ffn")
    wg, wu = _split_wait(f_gate_up, _first_window(ffn_cols), h2, "ag2_wait_gate_up")
    gate = _matmul(h2, wg, name="mm_gate")
    up = _matmul(h2, wu, name="mm_up")
    f_down = _forward_start(_split_wait(g_down, one_slot, up, "ag1_wait_down"), "ag2_start_down")
    act = _swiglu_fwd(gate, up, "swiglu_fwd", after=f_down[3])
    wd = _split_wait(f_down, one_slot, act, "ag2_wait_down")[0].reshape(ffp, d)
    x2 = _matmul(act, wd, name="mm_down", res=x1, out_dtype=F32)
    loss_part, dx2, dx2b, dg_final = _loss_head(x2, loss_target[0], g_final.reshape(1, d), "loss_head")

    dact = _matmul(dx2b, wd, tb=True, name="mm_d_act")
    dwd = _matmul(act, dx2b, ta=True, name="mm_dw_down")
    r_down = _exchange_start([dwd.reshape(N_DEV, fsp, d)], me, "rs_start_down")
    dgate, dup = _swiglu_bwd(dact, gate, up, "swiglu_bwd")
    dwg = _matmul(h2, dgate, ta=True, name="mm_dw_gate", after=r_down[3])
    dwu = _matmul(h2, dup, ta=True, name="mm_dw_up")
    r_gate_up = _exchange_start([dwg, dwu], me, "rs_start_gate_up", cols=[fsp, fsp])
    dh2 = _matmul(dgate, wg, tb=True, name="mm_d_h2_gate", out_dtype=F32, after=r_gate_up[3])
    dh2 = _matmul(dup, wu, tb=True, name="mm_d_h2_up", res=dh2, out_dtype=F32)
    dx1, dx1b, dg_ffn = _rms_bwd(dh2, x1, g_ffn, dx2, "rms_ffn_bwd")
    dwo = _matmul(merged, dx1b, ta=True, name="mm_dw_out")
    r_out = _exchange_start([dwo.reshape(N_DEV, d // N_DEV, d)], me, "rs_start_out")
    dmerged = _matmul(dx1b, wo, tb=True, name="mm_d_merged", after=r_out[3])
    dybr_a, dybr_b, dzg, db_gate = _merge_bwd(dmerged, zg, b_gate, ybr_a, ybr_b, 0, "merge_bwd")
    dwb_a = _matmul(y_a, dybr_a, ta=True, name="mm_dw_br_a")
    dwb_b = _matmul(y_b, dybr_b, ta=True, name="mm_dw_br_b")
    r_br = _exchange_start([dwb_a, dwb_b], me, "rs_start_branch", cols=[d // N_DEV] * 2)
    dy_a = _matmul(dybr_a, wb_a, tb=True, name="mm_d_y_a", after=r_br[3])
    dy_b = _matmul(dybr_b, wb_b, tb=True, name="mm_d_y_b")
    dq_pad, dk_pad, dv = _attn_bwd(q_pad, k_pad, kv, dy_b, n_heads, "attn_bwd")
    dq_raw = _rope_q(dq_pad, tabs, n_heads, -1.0, BF16, "rope_q_bwd")
    dkv, dkr = _rope_k_bwd(dk_pad, dv, tabs, n_heads, "rope_k_bwd")
    dwq = _matmul(qn, dq_raw, ta=True, name="mm_dw_q")
    dwkv = _matmul(kvn, dkv, ta=True, name="mm_dw_kv")
    dwq_full = dwq.reshape(Q_LORA, n_heads, HEAD_PAD)[:, :, :QK_NOPE + QK_ROPE].reshape(Q_LORA, -1)
    dwkv_full = jnp.transpose(dwkv.reshape(KV_LORA, 2, n_heads, QK_NOPE), (0, 2, 1, 3)).reshape(KV_LORA, -1)
    r_qkv = _exchange_start([_col_parts(dwq_full), _col_parts(dwkv_full)], me, "rs_start_q_kv")
    dqn = _matmul(dq_raw, wq_pad, tb=True, name="mm_d_qn", after=r_qkv[3])
    dkvn = _matmul(dkv, wkv_perm, tb=True, name="mm_d_kvn")
    dqa, dkva, dg_q, dg_kv = _latent_norm_bwd(dqn, dkvn, z_head, g_q_a, g_kv_a, q_off, kv_off, "latent_norm_bwd")
    dcb, dcc, dch, dcw = _conv_bwd(dy_a, z_head, cw, cc, "conv_bwd")
    dz_pieces = [dcb, dcc, dch, dqa, dkva, dkr[:, :QK_ROPE], dzg]
    n_sh = in_cols // N_DEV
    dz_sh = jnp.stack([_cols_slice(dz_pieces, s * n_sh, (s + 1) * n_sh) for s in range(N_DEV)], axis=0)
    r_in = _exchange_start([_mm_dw_shards(h, dz_sh, "mm_dw_in")], me, "rs_start_in")
    dh = _mm_dh_shards(dz_sh, w_in_g, "mm_d_h", r_in[3])
    dx, _, dg_mix = _rms_bwd(dh, xs, g_mix, dx1, "rms_mix_bwd")

    results = {}
    last = dx

    def update(started, targets):
        nonlocal last
        n = len(targets)
        recvs = _split_wait(started, _land_block(n), last, "rs_wait_" + targets[0][0])[n:]
        for recv, (key, w_, m_, v_) in zip(recvs, targets):
            results[key] = _adamw(recv, w_, m_, v_, "adamw_" + key)
            last = results[key][0]

    shard = lambda wname: (wname,) + tuple(given[p + wname][0] for p in ("", "m_", "v_"))
    branch = lambda b: ("w_branch_%d" % b,) + tuple(given[p + "w_branch"][0, b] for p in ("", "m_", "v_"))
    update(r_down, [shard("w_ffn_down")])
    update(r_gate_up, [shard("w_ffn_gate"), shard("w_ffn_up")])
    update(r_out, [shard("w_out")])
    update(r_br, [branch(0), branch(1)])
    results["w_branch"] = tuple(jnp.stack([results["w_branch_0"][k], results["w_branch_1"][k]]) for k in range(4))
    update(r_qkv, [shard("w_q_b"), shard("w_kv_b")])

    small = [("g_mix", dg_mix), ("b_gate", db_gate), ("g_q_a", dg_q), ("g_kv_a", dg_kv), ("g_ffn", dg_ffn),
             ("g_final", dg_final), ("conv_w", dcw.reshape(1, 3 * cc))]
    packed = _all_gather(jnp.concatenate([p for _, p in small], axis=1), "ag_small_grads", last)
    off = 0
    for wname, p in small:
        n = p.shape[1]
        parts = packed[:, :, off:off + n]
        off += n
        if wname == "conv_w":
            width = conv_w.shape[2]
            parts = lax.dynamic_slice_in_dim(parts.reshape(N_DEV, 3, cc), me * width, width, axis=2)
            flat = (3, width)
        else:
            flat = (1, n)
        results[wname] = _adamw(parts, given[wname].reshape(flat), given["m_" + wname].reshape(flat),
                                given["v_" + wname].reshape(flat), "adamw_" + wname)
        last = results[wname][0]
    update(r_in, [shard("w_in")])

    loss = lax.psum(loss_part[0, 0], MESH_AXES)
    order = ["g_mix", "w_in", "b_gate", "conv_w", "g_q_a", "w_q_b", "g_kv_a", "w_kv_b", "w_branch", "w_out", "g_ffn",
             "w_ffn_gate", "w_ffn_up", "w_ffn_down", "g_final"]
    out = [loss, dx[None]]
    for k in range(4):
        out += [results[n][k].reshape(given[n].shape) for n in order]
    return tuple(out)
```
